```python
import math
import jax, jax.numpy as jnp
from jax import lax
import numpy as np

D_MODEL = 1024
BATCH = 8
SEQ = 4096
DEPTH = 4

GRID_W = 64
CTX_LEN = 256
EPS = 1e-6
N_EVEN = (DEPTH + 1) // 2
N_ODD = DEPTH // 2
MIX_WIDTH = D_MODEL

FNET_WIDTH = D_MODEL // 4
FNET_GROUP_DIM = 64
FNET_GROUPS = FNET_WIDTH // FNET_GROUP_DIM
NA_WIDTH = MIX_WIDTH - FNET_WIDTH
NA_HEAD_DIM = 64
NA_HEADS = NA_WIDTH // NA_HEAD_DIM
NA_ROWS = 8
NA_COLS = 16
EVEN_SPLITS = (FNET_WIDTH, FNET_WIDTH, NA_WIDTH, NA_WIDTH, NA_WIDTH, NA_WIDTH)
EVEN_IN = sum(EVEN_SPLITS)

SGU_CHUNK = 128
SGU_WIDTH = MIX_WIDTH // 2
SGU_GROUP_DIM = 128
SGU_GROUPS = SGU_WIDTH // SGU_GROUP_DIM
S5_WIDTH = MIX_WIDTH - SGU_WIDTH
S5_GROUP_DIM = 16
S5_GROUPS = S5_WIDTH // S5_GROUP_DIM
S5_STATE = 64
ODD_SPLITS = (SGU_WIDTH, SGU_WIDTH, SGU_WIDTH, S5_WIDTH, S5_WIDTH)
ODD_IN = sum(ODD_SPLITS)

kernel_name = "hybrid_fnet_natten_gmlp_s5_dit"


def split_cols(p, sizes):
    idx = np.cumsum(sizes)[:-1].tolist()
    return jnp.split(p, idx, axis=-1)


def rms_norm(x, g):
    xf = x.astype(jnp.float32)
    y = xf * lax.rsqrt(jnp.mean(xf * xf, axis=-1, keepdims=True) + EPS)
    return (y * g.astype(jnp.float32)).astype(x.dtype)


def layer_norm(x, g):
    xf = x.astype(jnp.float32)
    xc = xf - jnp.mean(xf, axis=-1, keepdims=True)
    y = xc * lax.rsqrt(jnp.mean(xc * xc, axis=-1, keepdims=True) + EPS)
    return (y * g.astype(jnp.float32)).astype(x.dtype)


def heads(t):
    return t.reshape(t.shape[0], t.shape[1], NA_HEADS, NA_HEAD_DIM)


def fourier_mix(v):
    b, l, _ = v.shape
    vg = v.astype(jnp.float32).reshape(b, l, FNET_GROUPS, FNET_GROUP_DIM)
    y = jnp.fft.fftn(vg, axes=(1, 3), norm="ortho").real
    return y.reshape(b, l, FNET_WIDTH).astype(v.dtype)


def neighbourhood_attention(q, k, v, k_ctx, v_ctx, rpb):
    b, s, h, dh = q.shape
    rows = s // GRID_W
    kh = min(NA_ROWS, rows)
    scale = dh ** -0.5
    r = jnp.arange(rows)
    r0 = jnp.clip(r - kh // 2, 0, rows - kh)
    key_rows = r0[:, None] + jnp.arange(kh)[None, :]
    cq = jnp.arange(GRID_W)
    c0 = jnp.clip(cq - NA_COLS // 2, 0, GRID_W - NA_COLS)
    ck = jnp.arange(GRID_W)
    col_in = (ck[None, :] >= c0[:, None]) & (ck[None, :] < c0[:, None] + NA_COLS)

    qg = q.reshape(b, rows, GRID_W, h, dh)
    kg = k.reshape(b, rows, GRID_W, h, dh)[:, key_rows]
    vg = v.reshape(b, rows, GRID_W, h, dh)[:, key_rows]

    dr = key_rows - r[:, None] + (NA_ROWS - 1)
    dc = jnp.clip(ck[None, :] - cq[:, None] + (NA_COLS - 1), 0, 2 * NA_COLS - 2)
    bias = rpb[:, dr[:, :, None, None], dc[None, None, :, :]]
    bias = bias.transpose(1, 0, 3, 2, 4).astype(jnp.float32)

    s_loc = jnp.einsum('brqhd,brkwhd->brhqkw', qg, kg).astype(jnp.float32) * scale + bias
    s_loc = jnp.where(col_in[:, None, :], s_loc, -1e30)
    s_ctx = jnp.einsum('brqhd,bchd->brhqc', qg, k_ctx).astype(jnp.float32) * scale
    n_loc = kh * GRID_W
    logits = jnp.concatenate([s_loc.reshape(b, rows, h, GRID_W, n_loc), s_ctx], axis=-1)
    p = jax.nn.softmax(logits, axis=-1).astype(v.dtype)
    p_loc = p[..., :n_loc].reshape(b, rows, h, GRID_W, kh, GRID_W)
    p_ctx = p[..., n_loc:]
    o = (jnp.einsum('brhqkw,brkwhd->brqhd', p_loc, vg)
         + jnp.einsum('brhqc,bchd->brqhd', p_ctx, v_ctx))
    return o.reshape(b, s, h * dh)


def context_attention(q, k, v):
    b, l, h, dh = q.shape
    s = jnp.einsum('bqhd,bkhd->bhqk', q, k).astype(jnp.float32) * (dh ** -0.5)
    p = jax.nn.softmax(s, axis=-1).astype(v.dtype)
    return jnp.einsum('bhqk,bkhd->bqhd', p, v).reshape(b, l, h * dh)


def spatial_gating(u, v, w_s, b_s, g):
    b, l, _ = u.shape
    n = l // SGU_CHUNK
    vc = layer_norm(v, g).reshape(b, n, SGU_CHUNK, SGU_GROUPS, SGU_GROUP_DIM)
    mixed = jnp.einsum('gpq,bnqgc->bnpgc', w_s, vc) + b_s.T[None, None, :, :, None]
    return u * mixed.reshape(b, l, SGU_WIDTH).astype(u.dtype)


def s5_discretise(lam_re, lam_im, log_step, b_re, b_im):
    lam = lax.complex(jnp.minimum(lam_re.astype(jnp.float32), -1e-4), lam_im.astype(jnp.float32))
    dt = jnp.exp(log_step.astype(jnp.float32))[:, None]
    lam_bar = jnp.exp(lam * dt)
    b_bar = ((lam_bar - 1.0) / lam)[..., None] * lax.complex(b_re.astype(jnp.float32), b_im.astype(jnp.float32))
    return lam_bar, b_bar


def linear_scan(lam_bar, bu, h0, reverse):
    if h0 is not None:
        first = -1 if reverse else 0
        bu = bu.at[:, first].add(lam_bar[None] * h0)
    a = jnp.broadcast_to(lam_bar, (1, bu.shape[1]) + lam_bar.shape)

    def combine(x, y):
        a1, b1 = x
        a2, b2 = y
        return a1 * a2, a2 * b1 + b2

    _, h = lax.associative_scan(combine, (a, bu), reverse=reverse, axis=1)
    return h


def s5_direction(u_lat, u_ctx, lam_re, lam_im, log_step, b_re, b_im, c_re, c_im, reverse, with_ctx_out):
    lam_bar, b_bar = s5_discretise(lam_re, lam_im, log_step, b_re, b_im)
    b_r, b_i = jnp.real(b_bar), jnp.imag(b_bar)
    cr, ci = c_re.astype(jnp.float32), c_im.astype(jnp.float32)

    def drive(u):
        ug = u.astype(jnp.float32).reshape(u.shape[0], u.shape[1], S5_GROUPS, S5_GROUP_DIM)
        return lax.complex(jnp.einsum('blgm,gpm->blgp', ug, b_r), jnp.einsum('blgm,gpm->blgp', ug, b_i))

    def read(h):
        y = jnp.einsum('gmp,blgp->blgm', cr, jnp.real(h)) - jnp.einsum('gmp,blgp->blgm', ci, jnp.imag(h))
        return y.reshape(h.shape[0], h.shape[1], S5_WIDTH)

    h_ctx = linear_scan(lam_bar, drive(u_ctx), None, reverse)
    h_end = h_ctx[:, 0] if reverse else h_ctx[:, -1]
    h_lat = linear_scan(lam_bar, drive(u_lat), h_end, reverse)
    return read(h_lat), (read(h_ctx) if with_ctx_out else None)


def s5_readout(y, u, d_skip, w_glu, b_glu):
    z = jax.nn.gelu(y + d_skip.astype(jnp.float32) * u.astype(jnp.float32))
    z = z @ w_glu.astype(jnp.float32) + b_glu.astype(jnp.float32)
    val, gt = jnp.split(z, 2, axis=-1)
    return (val * jax.nn.sigmoid(gt)).astype(u.dtype)


def even_mixer(hl, hc, w_in, w_out, rpb, with_ctx_out):
    fa_l, ga_l, q_l, k_l, v_l, gb_l = split_cols(hl @ w_in, EVEN_SPLITS)
    fa_c, ga_c, q_c, k_c, v_c, gb_c = split_cols(hc @ w_in, EVEN_SPLITS)
    k_c, v_c = heads(k_c), heads(v_c)
    a_l = fourier_mix(fa_l) * jax.nn.silu(ga_l)
    n_l = neighbourhood_attention(heads(q_l), heads(k_l), heads(v_l), k_c, v_c, rpb) * jax.nn.silu(gb_l)
    y_l = jnp.concatenate([a_l, n_l], axis=-1) @ w_out
    if not with_ctx_out:
        return y_l, None
    a_c = fourier_mix(fa_c) * jax.nn.silu(ga_c)
    n_c = context_attention(heads(q_c), k_c, v_c) * jax.nn.silu(gb_c)
    y_c = jnp.concatenate([a_c, n_c], axis=-1) @ w_out
    return y_l, y_c


def odd_mixer(hl, hc, w_in, w_out, sgu_w, sgu_b, sgu_g, lam_re, lam_im, log_step,
              b_re, b_im, c_re, c_im, d_skip, w_glu, b_glu, with_ctx_out):
    u_l, v_l, gc_l, s_l, gd_l = split_cols(hl @ w_in, ODD_SPLITS)
    u_c, v_c, gc_c, s_c, gd_c = split_cols(hc @ w_in, ODD_SPLITS)
    sg_l = spatial_gating(u_l, v_l, sgu_w, sgu_b, sgu_g) * jax.nn.silu(gc_l)
    yf_l, yf_c = s5_direction(s_l, s_c, lam_re[0], lam_im[0], log_step[0], b_re[0], b_im[0],
                              c_re[0], c_im[0], False, with_ctx_out)
    yb_l, yb_c = s5_direction(s_l, s_c, lam_re[1], lam_im[1], log_step[1], b_re[1], b_im[1],
                              c_re[1], c_im[1], True, with_ctx_out)
    ss_l = s5_readout(yf_l + yb_l, s_l, d_skip, w_glu, b_glu) * jax.nn.silu(gd_l)
    y_l = jnp.concatenate([sg_l, ss_l], axis=-1) @ w_out
    if not with_ctx_out:
        return y_l, None
    sg_c = spatial_gating(u_c, v_c, sgu_w, sgu_b, sgu_g) * jax.nn.silu(gc_c)
    ss_c = s5_readout(yf_c + yb_c, s_c, d_skip, w_glu, b_glu) * jax.nn.silu(gd_c)
    y_c = jnp.concatenate([sg_c, ss_c], axis=-1) @ w_out
    return y_l, y_c


def setup_inputs(seed: int = 0) -> dict:
    key = jax.random.key(seed)
    ks = jax.random.split(key, 26)
    f32 = jnp.float32
    D = D_MODEL
    nrm = lambda k, shape, s: jax.random.normal(k, shape, f32) * s
    lam_im_init = jnp.pi * jnp.arange(S5_STATE, dtype=f32)
    return {
        "x": nrm(ks[0], (BATCH, SEQ, D), 1.0),
        "c": nrm(ks[1], (BATCH, D), 1.0),
        "ctx": nrm(ks[2], (BATCH, CTX_LEN, D), 1.0),
        "c_ctx": nrm(ks[3], (D,), 1.0),
        "w_ada": nrm(ks[4], (DEPTH, D, 3 * D), 0.5 * D ** -0.5),
        "b_ada": nrm(ks[5], (DEPTH, 3 * D), 0.01),
        "pre_g": 1.0 + nrm(ks[6], (DEPTH, D), 0.02),
        "post_g": 1.0 + nrm(ks[7], (DEPTH, D), 0.02),
        "w_in_even": nrm(ks[8], (N_EVEN, D, EVEN_IN), D ** -0.5),
        "w_out_even": nrm(ks[9], (N_EVEN, MIX_WIDTH, D), MIX_WIDTH ** -0.5),
        "na_rpb": nrm(ks[10], (N_EVEN, NA_HEADS, 2 * NA_ROWS - 1, 2 * NA_COLS - 1), 0.02),
        "w_in_odd": nrm(ks[11], (N_ODD, D, ODD_IN), D ** -0.5),
        "w_out_odd": nrm(ks[12], (N_ODD, MIX_WIDTH, D), MIX_WIDTH ** -0.5),
        "sgu_w": nrm(ks[13], (N_ODD, SGU_GROUPS, SGU_CHUNK, SGU_CHUNK), SGU_CHUNK ** -0.5),
        "sgu_b": 1.0 + nrm(ks[14], (N_ODD, SGU_GROUPS, SGU_CHUNK), 0.01),
        "sgu_g": 1.0 + nrm(ks[15], (N_ODD, SGU_WIDTH), 0.02),
        "s5_lam_re": -0.5 + nrm(ks[16], (N_ODD, 2, S5_GROUPS, S5_STATE), 0.01),
        "s5_lam_im": lam_im_init + nrm(ks[17], (N_ODD, 2, S5_GROUPS, S5_STATE), 0.01),
        "s5_log_step": jax.random.uniform(ks[18], (N_ODD, 2, S5_GROUPS), f32,
                                          minval=math.log(1e-3), maxval=math.log(1e-1)),
        "s5_b_re": nrm(ks[19], (N_ODD, 2, S5_GROUPS, S5_STATE, S5_GROUP_DIM), (2 * S5_GROUP_DIM) ** -0.5),
        "s5_b_im": nrm(ks[20], (N_ODD, 2, S5_GROUPS, S5_STATE, S5_GROUP_DIM), (2 * S5_GROUP_DIM) ** -0.5),
        "s5_c_re": nrm(ks[21], (N_ODD, 2, S5_GROUPS, S5_GROUP_DIM, S5_STATE), (2 * S5_STATE) ** -0.5),
        "s5_c_im": nrm(ks[22], (N_ODD, 2, S5_GROUPS, S5_GROUP_DIM, S5_STATE), (2 * S5_STATE) ** -0.5),
        "s5_d": nrm(ks[23], (N_ODD, S5_WIDTH), 1.0),
        "glu_w": nrm(ks[24], (N_ODD, S5_WIDTH, 2 * S5_WIDTH), S5_WIDTH ** -0.5),
        "glu_b": nrm(ks[25], (N_ODD, 2 * S5_WIDTH), 0.01),
    }


def reference(x, c, ctx, c_ctx, w_ada, b_ada, pre_g, post_g, w_in_even, w_out_even, na_rpb,
              w_in_odd, w_out_odd, sgu_w, sgu_b, sgu_g, s5_lam_re, s5_lam_im, s5_log_step,
              s5_b_re, s5_b_im, s5_c_re, s5_c_im, s5_d, glu_w, glu_b):
    xl = x
    xc = ctx
    for i in range(DEPTH):
        last = i == DEPTH - 1
        mod = jax.nn.silu(c) @ w_ada[i] + b_ada[i]
        shift, scale, gate = jnp.split(mod[:, None, :], 3, axis=-1)
        mod_c = jax.nn.silu(c_ctx) @ w_ada[i] + b_ada[i]
        shift_c, scale_c, gate_c = jnp.split(mod_c, 3)
        hl = rms_norm(xl, pre_g[i]) * (1.0 + scale) + shift
        hc = rms_norm(xc, pre_g[i]) * (1.0 + scale_c) + shift_c
        if i % 2 == 0:
            j = i // 2
            yl, yc = even_mixer(hl, hc, w_in_even[j], w_out_even[j], na_rpb[j], not last)
        else:
            j = i // 2
            yl, yc = odd_mixer(hl, hc, w_in_odd[j], w_out_odd[j], sgu_w[j], sgu_b[j], sgu_g[j],
                               s5_lam_re[j], s5_lam_im[j], s5_log_step[j], s5_b_re[j], s5_b_im[j],
                               s5_c_re[j], s5_c_im[j], s5_d[j], glu_w[j], glu_b[j], not last)
        xl = xl + gate * rms_norm(yl.astype(xl.dtype), post_g[i])
        if not last:
            xc = xc + gate_c * rms_norm(yc.astype(xc.dtype), post_g[i])
    return xl
```

```python
import functools

import numpy as np
import jax
import jax.numpy as jnp
from jax import lax
from jax.experimental import pallas as pl
from jax.experimental.pallas import tpu as pltpu

F32 = jnp.float32
BF16 = jnp.bfloat16

EPS = 1e-6
GRID_W = 64
FNET_WIDTH = 256
FNET_GROUP_DIM = 64
NA_WIDTH = 768
NA_HEAD_DIM = 64
NA_HEADS = 12
NA_ROWS = 8
NA_COLS = 16
SGU_CHUNK = 128
SGU_WIDTH = 512
SGU_GROUPS = 4
S5_WIDTH = 512
S5_GROUP_DIM = 16
S5_GROUPS = 32
S5_STATE = 64
S5_CHUNK = 16
S5_TILE = S5_CHUNK * S5_GROUP_DIM

V7X_VMEM_BYTES = 64 * 1024 * 1024
VMEM_LIMIT = 48 * 1024 * 1024
ROW_TILE = 512


def _cparams(sem):
    return pltpu.CompilerParams(dimension_semantics=sem, vmem_limit_bytes=VMEM_LIMIT)


def _silu(x):
    return x * jax.nn.sigmoid(x)


def _dot(a, b):
    return jnp.dot(a, b, preferred_element_type=F32)


def _dot_nt(a, b):
    return lax.dot_general(a, b, (((1,), (1,)), ((), ())), preferred_element_type=F32)


def _ada_kernel(c_ref, w_ref, b_ref, o_ref):
    c = c_ref[...]
    o_ref[0] = jnp.dot(_silu(c), w_ref[0], preferred_element_type=F32,
                       precision=lax.Precision.HIGHEST) + b_ref[0]


def _ada_mod(cond, w_ada, b_ada):
    depth, d, n = w_ada.shape
    rows = cond.shape[0]
    tn = 1024
    return pl.pallas_call(
        _ada_kernel,
        grid=(depth, n // tn),
        in_specs=[pl.BlockSpec((rows, d), lambda i, j: (0, 0)),
                  pl.BlockSpec((1, d, tn), lambda i, j: (i, 0, j)),
                  pl.BlockSpec((1, 1, tn), lambda i, j: (i, 0, j))],
        out_specs=pl.BlockSpec((1, rows, tn), lambda i, j: (i, 0, j)),
        out_shape=jax.ShapeDtypeStruct((depth, rows, n), F32),
        compiler_params=_cparams(("parallel", "parallel")),
        name="ada_mod",
    )(cond, w_ada, b_ada.reshape(depth, 1, n))


def _norm_mod(x, g, scale, shift):
    y = x * lax.rsqrt(jnp.mean(x * x, axis=-1, keepdims=True) + EPS)
    return (y * g) * (1.0 + scale) + shift


def _post(y, post_g, gate, x):
    yn = y * lax.rsqrt(jnp.mean(y * y, axis=-1, keepdims=True) + EPS)
    return x + gate * (yn * post_g)


def _even_in_kernel(x_ref, g_ref, sc_ref, sh_ref, w_ref, bd_ref,
                    zc_ref, zs_ref, ga_ref, q_ref, k_ref, v_ref, gb_ref):
    h = _norm_mod(x_ref[0], g_ref[...], sc_ref[0], sh_ref[0]).astype(BF16)
    f0, f1 = 0, FNET_WIDTH
    fa = _dot(h, w_ref[:, f0:f1]).astype(BF16)
    z = _dot(fa, bd_ref[...])
    zc_ref[...] = z[:, :FNET_WIDTH].astype(BF16)
    zs_ref[...] = z[:, FNET_WIDTH:].astype(BF16)
    o = f1
    ga_ref[0] = _dot(h, w_ref[:, o:o + FNET_WIDTH]).astype(BF16)
    o += FNET_WIDTH
    q_ref[0] = (_dot(h, w_ref[:, o:o + NA_WIDTH]) * (NA_HEAD_DIM ** -0.5)).astype(BF16)
    o += NA_WIDTH
    k_ref[0] = _dot(h, w_ref[:, o:o + NA_WIDTH]).astype(BF16)
    o += NA_WIDTH
    v_ref[0] = _dot(h, w_ref[:, o:o + NA_WIDTH]).astype(BF16)
    o += NA_WIDTH
    gb_ref[0] = _dot(h, w_ref[:, o:o + NA_WIDTH]).astype(BF16)


def _even_in(x, pre_g, scale, shift, w, bd):
    b, l, d = x.shape
    n = w.shape[1]
    tm = min(ROW_TILE, l)
    tok = lambda width: pl.BlockSpec((1, tm, width), lambda bi, i: (bi, i, 0))
    zspec = pl.BlockSpec((tm, FNET_WIDTH), lambda bi, i: (i, bi))
    vec = pl.BlockSpec((1, 1, d), lambda bi, i: (bi, 0, 0))
    sd = lambda width: jax.ShapeDtypeStruct((b, l, width), BF16)
    zsd = jax.ShapeDtypeStruct((l, b * FNET_WIDTH), BF16)
    return pl.pallas_call(
        _even_in_kernel,
        grid=(b, l // tm),
        in_specs=[tok(d), pl.BlockSpec((1, d), lambda bi, i: (0, 0)), vec, vec,
                  pl.BlockSpec((d, n), lambda bi, i: (0, 0)),
                  pl.BlockSpec(bd.shape, lambda bi, i: (0, 0))],
        out_specs=[zspec, zspec, tok(FNET_WIDTH), tok(NA_WIDTH), tok(NA_WIDTH), tok(NA_WIDTH), tok(NA_WIDTH)],
        out_shape=[zsd, zsd, sd(FNET_WIDTH), sd(NA_WIDTH), sd(NA_WIDTH), sd(NA_WIDTH), sd(NA_WIDTH)],
        compiler_params=_cparams(("parallel", "parallel")),
        name="even_in",
    )(x, pre_g.reshape(1, d), scale, shift, w, bd)


def _dft_kernel(c_ref, s_ref, zc_ref, zs_ref, ga_ref, o_ref, acc_ref, *, nb, norm):
    kk = pl.program_id(1)

    @pl.when(kk == 0)
    def _():
        acc_ref[...] = jnp.zeros_like(acc_ref)

    acc_ref[...] += _dot(c_ref[...], zc_ref[...]) + _dot(s_ref[...], zs_ref[...])

    @pl.when(kk == pl.num_programs(1) - 1)
    def _():
        for bi in range(nb):
            g = ga_ref[bi].astype(F32)
            y = acc_ref[:, bi * FNET_WIDTH:(bi + 1) * FNET_WIDTH] * norm
            o_ref[bi] = (y * _silu(g)).astype(BF16)


def _dft_mix(cm, sm, zc, zs, ga):
    b, l, _ = ga.shape
    tm = min(1024, l)
    tk = min(512, l)
    nc = b * FNET_WIDTH
    kern = functools.partial(_dft_kernel, nb=b, norm=float((l * FNET_GROUP_DIM) ** -0.5))
    return pl.pallas_call(
        kern,
        grid=(l // tm, l // tk),
        in_specs=[pl.BlockSpec((tm, tk), lambda i, k: (i, k)),
                  pl.BlockSpec((tm, tk), lambda i, k: (i, k)),
                  pl.BlockSpec((tk, nc), lambda i, k: (k, 0)),
                  pl.BlockSpec((tk, nc), lambda i, k: (k, 0)),
                  pl.BlockSpec((b, tm, FNET_WIDTH), lambda i, k: (0, i, 0))],
        out_specs=pl.BlockSpec((b, tm, FNET_WIDTH), lambda i, k: (0, i, 0)),
        out_shape=jax.ShapeDtypeStruct((b, l, FNET_WIDTH), BF16),
        scratch_shapes=[pltpu.VMEM((tm, nc), F32)],
        compiler_params=_cparams(("parallel", "arbitrary")),
        name="dft_mix",
    )(cm, sm, zc, zs, ga)


def _dft_mats(l):
    j = jnp.arange(l, dtype=jnp.int32)
    r = (j[:, None] * j[None, :]) % l
    ang = r.astype(F32) * (2.0 * np.pi / l)
    return jnp.cos(ang).astype(BF16), (-jnp.sin(ang)).astype(BF16)


def _group_dft_mat():
    n = FNET_GROUP_DIM
    j = np.arange(n)
    ang = 2.0 * np.pi * ((j[:, None] * j[None, :]) % n) / n
    eye = np.eye(FNET_WIDTH // n)
    mat = np.concatenate([np.kron(eye, np.cos(ang)), np.kron(eye, np.sin(ang))], axis=1)
    return jnp.asarray(mat, F32).astype(BF16)


def _na_kernel(q_ref, k_ref, v_ref, kc_ref, vc_ref, gb_ref, bias_ref, o_ref, acc_ref, *, rows, kh):
    r = pl.program_id(1)
    r0 = jnp.clip(r - kh // 2, 0, rows - kh)
    start = pl.multiple_of(r0 * GRID_W, GRID_W)
    nloc = kh * GRID_W
    for h in range(NA_HEADS):
        lo, hi = h * NA_HEAD_DIM, (h + 1) * NA_HEAD_DIM
        qh = q_ref[0, :, lo:hi]
        kl = k_ref[0, pl.ds(start, nloc), lo:hi]
        vl = v_ref[0, pl.ds(start, nloc), lo:hi]
        s_loc = _dot_nt(qh, kl) + bias_ref[0, h]
        s_ctx = _dot_nt(qh, kc_ref[0, :, lo:hi])
        m = jnp.maximum(jnp.max(s_loc, axis=-1, keepdims=True), jnp.max(s_ctx, axis=-1, keepdims=True))
        p_loc = jnp.exp(s_loc - m)
        p_ctx = jnp.exp(s_ctx - m)
        den = jnp.sum(p_loc, axis=-1, keepdims=True) + jnp.sum(p_ctx, axis=-1, keepdims=True)
        o = _dot(p_loc.astype(BF16), vl) + _dot(p_ctx.astype(BF16), vc_ref[0, :, lo:hi])
        acc_ref[:, lo:hi] = o / den
    g = gb_ref[0].astype(F32)
    o_ref[0] = (acc_ref[...] * _silu(g)).astype(BF16)


def _na_attention(q, k, v, kc, vc, gb, bias):
    b, l, w = q.shape
    lc = kc.shape[1]
    rows = l // GRID_W
    kh = min(NA_ROWS, rows)

    def bias_idx(bi, r):
        r0 = jnp.clip(r - kh // 2, 0, rows - kh)
        return (r0 - r + (NA_ROWS - 1), 0, 0, 0)

    row = pl.BlockSpec((1, GRID_W, w), lambda bi, r: (bi, r, 0))
    full = lambda n: pl.BlockSpec((1, n, w), lambda bi, r: (bi, 0, 0))
    kern = functools.partial(_na_kernel, rows=rows, kh=kh)
    return pl.pallas_call(
        kern,
        grid=(b, rows),
        in_specs=[row, full(l), full(l), full(lc), full(lc), row,
                  pl.BlockSpec((1, NA_HEADS, GRID_W, kh * GRID_W), bias_idx)],
        out_specs=row,
        out_shape=jax.ShapeDtypeStruct((b, l, w), BF16),
        scratch_shapes=[pltpu.VMEM((GRID_W, w), F32)],
        compiler_params=_cparams(("parallel", "arbitrary")),
        name="na_attention",
    )(q, k, v, kc, vc, gb, bias)


def _na_bias(rpb, kh):
    cq = np.arange(GRID_W)
    ck = np.arange(GRID_W)
    c0 = np.clip(cq - NA_COLS // 2, 0, GRID_W - NA_COLS)
    col_in = (ck[None, :] >= c0[:, None]) & (ck[None, :] < c0[:, None] + NA_COLS)
    dc = np.clip(ck[None, :] - cq[:, None] + (NA_COLS - 1), 0, 2 * NA_COLS - 2)
    dr = np.arange(NA_ROWS)[:, None] + np.arange(kh)[None, :]
    dr = np.minimum(dr, 2 * NA_ROWS - 2)
    t = rpb.astype(F32)[:, dr[:, :, None, None], dc[None, None, :, :]]
    t = jnp.where(col_in[None, None, None], t, -1e30)
    t = t.transpose(1, 0, 3, 2, 4)
    return t.reshape(NA_ROWS, NA_HEADS, GRID_W, kh * GRID_W)


def _ctx_attn_kernel(q_ref, k_ref, v_ref, gb_ref, o_ref, acc_ref):
    for h in range(NA_HEADS):
        lo, hi = h * NA_HEAD_DIM, (h + 1) * NA_HEAD_DIM
        s = _dot_nt(q_ref[0, :, lo:hi], k_ref[0, :, lo:hi])
        m = jnp.max(s, axis=-1, keepdims=True)
        p = jnp.exp(s - m)
        den = jnp.sum(p, axis=-1, keepdims=True)
        acc_ref[:, lo:hi] = _dot(p.astype(BF16), v_ref[0, :, lo:hi]) / den
    g = gb_ref[0].astype(F32)
    o_ref[0] = (acc_ref[...] * _silu(g)).astype(BF16)


def _ctx_attention(q, k, v, gb):
    b, lc, w = q.shape
    spec = pl.BlockSpec((1, lc, w), lambda bi: (bi, 0, 0))
    return pl.pallas_call(
        _ctx_attn_kernel,
        grid=(b,),
        in_specs=[spec, spec, spec, spec],
        out_specs=spec,
        out_shape=jax.ShapeDtypeStruct((b, lc, w), BF16),
        scratch_shapes=[pltpu.VMEM((lc, w), F32)],
        compiler_params=_cparams(("parallel",)),
        name="ctx_attention",
    )(q, k, v, gb)


def _even_out_kernel(a_ref, n_ref, w_ref, pg_ref, gate_ref, x_ref, o_ref):
    y = _dot(a_ref[0], w_ref[:FNET_WIDTH, :]) + _dot(n_ref[0], w_ref[FNET_WIDTH:, :])
    o_ref[0] = _post(y, pg_ref[...], gate_ref[0], x_ref[0])


def _even_out(a, n, w, post_g, gate, x):
    b, l, d = x.shape
    tm = min(ROW_TILE, l)
    tok = lambda width: pl.BlockSpec((1, tm, width), lambda bi, i: (bi, i, 0))
    return pl.pallas_call(
        _even_out_kernel,
        grid=(b, l // tm),
        in_specs=[tok(FNET_WIDTH), tok(NA_WIDTH),
                  pl.BlockSpec(w.shape, lambda bi, i: (0, 0)),
                  pl.BlockSpec((1, d), lambda bi, i: (0, 0)),
                  pl.BlockSpec((1, 1, d), lambda bi, i: (bi, 0, 0)),
                  tok(d)],
        out_specs=tok(d),
        out_shape=jax.ShapeDtypeStruct((b, l, d), F32),
        compiler_params=_cparams(("parallel", "parallel")),
        name="even_out",
    )(a, n, w, post_g.reshape(1, d), gate, x)


def _odd_in_kernel(x_ref, g_ref, sc_ref, sh_ref, w_ref, ws_ref, bs_ref, sg_g_ref,
                   sg_ref, s_ref, gd_ref, *, tm):
    h = _norm_mod(x_ref[0], g_ref[...], sc_ref[0], sh_ref[0]).astype(BF16)
    wd = SGU_WIDTH
    u = _dot(h, w_ref[:, 0:wd])
    v = _dot(h, w_ref[:, wd:2 * wd])
    gc = _dot(h, w_ref[:, 2 * wd:3 * wd])
    s_ref[0] = _dot(h, w_ref[:, 3 * wd:3 * wd + S5_WIDTH]).astype(BF16)
    gd_ref[0] = _dot(h, w_ref[:, 3 * wd + S5_WIDTH:]).astype(BF16)
    vc = v - jnp.mean(v, axis=-1, keepdims=True)
    vn = (vc * lax.rsqrt(jnp.mean(vc * vc, axis=-1, keepdims=True) + EPS) * sg_g_ref[...]).astype(BF16)
    gate = u * _silu(gc)
    gw = SGU_WIDTH // SGU_GROUPS
    for j in range(tm // SGU_CHUNK):
        rs = slice(j * SGU_CHUNK, (j + 1) * SGU_CHUNK)
        for g in range(SGU_GROUPS):
            cs = slice(g * gw, (g + 1) * gw)
            mixed = _dot(ws_ref[g], vn[rs, cs]) + bs_ref[:, cs]
            sg_ref[0, rs, cs] = (gate[rs, cs] * mixed).astype(BF16)


def _odd_in(x, pre_g, scale, shift, w, ws, bs_full, sgu_g):
    b, l, d = x.shape
    n = w.shape[1]
    tm = min(ROW_TILE, l)
    tok = lambda width: pl.BlockSpec((1, tm, width), lambda bi, i: (bi, i, 0))
    vec = pl.BlockSpec((1, 1, d), lambda bi, i: (bi, 0, 0))
    const2 = lambda a: pl.BlockSpec(a.shape, lambda bi, i: (0,) * a.ndim)
    sd = jax.ShapeDtypeStruct((b, l, SGU_WIDTH), BF16)
    g2 = sgu_g.reshape(1, SGU_WIDTH)
    return pl.pallas_call(
        functools.partial(_odd_in_kernel, tm=tm),
        grid=(b, l // tm),
        in_specs=[tok(d), pl.BlockSpec((1, d), lambda bi, i: (0, 0)), vec, vec,
                  pl.BlockSpec((d, n), lambda bi, i: (0, 0)), const2(ws), const2(bs_full), const2(g2)],
        out_specs=[tok(SGU_WIDTH), tok(S5_WIDTH), tok(S5_WIDTH)],
        out_shape=[sd, sd, sd],
        compiler_params=_cparams(("parallel", "parallel")),
        name="odd_in",
    )(x, pre_g.reshape(1, d), scale, shift, w, ws, bs_full, g2)


def _gmm_kernel(*refs, n_in):
    xs, ws, o_ref = refs[:n_in], refs[n_in:2 * n_in], refs[2 * n_in]
    acc = _dot(xs[0][...].astype(BF16), ws[0][0])
    for x_ref, w_ref in zip(xs[1:], ws[1:]):
        acc += _dot(x_ref[...].astype(BF16), w_ref[0])
    o_ref[...] = acc.astype(o_ref.dtype)


def _grouped_matmul(xs, ws, out_dtype, name):
    r = xs[0].shape[0]
    groups = ws[0].shape[0]
    n_in = len(xs)
    in_specs = [pl.BlockSpec((r, S5_TILE), lambda g: (0, g)) for _ in xs]
    in_specs += [pl.BlockSpec((1, S5_TILE, S5_TILE), lambda g: (g, 0, 0)) for _ in ws]
    return pl.pallas_call(
        functools.partial(_gmm_kernel, n_in=n_in),
        grid=(groups,),
        in_specs=in_specs,
        out_specs=pl.BlockSpec((r, S5_TILE), lambda g: (0, g)),
        out_shape=jax.ShapeDtypeStruct((r, groups * S5_TILE), out_dtype),
        compiler_params=_cparams(("parallel",)),
        name=name,
    )(*xs, *ws)


REC_LANES = 512


def _rec_kernel(s_ref, a_ref, h_ref, *, nb, n_ctx, n_tot):
    half = S5_TILE // 2
    npieces = REC_LANES // S5_TILE
    fwd_lane = lax.broadcasted_iota(jnp.int32, (1, half), 1) < S5_STATE
    coef = [(a_ref[:, p * S5_TILE:p * S5_TILE + half], a_ref[:, p * S5_TILE + half:(p + 1) * S5_TILE])
            for p in range(npieces)]

    def step(c, carry, forward):
        rows = pl.ds(pl.multiple_of(c * nb, nb), nb)
        out = []
        for p in range(npieces):
            hre, him = carry[2 * p], carry[2 * p + 1]
            are, aim = coef[p]
            re_sl = slice(p * S5_TILE, p * S5_TILE + half)
            im_sl = slice(p * S5_TILE + half, (p + 1) * S5_TILE)
            if forward:
                h_ref[rows, re_sl] = jnp.where(fwd_lane, hre, 0.0)
                h_ref[rows, im_sl] = jnp.where(fwd_lane, him, 0.0)
            else:
                h_ref[rows, re_sl] = jnp.where(fwd_lane, h_ref[rows, re_sl], hre)
                h_ref[rows, im_sl] = jnp.where(fwd_lane, h_ref[rows, im_sl], him)
            sre, sim = s_ref[rows, re_sl], s_ref[rows, im_sl]
            out.append(are * hre - aim * him + sre)
            out.append(are * him + aim * hre + sim)
        return tuple(out)

    zero = tuple(jnp.zeros((nb, half), F32) for _ in range(2 * npieces))
    lax.fori_loop(0, n_tot, lambda c, carry: step(c, carry, True), zero)
    carry = lax.fori_loop(0, n_ctx, lambda i, carry: step(n_ctx - 1 - i, carry, False), zero)
    lax.fori_loop(0, n_tot - n_ctx, lambda i, carry: step(n_tot - 1 - i, carry, False), carry)


def _chunk_recurrence(s, a16, nb, n_ctx, n_tot):
    r, lanes = s.shape
    return pl.pallas_call(
        functools.partial(_rec_kernel, nb=nb, n_ctx=n_ctx, n_tot=n_tot),
        grid=(lanes // REC_LANES,),
        in_specs=[pl.BlockSpec((r, REC_LANES), lambda j: (0, j)),
                  pl.BlockSpec((1, REC_LANES), lambda j: (0, j))],
        out_specs=pl.BlockSpec((r, REC_LANES), lambda j: (0, j)),
        out_shape=jax.ShapeDtypeStruct((r, lanes), F32),
        compiler_params=_cparams(("parallel",)),
        name="s5_recurrence",
    )(s, a16)


def _cmul(ar, ai, br, bi):
    return ar * br - ai * bi, ar * bi + ai * br


def _s5_matrices(lam_re, lam_im, log_step, b_re, b_im, c_re, c_im):
    t = S5_CHUNK
    hp = lax.Precision.HIGHEST
    taus = jnp.arange(t + 1, dtype=F32)[:, None, None]
    er, ei, wr, wi, a_re, a_im, ks = [], [], [], [], [], [], []
    for d in range(2):
        lr = jnp.minimum(lam_re[d].astype(F32), -1e-4)
        li = lam_im[d].astype(F32)
        dt = jnp.exp(log_step[d].astype(F32))[:, None]
        mag = jnp.exp(lr * dt * taus)
        pr, pi = mag * jnp.cos(li * dt * taus), mag * jnp.sin(li * dt * taus)
        den = lr * lr + li * li
        qr = ((pr[1] - 1.0) * lr + pi[1] * li) / den
        qi = (pi[1] * lr - (pr[1] - 1.0) * li) / den
        bbr, bbi = _cmul(qr[..., None], qi[..., None], b_re[d].astype(F32), b_im[d].astype(F32))
        e_r, e_i = _cmul(pr[..., None], pi[..., None], bbr[None], bbi[None])
        cr, ci = c_re[d].astype(F32), c_im[d].astype(F32)
        w_r, w_i = _cmul(cr[None], ci[None], pr[:, :, None, :], pi[:, :, None, :])
        k = (jnp.einsum('gmp,tgpn->tgmn', cr, e_r[:t], precision=hp)
             - jnp.einsum('gmp,tgpn->tgmn', ci, e_i[:t], precision=hp))
        er.append(e_r); ei.append(e_i); wr.append(w_r); wi.append(w_i); ks.append(k)
        a_re.append(pr[t]); a_im.append(pi[t])
    g = S5_GROUPS
    idx = np.arange(t)
    dlt = idx[None, :] - idx[:, None]
    kf = ks[0][np.clip(dlt, 0, t - 1)] * jnp.asarray(dlt >= 0, F32)[:, :, None, None, None]
    kb = ks[1][np.clip(-dlt, 0, t - 1)] * jnp.asarray(dlt <= 0, F32)[:, :, None, None, None]
    m = (kf + kb).transpose(2, 0, 4, 1, 3).reshape(g, S5_TILE, S5_TILE)
    rows = lambda e: e.transpose(1, 0, 3, 2).reshape(g, S5_TILE, S5_STATE)
    p_mat = jnp.concatenate([rows(er[0][:t][::-1]), rows(er[1][:t]),
                             rows(ei[0][:t][::-1]), rows(ei[1][:t])], axis=-1)
    cols = lambda w: w.transpose(1, 3, 0, 2).reshape(g, S5_STATE, S5_TILE)
    q_mat = jnp.concatenate([cols(wr[0][1:]), cols(wr[1][1:][::-1]),
                             -cols(wi[0][1:]), -cols(wi[1][1:][::-1])], axis=1)
    a16 = jnp.concatenate([a_re[0], a_re[1], a_im[0], a_im[1]], axis=-1).reshape(1, g * S5_TILE)
    return m.astype(BF16), p_mat.astype(BF16), q_mat.astype(BF16), a16


def _to_chunks(s):
    b, l, _ = s.shape
    c = l // S5_CHUNK
    u = s.reshape(b, c, S5_CHUNK, S5_GROUPS, S5_GROUP_DIM).transpose(1, 0, 3, 2, 4)
    return u.reshape(c * b, S5_GROUPS * S5_TILE)


def _from_chunks(y, b):
    r = y.shape[0]
    c = r // b
    t = y.reshape(c, b, S5_GROUPS, S5_CHUNK, S5_GROUP_DIM).transpose(1, 0, 3, 2, 4)
    return t.reshape(b, c * S5_CHUNK, S5_WIDTH)


def _s5_scan(s_l, s_c, mats):
    m, p_mat, q_mat, a16 = mats
    b = s_l.shape[0]
    lc = s_c.shape[1]
    u = jnp.concatenate([_to_chunks(s_c), _to_chunks(s_l)], axis=0)
    n_ctx = lc // S5_CHUNK
    n_tot = u.shape[0] // b
    summ = _grouped_matmul([u], [p_mat], F32, "s5_summary")
    h_in = _chunk_recurrence(summ, a16, b, n_ctx, n_tot)
    y = _grouped_matmul([u, h_in], [m, q_mat], BF16, "s5_apply")
    y = _from_chunks(y, b)
    return y[:, lc:], y[:, :lc]


def _gelu_tanh(x):
    return 0.5 * x * (1.0 + jnp.tanh(float(np.sqrt(2.0 / np.pi)) * (x + 0.044715 * (x * x * x))))


def _odd_out_kernel(sg_ref, y_ref, s_ref, gd_ref, dsk_ref, wg_ref, bg_ref, w_ref, pg_ref, gate_ref, x_ref, o_ref):
    z = _gelu_tanh(y_ref[0].astype(F32) + dsk_ref[...] * s_ref[0].astype(F32))
    zz = _dot(z.astype(BF16), wg_ref[...]) + bg_ref[...]
    gd = gd_ref[0].astype(F32)
    ss = zz[:, :S5_WIDTH] * jax.nn.sigmoid(zz[:, S5_WIDTH:]) * _silu(gd)
    y = _dot(sg_ref[0], w_ref[:SGU_WIDTH, :]) + _dot(ss.astype(BF16), w_ref[SGU_WIDTH:, :])
    o_ref[0] = _post(y, pg_ref[...], gate_ref[0], x_ref[0])


def _odd_out(sg, y, s, gd, d_skip, wg, bg, w, post_g, gate, x):
    b, l, d = x.shape
    tm = min(ROW_TILE, l)
    tok = lambda width: pl.BlockSpec((1, tm, width), lambda bi, i: (bi, i, 0))
    const2 = lambda a: pl.BlockSpec(a.shape, lambda bi, i: (0,) * a.ndim)
    dsk = d_skip.reshape(1, S5_WIDTH).astype(F32)
    bg2 = bg.reshape(1, 2 * S5_WIDTH).astype(F32)
    pg2 = post_g.reshape(1, d)
    return pl.pallas_call(
        _odd_out_kernel,
        grid=(b, l // tm),
        in_specs=[tok(SGU_WIDTH), tok(S5_WIDTH), tok(S5_WIDTH), tok(S5_WIDTH),
                  const2(dsk), const2(wg), const2(bg2), const2(w), const2(pg2),
                  pl.BlockSpec((1, 1, d), lambda bi, i: (bi, 0, 0)), tok(d)],
        out_specs=tok(d),
        out_shape=jax.ShapeDtypeStruct((b, l, d), F32),
        compiler_params=_cparams(("parallel", "parallel")),
        name="odd_out",
    )(sg, y, s, gd, dsk, wg, bg2, w, pg2, gate, x)


def kernel(x, c, ctx, c_ctx, w_ada, b_ada, pre_g, post_g, w_in_even, w_out_even, na_rpb,
           w_in_odd, w_out_odd, sgu_w, sgu_b, sgu_g, s5_lam_re, s5_lam_im, s5_log_step,
           s5_b_re, s5_b_im, s5_c_re, s5_c_im, s5_d, glu_w, glu_b):
    b, l, d = x.shape
    lc = ctx.shape[1]
    depth = w_ada.shape[0]
    rows = l // GRID_W
    kh = min(NA_ROWS, rows)

    n_rows = -(-(b + 1) // 8) * 8
    cond = jnp.zeros((n_rows, d), F32).at[:b].set(c).at[b].set(c_ctx)
    mod = _ada_mod(cond, w_ada, b_ada)

    bd = _group_dft_mat()
    cm_l, sm_l = _dft_mats(l)
    cm_c, sm_c = _dft_mats(lc)

    xl, xc = x, ctx
    for i in range(depth):
        last = i == depth - 1
        j = i // 2
        shift, scale, gate = (mod[i, :b, k * d:(k + 1) * d][:, None, :] for k in range(3))
        ctx_mod = jnp.broadcast_to(mod[i, b][None, None, :], (b, 1, 3 * d))
        shift_c, scale_c, gate_c = (ctx_mod[..., k * d:(k + 1) * d] for k in range(3))
        if i % 2 == 0:
            w_in = w_in_even[j].astype(BF16)
            w_out = w_out_even[j].astype(BF16)
            zc, zs, ga, q, k_, v, gb = _even_in(xl, pre_g[i], scale, shift, w_in, bd)
            zc_c, zs_c, ga_c, q_c, k_c, v_c, gb_c = _even_in(xc, pre_g[i], scale_c, shift_c, w_in, bd)
            a_l = _dft_mix(cm_l, sm_l, zc, zs, ga)
            n_l = _na_attention(q, k_, v, k_c, v_c, gb, _na_bias(na_rpb[j], kh))
            xl = _even_out(a_l, n_l, w_out, post_g[i], gate, xl)
            if not last:
                a_c = _dft_mix(cm_c, sm_c, zc_c, zs_c, ga_c)
                n_c = _ctx_attention(q_c, k_c, v_c, gb_c)
                xc = _even_out(a_c, n_c, w_out, post_g[i], gate_c, xc)
        else:
            w_in = w_in_odd[j].astype(BF16)
            w_out = w_out_odd[j].astype(BF16)
            ws = sgu_w[j].astype(BF16)
            gw = SGU_WIDTH // SGU_GROUPS
            bs_full = jnp.repeat(sgu_b[j].astype(F32).T, gw, axis=1)
            mats = _s5_matrices(s5_lam_re[j], s5_lam_im[j], s5_log_step[j], s5_b_re[j], s5_b_im[j],
                                s5_c_re[j], s5_c_im[j])
            sg_l, s_l, gd_l = _odd_in(xl, pre_g[i], scale, shift, w_in, ws, bs_full, sgu_g[j])
            sg_c, s_c, gd_c = _odd_in(xc, pre_g[i], scale_c, shift_c, w_in, ws, bs_full, sgu_g[j])
            y_l, y_c = _s5_scan(s_l, s_c, mats)
            wg = glu_w[j].astype(BF16)
            xl = _odd_out(sg_l, y_l, s_l, gd_l, s5_d[j], wg, glu_b[j], w_out, post_g[i], gate, xl)
            if not last:
                xc = _odd_out(sg_c, y_c, s_c, gd_c, s5_d[j], wg, glu_b[j], w_out, post_g[i], gate_c, xc)
    return xl
```

```python
import functools

import numpy as np
import jax
import jax.numpy as jnp
from jax import lax
from jax.experimental import pallas as pl
from jax.experimental.pallas import tpu as pltpu

F32 = jnp.float32
BF16 = jnp.bfloat16

EPS = 1e-6
GRID_W = 64
FNET_WIDTH = 256
FNET_GROUP_DIM = 64
NA_WIDTH = 768
NA_HEAD_DIM = 64
NA_HEADS = 12
NA_ROWS = 8
NA_COLS = 16
SGU_CHUNK = 128
SGU_WIDTH = 512
SGU_GROUPS = 4
S5_WIDTH = 512
S5_GROUP_DIM = 16
S5_GROUPS = 32
S5_STATE = 64
S5_CHUNK = 16
S5_TILE = S5_CHUNK * S5_GROUP_DIM

V7X_VMEM_BYTES = 64 * 1024 * 1024
VMEM_LIMIT = 48 * 1024 * 1024
ROW_TILE = 512


def _cparams(sem):
    return pltpu.CompilerParams(dimension_semantics=sem, vmem_limit_bytes=VMEM_LIMIT)


def _silu(x):
    return x * jax.nn.sigmoid(x)


def _dot(a, b):
    return jnp.dot(a, b, preferred_element_type=F32)


def _dot_nt(a, b):
    return lax.dot_general(a, b, (((1,), (1,)), ((), ())), preferred_element_type=F32)


def _ada_kernel(c_ref, w_ref, b_ref, o_ref):
    c = c_ref[...]
    o_ref[0] = jnp.dot(_silu(c), w_ref[0], preferred_element_type=F32,
                       precision=lax.Precision.HIGHEST) + b_ref[0]


def _ada_mod(cond, w_ada, b_ada):
    depth, d, n = w_ada.shape
    rows = cond.shape[0]
    tn = 1024
    return pl.pallas_call(
        _ada_kernel,
        grid=(depth, n // tn),
        in_specs=[pl.BlockSpec((rows, d), lambda i, j: (0, 0)),
                  pl.BlockSpec((1, d, tn), lambda i, j: (i, 0, j)),
                  pl.BlockSpec((1, 1, tn), lambda i, j: (i, 0, j))],
        out_specs=pl.BlockSpec((1, rows, tn), lambda i, j: (i, 0, j)),
        out_shape=jax.ShapeDtypeStruct((depth, rows, n), F32),
        compiler_params=_cparams(("parallel", "parallel")),
        name="ada_mod",
    )(cond, w_ada, b_ada.reshape(depth, 1, n))


def _norm_mod(x, g, scale, shift):
    y = x * lax.rsqrt(jnp.mean(x * x, axis=-1, keepdims=True) + EPS)
    return (y * g) * (1.0 + scale) + shift


def _post(y, post_g, gate, x):
    yn = y * lax.rsqrt(jnp.mean(y * y, axis=-1, keepdims=True) + EPS)
    return x + gate * (yn * post_g)


def _even_in_kernel(x_ref, g_ref, sc_ref, sh_ref, w_ref, bd_ref,
                    zc_ref, zs_ref, ga_ref, q_ref, k_ref, v_ref, gb_ref):
    h = _norm_mod(x_ref[0], g_ref[...], sc_ref[0], sh_ref[0]).astype(BF16)
    f0, f1 = 0, FNET_WIDTH
    fa = _dot(h, w_ref[:, f0:f1]).astype(BF16)
    z = _dot(fa, bd_ref[...])
    zc_ref[...] = z[:, :FNET_WIDTH].astype(BF16)
    zs_ref[...] = z[:, FNET_WIDTH:].astype(BF16)
    o = f1
    ga_ref[0] = _dot(h, w_ref[:, o:o + FNET_WIDTH]).astype(BF16)
    o += FNET_WIDTH
    q_ref[0] = (_dot(h, w_ref[:, o:o + NA_WIDTH]) * (NA_HEAD_DIM ** -0.5)).astype(BF16)
    o += NA_WIDTH
    k_ref[0] = _dot(h, w_ref[:, o:o + NA_WIDTH]).astype(BF16)
    o += NA_WIDTH
    v_ref[0] = _dot(h, w_ref[:, o:o + NA_WIDTH]).astype(BF16)
    o += NA_WIDTH
    gb_ref[0] = _dot(h, w_ref[:, o:o + NA_WIDTH]).astype(BF16)


def _even_in(x, pre_g, scale, shift, w, bd):
    b, l, d = x.shape
    n = w.shape[1]
    tm = min(ROW_TILE, l)
    tok = lambda width: pl.BlockSpec((1, tm, width), lambda bi, i: (bi, i, 0))
    zspec = pl.BlockSpec((tm, FNET_WIDTH), lambda bi, i: (i, bi))
    vec = pl.BlockSpec((1, 1, d), lambda bi, i: (bi, 0, 0))
    sd = lambda width: jax.ShapeDtypeStruct((b, l, width), BF16)
    zsd = jax.ShapeDtypeStruct((l, b * FNET_WIDTH), BF16)
    return pl.pallas_call(
        _even_in_kernel,
        grid=(b, l // tm),
        in_specs=[tok(d), pl.BlockSpec((1, d), lambda bi, i: (0, 0)), vec, vec,
                  pl.BlockSpec((d, n), lambda bi, i: (0, 0)),
                  pl.BlockSpec(bd.shape, lambda bi, i: (0, 0))],
        out_specs=[zspec, zspec, tok(FNET_WIDTH), tok(NA_WIDTH), tok(NA_WIDTH), tok(NA_WIDTH), tok(NA_WIDTH)],
        out_shape=[zsd, zsd, sd(FNET_WIDTH), sd(NA_WIDTH), sd(NA_WIDTH), sd(NA_WIDTH), sd(NA_WIDTH)],
        compiler_params=_cparams(("parallel", "parallel")),
        name="even_in",
    )(x, pre_g.reshape(1, d), scale, shift, w, bd)


def _dft_kernel(c_ref, s_ref, zc_ref, zs_ref, ga_ref, o_ref, acc_ref, *, nb, norm):
    kk = pl.program_id(1)

    @pl.when(kk == 0)
    def _():
        acc_ref[...] = jnp.zeros_like(acc_ref)

    acc_ref[...] += _dot(c_ref[...], zc_ref[...]) + _dot(s_ref[...], zs_ref[...])

    @pl.when(kk == pl.num_programs(1) - 1)
    def _():
        for bi in range(nb):
            g = ga_ref[bi].astype(F32)
            y = acc_ref[:, bi * FNET_WIDTH:(bi + 1) * FNET_WIDTH] * norm
            o_ref[bi] = (y * _silu(g)).astype(BF16)


def _dft_mix(cm, sm, zc, zs, ga):
    b, l, _ = ga.shape
    tm = min(1024, l)
    tk = min(512, l)
    nc = b * FNET_WIDTH
    kern = functools.partial(_dft_kernel, nb=b, norm=float((l * FNET_GROUP_DIM) ** -0.5))
    return pl.pallas_call(
        kern,
        grid=(l // tm, l // tk),
        in_specs=[pl.BlockSpec((tm, tk), lambda i, k: (i, k)),
                  pl.BlockSpec((tm, tk), lambda i, k: (i, k)),
                  pl.BlockSpec((tk, nc), lambda i, k: (k, 0)),
                  pl.BlockSpec((tk, nc), lambda i, k: (k, 0)),
                  pl.BlockSpec((b, tm, FNET_WIDTH), lambda i, k: (0, i, 0))],
        out_specs=pl.BlockSpec((b, tm, FNET_WIDTH), lambda i, k: (0, i, 0)),
        out_shape=jax.ShapeDtypeStruct((b, l, FNET_WIDTH), BF16),
        scratch_shapes=[pltpu.VMEM((tm, nc), F32)],
        compiler_params=_cparams(("parallel", "arbitrary")),
        name="dft_mix",
    )(cm, sm, zc, zs, ga)


def _dft_gen_kernel(cb_ref, sb_ref, ca_ref, sa_ref, c_ref, s_ref):
    ca, sa = ca_ref[0], sa_ref[0]
    cb, sb = cb_ref[...], sb_ref[...]
    c_ref[...] = (ca * cb - sa * sb).astype(BF16)
    s_ref[...] = (-(sa * cb + ca * sb)).astype(BF16)


def _dft_mats(l):
    tr = min(256, l)
    k = jnp.arange(l, dtype=jnp.int32)
    w = 2.0 * np.pi / l
    ang_b = ((jnp.arange(tr, dtype=jnp.int32)[:, None] * k[None, :]) % l).astype(F32) * w
    ang_a = ((jnp.arange(l // tr, dtype=jnp.int32)[:, None] * tr * k[None, :]) % l).astype(F32) * w
    ca, sa = jnp.cos(ang_a)[:, None, :], jnp.sin(ang_a)[:, None, :]
    base = pl.BlockSpec((tr, l), lambda i: (0, 0))
    rowv = pl.BlockSpec((1, 1, l), lambda i: (i, 0, 0))
    out = pl.BlockSpec((tr, l), lambda i: (i, 0))
    sd = jax.ShapeDtypeStruct((l, l), BF16)
    return pl.pallas_call(
        _dft_gen_kernel,
        grid=(l // tr,),
        in_specs=[base, base, rowv, rowv],
        out_specs=[out, out],
        out_shape=[sd, sd],
        compiler_params=_cparams(("parallel",)),
        name="dft_gen",
    )(jnp.cos(ang_b), jnp.sin(ang_b), ca, sa)


def _group_dft_mat():
    n = FNET_GROUP_DIM
    j = np.arange(n)
    ang = 2.0 * np.pi * ((j[:, None] * j[None, :]) % n) / n
    eye = np.eye(FNET_WIDTH // n)
    mat = np.concatenate([np.kron(eye, np.cos(ang)), np.kron(eye, np.sin(ang))], axis=1)
    return jnp.asarray(mat, F32).astype(BF16)


HEADS_PER_TILE = 4
HEAD_TILE = HEADS_PER_TILE * NA_HEAD_DIM
HEAD_TILES = NA_WIDTH // HEAD_TILE


def _na_kernel(q_ref, k_ref, v_ref, kc_ref, vc_ref, gb_ref, bias_ref, o_ref, *, rows, kh):
    r = pl.program_id(1)
    r0 = jnp.clip(r - kh // 2, 0, rows - kh)
    start = pl.multiple_of(r0 * GRID_W, GRID_W)
    nloc = kh * GRID_W
    row_head = lax.broadcasted_iota(jnp.int32, (HEAD_TILE, HEAD_TILE), 0) // GRID_W
    lane_head = lax.broadcasted_iota(jnp.int32, (HEAD_TILE, HEAD_TILE), 1) // NA_HEAD_DIM
    own = row_head == lane_head
    out_head = lax.broadcasted_iota(jnp.int32, (GRID_W, HEAD_TILE), 1) // NA_HEAD_DIM
    for t in range(HEAD_TILES):
        cs = slice(t * HEAD_TILE, (t + 1) * HEAD_TILE)
        q4 = q_ref[0, :, cs]
        qm = jnp.where(own, jnp.concatenate([q4] * HEADS_PER_TILE, axis=0), 0.0).astype(BF16)
        kl = k_ref[0, pl.ds(start, nloc), cs]
        vl = v_ref[0, pl.ds(start, nloc), cs]
        s_loc = _dot_nt(qm, kl) + bias_ref[0, t]
        s_ctx = _dot_nt(qm, kc_ref[0, :, cs])
        m = jnp.maximum(jnp.max(s_loc, axis=-1, keepdims=True), jnp.max(s_ctx, axis=-1, keepdims=True))
        p_loc = jnp.exp(s_loc - m)
        p_ctx = jnp.exp(s_ctx - m)
        den = jnp.sum(p_loc, axis=-1, keepdims=True) + jnp.sum(p_ctx, axis=-1, keepdims=True)
        o = (_dot(p_loc.astype(BF16), vl) + _dot(p_ctx.astype(BF16), vc_ref[0, :, cs])) / den
        o4 = o[0:GRID_W]
        for h in range(1, HEADS_PER_TILE):
            o4 = jnp.where(out_head == h, o[h * GRID_W:(h + 1) * GRID_W], o4)
        g = gb_ref[0, :, cs].astype(F32)
        o_ref[0, :, cs] = (o4 * _silu(g)).astype(BF16)


def _na_attention(q, k, v, kc, vc, gb, bias):
    b, l, w = q.shape
    lc = kc.shape[1]
    rows = l // GRID_W
    kh = min(NA_ROWS, rows)

    def bias_idx(bi, r):
        r0 = jnp.clip(r - kh // 2, 0, rows - kh)
        return (r0 - r + (NA_ROWS - 1), 0, 0, 0)

    row = pl.BlockSpec((1, GRID_W, w), lambda bi, r: (bi, r, 0))
    full = lambda n: pl.BlockSpec((1, n, w), lambda bi, r: (bi, 0, 0))
    kern = functools.partial(_na_kernel, rows=rows, kh=kh)
    return pl.pallas_call(
        kern,
        grid=(b, rows),
        in_specs=[row, full(l), full(l), full(lc), full(lc), row,
                  pl.BlockSpec((1, HEAD_TILES, HEAD_TILE, kh * GRID_W), bias_idx)],
        out_specs=row,
        out_shape=jax.ShapeDtypeStruct((b, l, w), BF16),
        compiler_params=_cparams(("parallel", "arbitrary")),
        name="na_attention",
    )(q, k, v, kc, vc, gb, bias)


def _na_bias(rpb, kh):
    cq = np.arange(GRID_W)
    ck = np.arange(GRID_W)
    c0 = np.clip(cq - NA_COLS // 2, 0, GRID_W - NA_COLS)
    col_in = (ck[None, :] >= c0[:, None]) & (ck[None, :] < c0[:, None] + NA_COLS)
    dc = np.clip(ck[None, :] - cq[:, None] + (NA_COLS - 1), 0, 2 * NA_COLS - 2)
    onehot = (dc.reshape(1, -1) == np.arange(2 * NA_COLS - 1)[:, None]).astype(np.float32)
    t = jnp.einsum('hrc,cx->hrx', rpb.astype(F32), jnp.asarray(onehot), precision=lax.Precision.HIGHEST)
    t = t.reshape(NA_HEADS, 2 * NA_ROWS - 1, GRID_W, GRID_W)
    t = jnp.where(col_in[None, None], t, -1e30)
    var = jnp.stack([t[:, s:s + kh] for s in range(NA_ROWS)])
    var = var.transpose(0, 1, 3, 2, 4)
    return var.reshape(NA_ROWS, HEAD_TILES, HEAD_TILE, kh * GRID_W)


def _ctx_attn_kernel(q_ref, k_ref, v_ref, gb_ref, o_ref, acc_ref):
    for h in range(NA_HEADS):
        lo, hi = h * NA_HEAD_DIM, (h + 1) * NA_HEAD_DIM
        s = _dot_nt(q_ref[0, :, lo:hi], k_ref[0, :, lo:hi])
        m = jnp.max(s, axis=-1, keepdims=True)
        p = jnp.exp(s - m)
        den = jnp.sum(p, axis=-1, keepdims=True)
        acc_ref[:, lo:hi] = _dot(p.astype(BF16), v_ref[0, :, lo:hi]) / den
    g = gb_ref[0].astype(F32)
    o_ref[0] = (acc_ref[...] * _silu(g)).astype(BF16)


def _ctx_attention(q, k, v, gb):
    b, lc, w = q.shape
    spec = pl.BlockSpec((1, lc, w), lambda bi: (bi, 0, 0))
    return pl.pallas_call(
        _ctx_attn_kernel,
        grid=(b,),
        in_specs=[spec, spec, spec, spec],
        out_specs=spec,
        out_shape=jax.ShapeDtypeStruct((b, lc, w), BF16),
        scratch_shapes=[pltpu.VMEM((lc, w), F32)],
        compiler_params=_cparams(("parallel",)),
        name="ctx_attention",
    )(q, k, v, gb)


def _even_out_kernel(a_ref, n_ref, w_ref, pg_ref, gate_ref, x_ref, o_ref):
    y = _dot(a_ref[0], w_ref[:FNET_WIDTH, :]) + _dot(n_ref[0], w_ref[FNET_WIDTH:, :])
    o_ref[0] = _post(y, pg_ref[...], gate_ref[0], x_ref[0])


def _even_out(a, n, w, post_g, gate, x):
    b, l, d = x.shape
    tm = min(ROW_TILE, l)
    tok = lambda width: pl.BlockSpec((1, tm, width), lambda bi, i: (bi, i, 0))
    return pl.pallas_call(
        _even_out_kernel,
        grid=(b, l // tm),
        in_specs=[tok(FNET_WIDTH), tok(NA_WIDTH),
                  pl.BlockSpec(w.shape, lambda bi, i: (0, 0)),
                  pl.BlockSpec((1, d), lambda bi, i: (0, 0)),
                  pl.BlockSpec((1, 1, d), lambda bi, i: (bi, 0, 0)),
                  tok(d)],
        out_specs=tok(d),
        out_shape=jax.ShapeDtypeStruct((b, l, d), F32),
        compiler_params=_cparams(("parallel", "parallel")),
        name="even_out",
    )(a, n, w, post_g.reshape(1, d), gate, x)


S5_SUPER = 8
N_SUPER = S5_GROUPS // S5_SUPER
SUPER_LANES = S5_SUPER * S5_GROUP_DIM


def _super_spec(tm):
    return pl.BlockSpec((N_SUPER, tm // S5_CHUNK, 1, S5_CHUNK, SUPER_LANES), lambda bi, i: (0, i, bi, 0, 0))


def _super_shape(b, l):
    return jax.ShapeDtypeStruct((N_SUPER, l // S5_CHUNK, b, S5_CHUNK, SUPER_LANES), BF16)


def _odd_in_kernel(x_ref, g_ref, sc_ref, sh_ref, w_ref, ws_ref, bs_ref, sg_g_ref,
                   sg_ref, s_ref, gd_ref, *, tm):
    h = _norm_mod(x_ref[0], g_ref[...], sc_ref[0], sh_ref[0]).astype(BF16)
    wd = SGU_WIDTH
    u = _dot(h, w_ref[:, 0:wd])
    v = _dot(h, w_ref[:, wd:2 * wd])
    gc = _dot(h, w_ref[:, 2 * wd:3 * wd])
    s = _dot(h, w_ref[:, 3 * wd:3 * wd + S5_WIDTH]).astype(BF16)
    for j in range(N_SUPER):
        sj = s[:, j * SUPER_LANES:(j + 1) * SUPER_LANES]
        s_ref[j, :, 0] = sj.reshape(tm // S5_CHUNK, S5_CHUNK, SUPER_LANES)
    gd_ref[0] = _dot(h, w_ref[:, 3 * wd + S5_WIDTH:]).astype(BF16)
    vc = v - jnp.mean(v, axis=-1, keepdims=True)
    vn = (vc * lax.rsqrt(jnp.mean(vc * vc, axis=-1, keepdims=True) + EPS) * sg_g_ref[...]).astype(BF16)
    gate = u * _silu(gc)
    gw = SGU_WIDTH // SGU_GROUPS
    for j in range(tm // SGU_CHUNK):
        rs = slice(j * SGU_CHUNK, (j + 1) * SGU_CHUNK)
        for g in range(SGU_GROUPS):
            cs = slice(g * gw, (g + 1) * gw)
            mixed = _dot(ws_ref[g], vn[rs, cs]) + bs_ref[:, cs]
            sg_ref[0, rs, cs] = (gate[rs, cs] * mixed).astype(BF16)


def _odd_in(x, pre_g, scale, shift, w, ws, bs_full, sgu_g):
    b, l, d = x.shape
    n = w.shape[1]
    tm = min(ROW_TILE, l)
    tok = lambda width: pl.BlockSpec((1, tm, width), lambda bi, i: (bi, i, 0))
    vec = pl.BlockSpec((1, 1, d), lambda bi, i: (bi, 0, 0))
    const2 = lambda a: pl.BlockSpec(a.shape, lambda bi, i: (0,) * a.ndim)
    sd = jax.ShapeDtypeStruct((b, l, SGU_WIDTH), BF16)
    g2 = sgu_g.reshape(1, SGU_WIDTH)
    return pl.pallas_call(
        functools.partial(_odd_in_kernel, tm=tm),
        grid=(b, l // tm),
        in_specs=[tok(d), pl.BlockSpec((1, d), lambda bi, i: (0, 0)), vec, vec,
                  pl.BlockSpec((d, n), lambda bi, i: (0, 0)), const2(ws), const2(bs_full), const2(g2)],
        out_specs=[tok(SGU_WIDTH), _super_spec(tm), tok(S5_WIDTH)],
        out_shape=[sd, _super_shape(b, l), sd],
        compiler_params=_cparams(("parallel", "parallel")),
        name="odd_in",
    )(x, pre_g.reshape(1, d), scale, shift, w, ws, bs_full, g2)


SUPER_K = S5_CHUNK * SUPER_LANES
SG_ROW_TILE = 512
SG_COL_TILE = 1024
REC_LANES = 1024


def _sg_kernel(*refs, n_in):
    xs, ws, o_ref = refs[:n_in], refs[n_in:2 * n_in], refs[2 * n_in]
    acc = _dot(xs[0][0].astype(BF16), ws[0][0])
    for x_ref, w_ref in zip(xs[1:], ws[1:]):
        acc += _dot(x_ref[0].astype(BF16), w_ref[0])
    o_ref[0] = acc.astype(o_ref.dtype)


def _sg_matmul(xs, ws, out_dtype, name):
    _, r, k = xs[0].shape
    n = ws[0].shape[2]
    tr, tn = min(SG_ROW_TILE, r), min(SG_COL_TILE, n)
    n_in = len(xs)
    in_specs = [pl.BlockSpec((1, tr, k), lambda j, ni, ri: (j, ri, 0)) for _ in xs]
    in_specs += [pl.BlockSpec((1, k, tn), lambda j, ni, ri: (j, 0, ni)) for _ in ws]
    return pl.pallas_call(
        functools.partial(_sg_kernel, n_in=n_in),
        grid=(N_SUPER, n // tn, r // tr),
        in_specs=in_specs,
        out_specs=pl.BlockSpec((1, tr, tn), lambda j, ni, ri: (j, ri, ni)),
        out_shape=jax.ShapeDtypeStruct((N_SUPER, r, n), out_dtype),
        compiler_params=_cparams(("parallel", "parallel", "parallel")),
        name=name,
    )(*xs, *ws)


def _rec_kernel(sc_ref, sl_ref, a_ref, hc_ref, hl_ref, *, nb, n_ctx, n_lat):
    half = S5_TILE // 2
    npieces = REC_LANES // S5_TILE
    fwd_lane = lax.broadcasted_iota(jnp.int32, (1, half), 1) < S5_STATE
    coef = [(a_ref[0, :, p * S5_TILE:p * S5_TILE + half], a_ref[0, :, p * S5_TILE + half:(p + 1) * S5_TILE])
            for p in range(npieces)]

    def step(s_ref, h_ref, c, carry, forward):
        rows = pl.ds(pl.multiple_of(c * nb, nb), nb)
        out = []
        for p in range(npieces):
            hre, him = carry[2 * p], carry[2 * p + 1]
            are, aim = coef[p]
            re_sl = slice(p * S5_TILE, p * S5_TILE + half)
            im_sl = slice(p * S5_TILE + half, (p + 1) * S5_TILE)
            if forward:
                h_ref[0, rows, re_sl] = jnp.where(fwd_lane, hre, 0.0)
                h_ref[0, rows, im_sl] = jnp.where(fwd_lane, him, 0.0)
            else:
                h_ref[0, rows, re_sl] = jnp.where(fwd_lane, h_ref[0, rows, re_sl], hre)
                h_ref[0, rows, im_sl] = jnp.where(fwd_lane, h_ref[0, rows, im_sl], him)
            sre, sim = s_ref[0, rows, re_sl], s_ref[0, rows, im_sl]
            out.append(are * hre - aim * him + sre)
            out.append(are * him + aim * hre + sim)
        return tuple(out)

    zero = tuple(jnp.zeros((nb, half), F32) for _ in range(2 * npieces))
    carry = lax.fori_loop(0, n_ctx, lambda c, cr: step(sc_ref, hc_ref, c, cr, True), zero)
    lax.fori_loop(0, n_lat, lambda c, cr: step(sl_ref, hl_ref, c, cr, True), carry)
    carry = lax.fori_loop(0, n_ctx, lambda i, cr: step(sc_ref, hc_ref, n_ctx - 1 - i, cr, False), zero)
    lax.fori_loop(0, n_lat, lambda i, cr: step(sl_ref, hl_ref, n_lat - 1 - i, cr, False), carry)


def _chunk_recurrence(s_c, s_l, a16, nb):
    _, rc, lanes = s_c.shape
    rl = s_l.shape[1]
    spec = lambda r: pl.BlockSpec((1, r, REC_LANES), lambda j, i: (j, 0, i))
    return pl.pallas_call(
        functools.partial(_rec_kernel, nb=nb, n_ctx=rc // nb, n_lat=rl // nb),
        grid=(N_SUPER, lanes // REC_LANES),
        in_specs=[spec(rc), spec(rl), spec(1)],
        out_specs=[spec(rc), spec(rl)],
        out_shape=[jax.ShapeDtypeStruct(s_c.shape, F32), jax.ShapeDtypeStruct(s_l.shape, F32)],
        compiler_params=_cparams(("parallel", "parallel")),
        name="s5_recurrence",
    )(s_c, s_l, a16)


def _cmul(ar, ai, br, bi):
    return ar * br - ai * bi, ar * bi + ai * br


def _toeplitz(k, lower):
    n = k.shape[0]
    pad = lambda a, lo, hi: jnp.pad(a, ((lo, hi),) + ((0, 0),) * (a.ndim - 1))
    if lower:
        return jnp.stack([pad(k[:n - s], s, 0) for s in range(n)])
    return jnp.stack([pad(k[:s + 1][::-1], 0, n - 1 - s) for s in range(n)])


def _s5_matrices(lam_re, lam_im, log_step, b_re, b_im, c_re, c_im):
    t = S5_CHUNK
    hp = lax.Precision.HIGHEST
    taus = jnp.arange(t + 1, dtype=F32)[:, None, None]
    er, ei, wr, wi, a_re, a_im, ks = [], [], [], [], [], [], []
    for d in range(2):
        lr = jnp.minimum(lam_re[d].astype(F32), -1e-4)
        li = lam_im[d].astype(F32)
        dt = jnp.exp(log_step[d].astype(F32))[:, None]
        mag = jnp.exp(lr * dt * taus)
        pr, pi = mag * jnp.cos(li * dt * taus), mag * jnp.sin(li * dt * taus)
        den = lr * lr + li * li
        qr = ((pr[1] - 1.0) * lr + pi[1] * li) / den
        qi = (pi[1] * lr - (pr[1] - 1.0) * li) / den
        bbr, bbi = _cmul(qr[..., None], qi[..., None], b_re[d].astype(F32), b_im[d].astype(F32))
        e_r, e_i = _cmul(pr[..., None], pi[..., None], bbr[None], bbi[None])
        cr, ci = c_re[d].astype(F32), c_im[d].astype(F32)
        w_r, w_i = _cmul(cr[None], ci[None], pr[:, :, None, :], pi[:, :, None, :])
        k = (jnp.einsum('gmp,tgpn->tgmn', cr, e_r[:t], precision=hp)
             - jnp.einsum('gmp,tgpn->tgmn', ci, e_i[:t], precision=hp))
        er.append(e_r); ei.append(e_i); wr.append(w_r); wi.append(w_i); ks.append(k)
        a_re.append(pr[t]); a_im.append(pi[t])
    m = _toeplitz(ks[0], True) + _toeplitz(ks[1], False)
    m = m.transpose(2, 0, 4, 1, 3)
    rows = lambda e: e.transpose(1, 0, 3, 2)
    p_mat = jnp.concatenate([rows(er[0][:t][::-1]), rows(er[1][:t]),
                             rows(ei[0][:t][::-1]), rows(ei[1][:t])], axis=-1)
    cols = lambda w: w.transpose(1, 3, 0, 2)
    q_mat = jnp.concatenate([cols(wr[0][1:]), cols(wr[1][1:][::-1]),
                             -cols(wi[0][1:]), -cols(wi[1][1:][::-1])], axis=1)
    a16 = jnp.concatenate([a_re[0], a_re[1], a_im[0], a_im[1]], axis=-1)

    eye = jnp.eye(S5_SUPER, dtype=BF16)

    def expand(a, axes_in, axes_out):
        nd_in, nd_out = len(axes_in) - 1, len(axes_out) - 1
        a = a.astype(BF16).reshape((N_SUPER, S5_SUPER) + a.shape[1:])
        a = a[..., None] * eye.reshape((1, S5_SUPER) + (1,) * (nd_in + nd_out) + (S5_SUPER,))
        perm_in = list(range(2, 2 + nd_in))
        perm_in.insert(axes_in.index('g'), 1)
        perm_out = list(range(2 + nd_in, 2 + nd_in + nd_out))
        perm_out.insert(axes_out.index('g'), 2 + nd_in + nd_out)
        return a.transpose([0] + perm_in + perm_out).reshape(N_SUPER, SUPER_K, SUPER_K)

    m4 = expand(m, ('s', 'g', 'n'), ('t', 'g', 'm'))
    p4 = expand(p_mat, ('s', 'g', 'n'), ('g', 'c'))
    q4 = expand(q_mat, ('g', 'r'), ('t', 'g', 'm'))
    return m4, p4, q4, a16.reshape(N_SUPER, 1, SUPER_K)


def _s5_scan(s_l, s_c, mats):
    m4, p4, q4, a16 = mats
    b = s_l.shape[2]
    u_l = s_l.reshape(N_SUPER, -1, SUPER_K)
    u_c = s_c.reshape(N_SUPER, -1, SUPER_K)
    sum_l = _sg_matmul([u_l], [p4], F32, "s5_summary")
    sum_c = _sg_matmul([u_c], [p4], F32, "s5_summary_ctx")
    h_c, h_l = _chunk_recurrence(sum_c, sum_l, a16, b)
    y_l = _sg_matmul([u_l, h_l], [m4, q4], BF16, "s5_apply")
    y_c = _sg_matmul([u_c, h_c], [m4, q4], BF16, "s5_apply_ctx")
    return y_l.reshape(s_l.shape), y_c.reshape(s_c.shape)


def _gelu_tanh(x):
    return 0.5 * x * (1.0 + jnp.tanh(float(np.sqrt(2.0 / np.pi)) * (x + 0.044715 * (x * x * x))))


def _odd_out_kernel(sg_ref, y_ref, s_ref, gd_ref, dsk_ref, wg_ref, bg_ref, w_ref, pg_ref, gate_ref, x_ref, o_ref):
    tm = x_ref.shape[1]
    tokens = lambda ref: jnp.concatenate([ref[j, :, 0].reshape(tm, SUPER_LANES) for j in range(N_SUPER)], axis=-1)
    ys = tokens(y_ref).astype(F32)
    s = tokens(s_ref).astype(F32)
    z = _gelu_tanh(ys + dsk_ref[...] * s)
    zz = _dot(z.astype(BF16), wg_ref[...]) + bg_ref[...]
    gd = gd_ref[0].astype(F32)
    ss = zz[:, :S5_WIDTH] * jax.nn.sigmoid(zz[:, S5_WIDTH:]) * _silu(gd)
    y = _dot(sg_ref[0], w_ref[:SGU_WIDTH, :]) + _dot(ss.astype(BF16), w_ref[SGU_WIDTH:, :])
    o_ref[0] = _post(y, pg_ref[...], gate_ref[0], x_ref[0])


def _odd_out(sg, y, s, gd, d_skip, wg, bg, w, post_g, gate, x):
    b, l, d = x.shape
    tm = min(ROW_TILE, l)
    tok = lambda width: pl.BlockSpec((1, tm, width), lambda bi, i: (bi, i, 0))
    const2 = lambda a: pl.BlockSpec(a.shape, lambda bi, i: (0,) * a.ndim)
    dsk = d_skip.reshape(1, S5_WIDTH).astype(F32)
    bg2 = bg.reshape(1, 2 * S5_WIDTH).astype(F32)
    pg2 = post_g.reshape(1, d)
    return pl.pallas_call(
        _odd_out_kernel,
        grid=(b, l // tm),
        in_specs=[tok(SGU_WIDTH), _super_spec(tm), _super_spec(tm), tok(S5_WIDTH),
                  const2(dsk), const2(wg), const2(bg2), const2(w), const2(pg2),
                  pl.BlockSpec((1, 1, d), lambda bi, i: (bi, 0, 0)), tok(d)],
        out_specs=tok(d),
        out_shape=jax.ShapeDtypeStruct((b, l, d), F32),
        compiler_params=_cparams(("parallel", "parallel")),
        name="odd_out",
    )(sg, y, s, gd, dsk, wg, bg2, w, pg2, gate, x)


def kernel(x, c, ctx, c_ctx, w_ada, b_ada, pre_g, post_g, w_in_even, w_out_even, na_rpb,
           w_in_odd, w_out_odd, sgu_w, sgu_b, sgu_g, s5_lam_re, s5_lam_im, s5_log_step,
           s5_b_re, s5_b_im, s5_c_re, s5_c_im, s5_d, glu_w, glu_b):
    b, l, d = x.shape
    lc = ctx.shape[1]
    depth = w_ada.shape[0]
    rows = l // GRID_W
    kh = min(NA_ROWS, rows)

    n_rows = -(-(b + 1) // 8) * 8
    cond = jnp.zeros((n_rows, d), F32).at[:b].set(c).at[b].set(c_ctx)
    mod = _ada_mod(cond, w_ada, b_ada)

    bd = _group_dft_mat()
    cm_l, sm_l = _dft_mats(l)
    cm_c, sm_c = _dft_mats(lc)

    xl, xc = x, ctx
    for i in range(depth):
        last = i == depth - 1
        j = i // 2
        shift, scale, gate = (mod[i, :b, k * d:(k + 1) * d][:, None, :] for k in range(3))
        ctx_mod = jnp.broadcast_to(mod[i, b][None, None, :], (b, 1, 3 * d))
        shift_c, scale_c, gate_c = (ctx_mod[..., k * d:(k + 1) * d] for k in range(3))
        if i % 2 == 0:
            w_in = w_in_even[j].astype(BF16)
            w_out = w_out_even[j].astype(BF16)
            zc, zs, ga, q, k_, v, gb = _even_in(xl, pre_g[i], scale, shift, w_in, bd)
            zc_c, zs_c, ga_c, q_c, k_c, v_c, gb_c = _even_in(xc, pre_g[i], scale_c, shift_c, w_in, bd)
            a_l = _dft_mix(cm_l, sm_l, zc, zs, ga)
            n_l = _na_attention(q, k_, v, k_c, v_c, gb, _na_bias(na_rpb[j], kh))
            xl = _even_out(a_l, n_l, w_out, post_g[i], gate, xl)
            if not last:
                a_c = _dft_mix(cm_c, sm_c, zc_c, zs_c, ga_c)
                n_c = _ctx_attention(q_c, k_c, v_c, gb_c)
                xc = _even_out(a_c, n_c, w_out, post_g[i], gate_c, xc)
        else:
            w_in = w_in_odd[j].astype(BF16)
            w_out = w_out_odd[j].astype(BF16)
            ws = sgu_w[j].astype(BF16)
            gw = SGU_WIDTH // SGU_GROUPS
            bs_full = jnp.repeat(sgu_b[j].astype(F32).T, gw, axis=1)
            mats = _s5_matrices(s5_lam_re[j], s5_lam_im[j], s5_log_step[j], s5_b_re[j], s5_b_im[j],
                                s5_c_re[j], s5_c_im[j])
            sg_l, s_l, gd_l = _odd_in(xl, pre_g[i], scale, shift, w_in, ws, bs_full, sgu_g[j])
            sg_c, s_c, gd_c = _odd_in(xc, pre_g[i], scale_c, shift_c, w_in, ws, bs_full, sgu_g[j])
            y_l, y_c = _s5_scan(s_l, s_c, mats)
            wg = glu_w[j].astype(BF16)
            xl = _odd_out(sg_l, y_l, s_l, gd_l, s5_d[j], wg, glu_b[j], w_out, post_g[i], gate, xl)
            if not last:
                xc = _odd_out(sg_c, y_c, s_c, gd_c, s5_d[j], wg, glu_b[j], w_out, post_g[i], gate_c, xc)
    return xl
```

```python
import functools

import numpy as np
import jax
import jax.numpy as jnp
from jax import lax
from jax.experimental import pallas as pl
from jax.experimental.pallas import tpu as pltpu

F32 = jnp.float32
BF16 = jnp.bfloat16

EPS = 1e-6
GRID_W = 64
FNET_WIDTH = 256
FNET_GROUP_DIM = 64
NA_WIDTH = 768
NA_HEAD_DIM = 64
NA_HEADS = 12
NA_ROWS = 8
NA_COLS = 16
SGU_CHUNK = 128
SGU_WIDTH = 512
SGU_GROUPS = 4
S5_WIDTH = 512
S5_GROUP_DIM = 16
S5_GROUPS = 32
S5_STATE = 64
S5_CHUNK = 16
S5_TILE = S5_CHUNK * S5_GROUP_DIM

V7X_VMEM_BYTES = 64 * 1024 * 1024
VMEM_LIMIT = 48 * 1024 * 1024
ROW_TILE = 512


def _cparams(sem):
    return pltpu.CompilerParams(dimension_semantics=sem, vmem_limit_bytes=VMEM_LIMIT)


def _silu(x):
    return x * jax.nn.sigmoid(x)


def _dot(a, b):
    return jnp.dot(a, b, preferred_element_type=F32)


def _dot_nt(a, b):
    return lax.dot_general(a, b, (((1,), (1,)), ((), ())), preferred_element_type=F32)


def _ada_kernel(c_ref, w_ref, b_ref, o_ref):
    c = c_ref[...]
    o_ref[0] = jnp.dot(_silu(c), w_ref[0], preferred_element_type=F32,
                       precision=lax.Precision.HIGHEST) + b_ref[0]


def _ada_mod(cond, w_ada, b_ada):
    depth, d, n = w_ada.shape
    rows = cond.shape[0]
    tn = 1024
    return pl.pallas_call(
        _ada_kernel,
        grid=(depth, n // tn),
        in_specs=[pl.BlockSpec((rows, d), lambda i, j: (0, 0)),
                  pl.BlockSpec((1, d, tn), lambda i, j: (i, 0, j)),
                  pl.BlockSpec((1, 1, tn), lambda i, j: (i, 0, j))],
        out_specs=pl.BlockSpec((1, rows, tn), lambda i, j: (i, 0, j)),
        out_shape=jax.ShapeDtypeStruct((depth, rows, n), F32),
        compiler_params=_cparams(("parallel", "parallel")),
        name="ada_mod",
    )(cond, w_ada, b_ada.reshape(depth, 1, n))


def _norm_mod(x, g, scale, shift):
    y = x * lax.rsqrt(jnp.mean(x * x, axis=-1, keepdims=True) + EPS)
    return (y * g) * (1.0 + scale) + shift


def _post(y, post_g, gate, x):
    yn = y * lax.rsqrt(jnp.mean(y * y, axis=-1, keepdims=True) + EPS)
    return x + gate * (yn * post_g)


def _even_in_kernel(x_ref, g_ref, sc_ref, sh_ref, w_ref, bd_ref,
                    zc_ref, zs_ref, ga_ref, q_ref, k_ref, v_ref, gb_ref):
    h = _norm_mod(x_ref[0], g_ref[...], sc_ref[0], sh_ref[0]).astype(BF16)
    f0, f1 = 0, FNET_WIDTH
    fa = _dot(h, w_ref[:, f0:f1]).astype(BF16)
    z = _dot(fa, bd_ref[...]).astype(BF16)
    half = z.shape[0] // 2
    r = lax.broadcasted_iota(jnp.int32, (half, 2 * half), 0)
    c = lax.broadcasted_iota(jnp.int32, (half, 2 * half), 1)
    for par in range(2):
        zp = _dot((c == 2 * r + par).astype(BF16), z).astype(BF16)
        zc_ref[par] = zp[:, :FNET_WIDTH]
        zs_ref[par] = zp[:, FNET_WIDTH:]
    o = f1
    ga_ref[0] = _dot(h, w_ref[:, o:o + FNET_WIDTH]).astype(BF16)
    o += FNET_WIDTH
    q_ref[0] = (_dot(h, w_ref[:, o:o + NA_WIDTH]) * (NA_HEAD_DIM ** -0.5)).astype(BF16)
    o += NA_WIDTH
    k_ref[0] = _dot(h, w_ref[:, o:o + NA_WIDTH]).astype(BF16)
    o += NA_WIDTH
    v_ref[0] = _dot(h, w_ref[:, o:o + NA_WIDTH]).astype(BF16)
    o += NA_WIDTH
    gb_ref[0] = _dot(h, w_ref[:, o:o + NA_WIDTH]).astype(BF16)


def _even_in(x, pre_g, scale, shift, w, bd):
    b, l, d = x.shape
    n = w.shape[1]
    tm = min(ROW_TILE, l)
    tok = lambda width: pl.BlockSpec((1, tm, width), lambda bi, i: (bi, i, 0))
    zspec = pl.BlockSpec((2, tm // 2, FNET_WIDTH), lambda bi, i: (0, i, bi))
    vec = pl.BlockSpec((1, 1, d), lambda bi, i: (bi, 0, 0))
    sd = lambda width: jax.ShapeDtypeStruct((b, l, width), BF16)
    zsd = jax.ShapeDtypeStruct((2, l // 2, b * FNET_WIDTH), BF16)
    return pl.pallas_call(
        _even_in_kernel,
        grid=(b, l // tm),
        in_specs=[tok(d), pl.BlockSpec((1, d), lambda bi, i: (0, 0)), vec, vec,
                  pl.BlockSpec((d, n), lambda bi, i: (0, 0)),
                  pl.BlockSpec(bd.shape, lambda bi, i: (0, 0))],
        out_specs=[zspec, zspec, tok(FNET_WIDTH), tok(NA_WIDTH), tok(NA_WIDTH), tok(NA_WIDTH), tok(NA_WIDTH)],
        out_shape=[zsd, zsd, sd(FNET_WIDTH), sd(NA_WIDTH), sd(NA_WIDTH), sd(NA_WIDTH), sd(NA_WIDTH)],
        compiler_params=_cparams(("parallel", "parallel")),
        name="even_in",
    )(x, pre_g.reshape(1, d), scale, shift, w, bd)


def _dft_kernel(ce_ref, se_ref, co_ref, so_ref, zc_ref, zs_ref, ga_ref, o_ref, acc_e, acc_o, *, nb, norm):
    kk = pl.program_id(1)

    @pl.when(kk == 0)
    def _():
        acc_e[...] = jnp.zeros_like(acc_e)
        acc_o[...] = jnp.zeros_like(acc_o)

    acc_e[...] += _dot(ce_ref[...], zc_ref[0]) + _dot(se_ref[...], zs_ref[0])
    acc_o[...] += _dot(co_ref[...], zc_ref[1]) + _dot(so_ref[...], zs_ref[1])

    @pl.when(kk == pl.num_programs(1) - 1)
    def _():
        for bi in range(nb):
            cs = slice(bi * FNET_WIDTH, (bi + 1) * FNET_WIDTH)
            e, o = acc_e[:, cs], acc_o[:, cs]
            for half, y in enumerate((e + o, e - o)):
                g = ga_ref[bi, half].astype(F32)
                o_ref[bi, half] = (y * norm * _silu(g)).astype(BF16)


def _dft_mix(cm, sm, zc, zs, ga):
    b, l, _ = ga.shape
    lh = l // 2
    tm = min(512, lh)
    tk = min(256, lh)
    nk = lh // tk
    nc = b * FNET_WIDTH
    kern = functools.partial(_dft_kernel, nb=b, norm=float((l * FNET_GROUP_DIM) ** -0.5))
    mat_e = pl.BlockSpec((tm, tk), lambda i, k: (i, k))
    mat_o = pl.BlockSpec((tm, tk), lambda i, k: (i, nk + k))
    zspec = pl.BlockSpec((2, tk, nc), lambda i, k: (0, k, 0))
    halves = pl.BlockSpec((b, 2, tm, FNET_WIDTH), lambda i, k: (0, 0, i, 0))
    out = pl.pallas_call(
        kern,
        grid=(lh // tm, nk),
        in_specs=[mat_e, mat_e, mat_o, mat_o, zspec, zspec, halves],
        out_specs=halves,
        out_shape=jax.ShapeDtypeStruct((b, 2, lh, FNET_WIDTH), BF16),
        scratch_shapes=[pltpu.VMEM((tm, nc), F32), pltpu.VMEM((tm, nc), F32)],
        compiler_params=_cparams(("parallel", "arbitrary")),
        name="dft_mix",
    )(cm, sm, cm, sm, zc, zs, ga.reshape(b, 2, lh, FNET_WIDTH))
    return out.reshape(b, l, FNET_WIDTH)


def _dft_gen_kernel(cb_ref, sb_ref, ca_ref, sa_ref, c_ref, s_ref):
    ca, sa = ca_ref[0], sa_ref[0]
    cb, sb = cb_ref[...], sb_ref[...]
    c_ref[...] = (ca * cb - sa * sb).astype(BF16)
    s_ref[...] = (-(sa * cb + ca * sb)).astype(BF16)


def _dft_mats(l):
    lh = l // 2
    tr = min(256, lh)
    kh = jnp.arange(lh, dtype=jnp.int32)
    k = jnp.concatenate([2 * kh, 2 * kh + 1])
    w = 2.0 * np.pi / l
    ang_b = ((jnp.arange(tr, dtype=jnp.int32)[:, None] * k[None, :]) % l).astype(F32) * w
    ang_a = ((jnp.arange(lh // tr, dtype=jnp.int32)[:, None] * tr * k[None, :]) % l).astype(F32) * w
    ca, sa = jnp.cos(ang_a)[:, None, :], jnp.sin(ang_a)[:, None, :]
    base = pl.BlockSpec((tr, l), lambda i: (0, 0))
    rowv = pl.BlockSpec((1, 1, l), lambda i: (i, 0, 0))
    out = pl.BlockSpec((tr, l), lambda i: (i, 0))
    sd = jax.ShapeDtypeStruct((lh, l), BF16)
    return pl.pallas_call(
        _dft_gen_kernel,
        grid=(lh // tr,),
        in_specs=[base, base, rowv, rowv],
        out_specs=[out, out],
        out_shape=[sd, sd],
        compiler_params=_cparams(("parallel",)),
        name="dft_gen",
    )(jnp.cos(ang_b), jnp.sin(ang_b), ca, sa)


def _group_dft_mat():
    n = FNET_GROUP_DIM
    j = np.arange(n)
    ang = 2.0 * np.pi * ((j[:, None] * j[None, :]) % n) / n
    eye = np.eye(FNET_WIDTH // n)
    mat = np.concatenate([np.kron(eye, np.cos(ang)), np.kron(eye, np.sin(ang))], axis=1)
    return jnp.asarray(mat, F32).astype(BF16)


HEADS_PER_TILE = 4
HEAD_TILE = HEADS_PER_TILE * NA_HEAD_DIM
HEAD_TILES = NA_WIDTH // HEAD_TILE


def _na_kernel(q_ref, k_ref, v_ref, kc_ref, vc_ref, gb_ref, bias_ref, o_ref, *, rows, kh):
    r = pl.program_id(1)
    r0 = jnp.clip(r - kh // 2, 0, rows - kh)
    start = pl.multiple_of(r0 * GRID_W, GRID_W)
    nloc = kh * GRID_W
    row_head = lax.broadcasted_iota(jnp.int32, (HEAD_TILE, HEAD_TILE), 0) // GRID_W
    lane_head = lax.broadcasted_iota(jnp.int32, (HEAD_TILE, HEAD_TILE), 1) // NA_HEAD_DIM
    own = row_head == lane_head
    out_head = lax.broadcasted_iota(jnp.int32, (GRID_W, HEAD_TILE), 1) // NA_HEAD_DIM
    for t in range(HEAD_TILES):
        cs = slice(t * HEAD_TILE, (t + 1) * HEAD_TILE)
        q4 = q_ref[0, :, cs]
        qm = jnp.where(own, jnp.concatenate([q4] * HEADS_PER_TILE, axis=0), 0.0).astype(BF16)
        kl = k_ref[0, pl.ds(start, nloc), cs]
        vl = v_ref[0, pl.ds(start, nloc), cs]
        s_loc = _dot_nt(qm, kl) + bias_ref[0, t]
        s_ctx = _dot_nt(qm, kc_ref[0, :, cs])
        m = jnp.maximum(jnp.max(s_loc, axis=-1, keepdims=True), jnp.max(s_ctx, axis=-1, keepdims=True))
        p_loc = jnp.exp(s_loc - m)
        p_ctx = jnp.exp(s_ctx - m)
        den = jnp.sum(p_loc, axis=-1, keepdims=True) + jnp.sum(p_ctx, axis=-1, keepdims=True)
        o = (_dot(p_loc.astype(BF16), vl) + _dot(p_ctx.astype(BF16), vc_ref[0, :, cs])) / den
        o4 = o[0:GRID_W]
        for h in range(1, HEADS_PER_TILE):
            o4 = jnp.where(out_head == h, o[h * GRID_W:(h + 1) * GRID_W], o4)
        g = gb_ref[0, :, cs].astype(F32)
        o_ref[0, :, cs] = (o4 * _silu(g)).astype(BF16)


def _na_attention(q, k, v, kc, vc, gb, bias):
    b, l, w = q.shape
    lc = kc.shape[1]
    rows = l // GRID_W
    kh = min(NA_ROWS, rows)

    def bias_idx(bi, r):
        r0 = jnp.clip(r - kh // 2, 0, rows - kh)
        return (r0 - r + (NA_ROWS - 1), 0, 0, 0)

    row = pl.BlockSpec((1, GRID_W, w), lambda bi, r: (bi, r, 0))
    full = lambda n: pl.BlockSpec((1, n, w), lambda bi, r: (bi, 0, 0))
    kern = functools.partial(_na_kernel, rows=rows, kh=kh)
    return pl.pallas_call(
        kern,
        grid=(b, rows),
        in_specs=[row, full(l), full(l), full(lc), full(lc), row,
                  pl.BlockSpec((1, HEAD_TILES, HEAD_TILE, kh * GRID_W), bias_idx)],
        out_specs=row,
        out_shape=jax.ShapeDtypeStruct((b, l, w), BF16),
        compiler_params=_cparams(("parallel", "arbitrary")),
        name="na_attention",
    )(q, k, v, kc, vc, gb, bias)


def _na_bias(rpb, kh):
    cq = np.arange(GRID_W)
    ck = np.arange(GRID_W)
    c0 = np.clip(cq - NA_COLS // 2, 0, GRID_W - NA_COLS)
    col_in = (ck[None, :] >= c0[:, None]) & (ck[None, :] < c0[:, None] + NA_COLS)
    dc = np.clip(ck[None, :] - cq[:, None] + (NA_COLS - 1), 0, 2 * NA_COLS - 2)
    onehot = (dc.reshape(1, -1) == np.arange(2 * NA_COLS - 1)[:, None]).astype(np.float32)
    t = jnp.einsum('hrc,cx->hrx', rpb.astype(F32), jnp.asarray(onehot), precision=lax.Precision.HIGHEST)
    t = t.reshape(NA_HEADS, 2 * NA_ROWS - 1, GRID_W, GRID_W)
    t = jnp.where(col_in[None, None], t, -1e30)
    var = jnp.stack([t[:, s:s + kh] for s in range(NA_ROWS)])
    var = var.transpose(0, 1, 3, 2, 4)
    return var.reshape(NA_ROWS, HEAD_TILES, HEAD_TILE, kh * GRID_W)


def _ctx_attn_kernel(q_ref, k_ref, v_ref, gb_ref, o_ref, acc_ref):
    for h in range(NA_HEADS):
        lo, hi = h * NA_HEAD_DIM, (h + 1) * NA_HEAD_DIM
        s = _dot_nt(q_ref[0, :, lo:hi], k_ref[0, :, lo:hi])
        m = jnp.max(s, axis=-1, keepdims=True)
        p = jnp.exp(s - m)
        den = jnp.sum(p, axis=-1, keepdims=True)
        acc_ref[:, lo:hi] = _dot(p.astype(BF16), v_ref[0, :, lo:hi]) / den
    g = gb_ref[0].astype(F32)
    o_ref[0] = (acc_ref[...] * _silu(g)).astype(BF16)


def _ctx_attention(q, k, v, gb):
    b, lc, w = q.shape
    spec = pl.BlockSpec((1, lc, w), lambda bi: (bi, 0, 0))
    return pl.pallas_call(
        _ctx_attn_kernel,
        grid=(b,),
        in_specs=[spec, spec, spec, spec],
        out_specs=spec,
        out_shape=jax.ShapeDtypeStruct((b, lc, w), BF16),
        scratch_shapes=[pltpu.VMEM((lc, w), F32)],
        compiler_params=_cparams(("parallel",)),
        name="ctx_attention",
    )(q, k, v, gb)


def _even_out_kernel(a_ref, n_ref, w_ref, pg_ref, gate_ref, x_ref, o_ref):
    y = _dot(a_ref[0], w_ref[:FNET_WIDTH, :]) + _dot(n_ref[0], w_ref[FNET_WIDTH:, :])
    o_ref[0] = _post(y, pg_ref[...], gate_ref[0], x_ref[0])


def _even_out(a, n, w, post_g, gate, x):
    b, l, d = x.shape
    tm = min(ROW_TILE, l)
    tok = lambda width: pl.BlockSpec((1, tm, width), lambda bi, i: (bi, i, 0))
    return pl.pallas_call(
        _even_out_kernel,
        grid=(b, l // tm),
        in_specs=[tok(FNET_WIDTH), tok(NA_WIDTH),
                  pl.BlockSpec(w.shape, lambda bi, i: (0, 0)),
                  pl.BlockSpec((1, d), lambda bi, i: (0, 0)),
                  pl.BlockSpec((1, 1, d), lambda bi, i: (bi, 0, 0)),
                  tok(d)],
        out_specs=tok(d),
        out_shape=jax.ShapeDtypeStruct((b, l, d), F32),
        compiler_params=_cparams(("parallel", "parallel")),
        name="even_out",
    )(a, n, w, post_g.reshape(1, d), gate, x)


S5_SUPER = 8
N_SUPER = S5_GROUPS // S5_SUPER
SUPER_LANES = S5_SUPER * S5_GROUP_DIM


def _super_spec(tm):
    return pl.BlockSpec((N_SUPER, tm // S5_CHUNK, 1, S5_CHUNK, SUPER_LANES), lambda bi, i: (0, i, bi, 0, 0))


def _super_shape(b, l):
    return jax.ShapeDtypeStruct((N_SUPER, l // S5_CHUNK, b, S5_CHUNK, SUPER_LANES), BF16)


def _odd_in_kernel(x_ref, g_ref, sc_ref, sh_ref, w_ref, ws_ref, bs_ref, sg_g_ref,
                   sg_ref, s_ref, gd_ref, *, tm):
    h = _norm_mod(x_ref[0], g_ref[...], sc_ref[0], sh_ref[0]).astype(BF16)
    wd = SGU_WIDTH
    u = _dot(h, w_ref[:, 0:wd])
    v = _dot(h, w_ref[:, wd:2 * wd])
    gc = _dot(h, w_ref[:, 2 * wd:3 * wd])
    s = _dot(h, w_ref[:, 3 * wd:3 * wd + S5_WIDTH]).astype(BF16)
    for j in range(N_SUPER):
        sj = s[:, j * SUPER_LANES:(j + 1) * SUPER_LANES]
        s_ref[j, :, 0] = sj.reshape(tm // S5_CHUNK, S5_CHUNK, SUPER_LANES)
    gd_ref[0] = _dot(h, w_ref[:, 3 * wd + S5_WIDTH:]).astype(BF16)
    vc = v - jnp.mean(v, axis=-1, keepdims=True)
    vn = (vc * lax.rsqrt(jnp.mean(vc * vc, axis=-1, keepdims=True) + EPS) * sg_g_ref[...]).astype(BF16)
    gate = u * _silu(gc)
    gw = SGU_WIDTH // SGU_GROUPS
    for j in range(tm // SGU_CHUNK):
        rs = slice(j * SGU_CHUNK, (j + 1) * SGU_CHUNK)
        for g in range(SGU_GROUPS):
            cs = slice(g * gw, (g + 1) * gw)
            mixed = _dot(ws_ref[g], vn[rs, cs]) + bs_ref[:, cs]
            sg_ref[0, rs, cs] = (gate[rs, cs] * mixed).astype(BF16)


def _odd_in(x, pre_g, scale, shift, w, ws, bs_full, sgu_g):
    b, l, d = x.shape
    n = w.shape[1]
    tm = min(ROW_TILE, l)
    tok = lambda width: pl.BlockSpec((1, tm, width), lambda bi, i: (bi, i, 0))
    vec = pl.BlockSpec((1, 1, d), lambda bi, i: (bi, 0, 0))
    const2 = lambda a: pl.BlockSpec(a.shape, lambda bi, i: (0,) * a.ndim)
    sd = jax.ShapeDtypeStruct((b, l, SGU_WIDTH), BF16)
    g2 = sgu_g.reshape(1, SGU_WIDTH)
    return pl.pallas_call(
        functools.partial(_odd_in_kernel, tm=tm),
        grid=(b, l // tm),
        in_specs=[tok(d), pl.BlockSpec((1, d), lambda bi, i: (0, 0)), vec, vec,
                  pl.BlockSpec((d, n), lambda bi, i: (0, 0)), const2(ws), const2(bs_full), const2(g2)],
        out_specs=[tok(SGU_WIDTH), _super_spec(tm), tok(S5_WIDTH)],
        out_shape=[sd, _super_shape(b, l), sd],
        compiler_params=_cparams(("parallel", "parallel")),
        name="odd_in",
    )(x, pre_g.reshape(1, d), scale, shift, w, ws, bs_full, g2)


SUPER_K = S5_CHUNK * SUPER_LANES
SG_ROW_TILE = 512
SG_COL_TILE = 1024
REC_LANES = 1024


def _sg_kernel(*refs, n_in):
    xs, ws, o_ref = refs[:n_in], refs[n_in:2 * n_in], refs[2 * n_in]
    acc = _dot(xs[0][0].astype(BF16), ws[0][0])
    for x_ref, w_ref in zip(xs[1:], ws[1:]):
        acc += _dot(x_ref[0].astype(BF16), w_ref[0])
    o_ref[0] = acc.astype(o_ref.dtype)


def _sg_matmul(xs, ws, out_dtype, name):
    _, r, k = xs[0].shape
    n = ws[0].shape[2]
    tr, tn = min(SG_ROW_TILE, r), min(SG_COL_TILE, n)
    n_in = len(xs)
    in_specs = [pl.BlockSpec((1, tr, k), lambda j, ni, ri: (j, ri, 0)) for _ in xs]
    in_specs += [pl.BlockSpec((1, k, tn), lambda j, ni, ri: (j, 0, ni)) for _ in ws]
    return pl.pallas_call(
        functools.partial(_sg_kernel, n_in=n_in),
        grid=(N_SUPER, n // tn, r // tr),
        in_specs=in_specs,
        out_specs=pl.BlockSpec((1, tr, tn), lambda j, ni, ri: (j, ri, ni)),
        out_shape=jax.ShapeDtypeStruct((N_SUPER, r, n), out_dtype),
        compiler_params=_cparams(("parallel", "parallel", "parallel")),
        name=name,
    )(*xs, *ws)


def _rec_kernel(sc_ref, sl_ref, a_ref, hc_ref, hl_ref, *, nb, n_ctx, n_lat):
    half = S5_TILE // 2
    npieces = REC_LANES // S5_TILE
    fwd_lane = lax.broadcasted_iota(jnp.int32, (1, half), 1) < S5_STATE
    coef = [(a_ref[0, :, p * S5_TILE:p * S5_TILE + half], a_ref[0, :, p * S5_TILE + half:(p + 1) * S5_TILE])
            for p in range(npieces)]

    def step(s_ref, h_ref, c, carry, forward):
        rows = pl.ds(pl.multiple_of(c * nb, nb), nb)
        out = []
        for p in range(npieces):
            hre, him = carry[2 * p], carry[2 * p + 1]
            are, aim = coef[p]
            re_sl = slice(p * S5_TILE, p * S5_TILE + half)
            im_sl = slice(p * S5_TILE + half, (p + 1) * S5_TILE)
            if forward:
                h_ref[0, rows, re_sl] = jnp.where(fwd_lane, hre, 0.0)
                h_ref[0, rows, im_sl] = jnp.where(fwd_lane, him, 0.0)
            else:
                h_ref[0, rows, re_sl] = jnp.where(fwd_lane, h_ref[0, rows, re_sl], hre)
                h_ref[0, rows, im_sl] = jnp.where(fwd_lane, h_ref[0, rows, im_sl], him)
            sre, sim = s_ref[0, rows, re_sl], s_ref[0, rows, im_sl]
            out.append(are * hre - aim * him + sre)
            out.append(are * him + aim * hre + sim)
        return tuple(out)

    zero = tuple(jnp.zeros((nb, half), F32) for _ in range(2 * npieces))
    carry = lax.fori_loop(0, n_ctx, lambda c, cr: step(sc_ref, hc_ref, c, cr, True), zero)
    lax.fori_loop(0, n_lat, lambda c, cr: step(sl_ref, hl_ref, c, cr, True), carry)
    carry = lax.fori_loop(0, n_ctx, lambda i, cr: step(sc_ref, hc_ref, n_ctx - 1 - i, cr, False), zero)
    lax.fori_loop(0, n_lat, lambda i, cr: step(sl_ref, hl_ref, n_lat - 1 - i, cr, False), carry)


def _chunk_recurrence(s_c, s_l, a16, nb):
    _, rc, lanes = s_c.shape
    rl = s_l.shape[1]
    spec = lambda r: pl.BlockSpec((1, r, REC_LANES), lambda j, i: (j, 0, i))
    return pl.pallas_call(
        functools.partial(_rec_kernel, nb=nb, n_ctx=rc // nb, n_lat=rl // nb),
        grid=(N_SUPER, lanes // REC_LANES),
        in_specs=[spec(rc), spec(rl), spec(1)],
        out_specs=[spec(rc), spec(rl)],
        out_shape=[jax.ShapeDtypeStruct(s_c.shape, F32), jax.ShapeDtypeStruct(s_l.shape, F32)],
        compiler_params=_cparams(("parallel", "parallel")),
        name="s5_recurrence",
    )(s_c, s_l, a16)


def _cmul(ar, ai, br, bi):
    return ar * br - ai * bi, ar * bi + ai * br


def _s5_matrices(lam_re, lam_im, log_step, b_re, b_im, c_re, c_im):
    t = S5_CHUNK
    hp = lax.Precision.HIGHEST
    taus = jnp.arange(t + 1, dtype=F32)[:, None, None]
    er, ei, wr, wi, a_re, a_im, ks = [], [], [], [], [], [], []
    for d in range(2):
        lr = jnp.minimum(lam_re[d].astype(F32), -1e-4)
        li = lam_im[d].astype(F32)
        dt = jnp.exp(log_step[d].astype(F32))[:, None]
        mag = jnp.exp(lr * dt * taus)
        pr, pi = mag * jnp.cos(li * dt * taus), mag * jnp.sin(li * dt * taus)
        den = lr * lr + li * li
        qr = ((pr[1] - 1.0) * lr + pi[1] * li) / den
        qi = (pi[1] * lr - (pr[1] - 1.0) * li) / den
        bbr, bbi = _cmul(qr[..., None], qi[..., None], b_re[d].astype(F32), b_im[d].astype(F32))
        e_r, e_i = _cmul(pr[..., None], pi[..., None], bbr[None], bbi[None])
        cr, ci = c_re[d].astype(F32), c_im[d].astype(F32)
        w_r, w_i = _cmul(cr[None], ci[None], pr[:, :, None, :], pi[:, :, None, :])
        k = (jnp.einsum('gmp,tgpn->tgmn', cr, e_r[:t], precision=hp)
             - jnp.einsum('gmp,tgpn->tgmn', ci, e_i[:t], precision=hp))
        er.append(e_r); ei.append(e_i); wr.append(w_r); wi.append(w_i); ks.append(k)
        a_re.append(pr[t]); a_im.append(pi[t])
    rows = lambda e: e.transpose(1, 0, 3, 2)
    p_mat = jnp.concatenate([rows(er[0][:t][::-1]), rows(er[1][:t]),
                             rows(ei[0][:t][::-1]), rows(ei[1][:t])], axis=-1)
    cols = lambda w: w.transpose(1, 3, 0, 2)
    q_mat = jnp.concatenate([cols(wr[0][1:]), cols(wr[1][1:][::-1]),
                             -cols(wi[0][1:]), -cols(wi[1][1:][::-1])], axis=1)
    a16 = jnp.concatenate([a_re[0], a_re[1], a_im[0], a_im[1]], axis=-1)

    gd, sl = S5_GROUP_DIM, SUPER_LANES

    def spread(a, nblk):
        r = jnp.arange(nblk * gd)[:, None]
        c = jnp.arange(nblk * sl)[None, :]
        sel = ((r // gd == c // sl) & (r % gd == c % gd)).astype(BF16)
        return jnp.einsum('jrk,kc->jrc', a.astype(BF16), sel, preferred_element_type=F32).astype(BF16)

    def same_group(row_group, col_group):
        return (row_group[:, None] == col_group[None, :]).astype(BF16)[None]

    nlag = 2 * t - 1
    kf = ks[0].transpose(0, 1, 3, 2)
    kb = ks[1].transpose(0, 1, 3, 2)
    klag = jnp.concatenate([kb[1:][::-1], (kf[0] + kb[0])[None], kf[1:]], axis=0)
    k_flat = klag.transpose(1, 2, 0, 3).reshape(N_SUPER, sl, nlag * gd)
    lag_blocks = spread(k_flat, nlag) * same_group(jnp.arange(sl) // gd, (jnp.arange(nlag * sl) % sl) // gd)
    m4 = jnp.concatenate([lag_blocks[:, :, sl * (t - 1 - s):sl * (2 * t - 1 - s)] for s in range(t)], axis=1)
    p_flat = p_mat.reshape(N_SUPER, S5_SUPER, t, gd, S5_TILE).transpose(0, 2, 1, 3, 4).reshape(N_SUPER, SUPER_K, S5_TILE)
    p4 = (jnp.tile(p_flat.astype(BF16), (1, 1, S5_SUPER))
          * same_group((jnp.arange(SUPER_K) % sl) // gd, jnp.arange(SUPER_K) // S5_TILE))
    q_flat = q_mat.reshape(N_SUPER, S5_SUPER * S5_TILE, S5_TILE)
    q4 = spread(q_flat, t) * same_group(jnp.arange(SUPER_K) // S5_TILE, (jnp.arange(SUPER_K) % sl) // gd)
    return m4, p4, q4, a16.reshape(N_SUPER, 1, SUPER_K)


def _s5_scan(s_l, s_c, mats):
    m4, p4, q4, a16 = mats
    b = s_l.shape[2]
    u_l = s_l.reshape(N_SUPER, -1, SUPER_K)
    u_c = s_c.reshape(N_SUPER, -1, SUPER_K)
    sum_l = _sg_matmul([u_l], [p4], F32, "s5_summary")
    sum_c = _sg_matmul([u_c], [p4], F32, "s5_summary_ctx")
    h_c, h_l = _chunk_recurrence(sum_c, sum_l, a16, b)
    y_l = _sg_matmul([u_l, h_l], [m4, q4], BF16, "s5_apply")
    y_c = _sg_matmul([u_c, h_c], [m4, q4], BF16, "s5_apply_ctx")
    return y_l.reshape(s_l.shape), y_c.reshape(s_c.shape)


def _gelu_tanh(x):
    return 0.5 * x * (1.0 + jnp.tanh(float(np.sqrt(2.0 / np.pi)) * (x + 0.044715 * (x * x * x))))


def _odd_out_kernel(sg_ref, y_ref, s_ref, gd_ref, dsk_ref, wg_ref, bg_ref, w_ref, pg_ref, gate_ref, x_ref, o_ref):
    tm = x_ref.shape[1]
    tokens = lambda ref: jnp.concatenate([ref[j, :, 0].reshape(tm, SUPER_LANES) for j in range(N_SUPER)], axis=-1)
    ys = tokens(y_ref).astype(F32)
    s = tokens(s_ref).astype(F32)
    z = _gelu_tanh(ys + dsk_ref[...] * s)
    zz = _dot(z.astype(BF16), wg_ref[...]) + bg_ref[...]
    gd = gd_ref[0].astype(F32)
    ss = zz[:, :S5_WIDTH] * jax.nn.sigmoid(zz[:, S5_WIDTH:]) * _silu(gd)
    y = _dot(sg_ref[0], w_ref[:SGU_WIDTH, :]) + _dot(ss.astype(BF16), w_ref[SGU_WIDTH:, :])
    o_ref[0] = _post(y, pg_ref[...], gate_ref[0], x_ref[0])


def _odd_out(sg, y, s, gd, d_skip, wg, bg, w, post_g, gate, x):
    b, l, d = x.shape
    tm = min(ROW_TILE, l)
    tok = lambda width: pl.BlockSpec((1, tm, width), lambda bi, i: (bi, i, 0))
    const2 = lambda a: pl.BlockSpec(a.shape, lambda bi, i: (0,) * a.ndim)
    dsk = d_skip.reshape(1, S5_WIDTH).astype(F32)
    bg2 = bg.reshape(1, 2 * S5_WIDTH).astype(F32)
    pg2 = post_g.reshape(1, d)
    return pl.pallas_call(
        _odd_out_kernel,
        grid=(b, l // tm),
        in_specs=[tok(SGU_WIDTH), _super_spec(tm), _super_spec(tm), tok(S5_WIDTH),
                  const2(dsk), const2(wg), const2(bg2), const2(w), const2(pg2),
                  pl.BlockSpec((1, 1, d), lambda bi, i: (bi, 0, 0)), tok(d)],
        out_specs=tok(d),
        out_shape=jax.ShapeDtypeStruct((b, l, d), F32),
        compiler_params=_cparams(("parallel", "parallel")),
        name="odd_out",
    )(sg, y, s, gd, dsk, wg, bg2, w, pg2, gate, x)


def kernel(x, c, ctx, c_ctx, w_ada, b_ada, pre_g, post_g, w_in_even, w_out_even, na_rpb,
           w_in_odd, w_out_odd, sgu_w, sgu_b, sgu_g, s5_lam_re, s5_lam_im, s5_log_step,
           s5_b_re, s5_b_im, s5_c_re, s5_c_im, s5_d, glu_w, glu_b):
    b, l, d = x.shape
    lc = ctx.shape[1]
    depth = w_ada.shape[0]
    rows = l // GRID_W
    kh = min(NA_ROWS, rows)

    n_rows = -(-(b + 1) // 8) * 8
    cond = jnp.zeros((n_rows, d), F32).at[:b].set(c).at[b].set(c_ctx)
    mod = _ada_mod(cond, w_ada, b_ada)

    bd = _group_dft_mat()
    cm_l, sm_l = _dft_mats(l)
    cm_c, sm_c = _dft_mats(lc)

    xl, xc = x, ctx
    for i in range(depth):
        last = i == depth - 1
        j = i // 2
        shift, scale, gate = (mod[i, :b, k * d:(k + 1) * d][:, None, :] for k in range(3))
        ctx_mod = jnp.broadcast_to(mod[i, b][None, None, :], (b, 1, 3 * d))
        shift_c, scale_c, gate_c = (ctx_mod[..., k * d:(k + 1) * d] for k in range(3))
        if i % 2 == 0:
            w_in = w_in_even[j].astype(BF16)
            w_out = w_out_even[j].astype(BF16)
            zc, zs, ga, q, k_, v, gb = _even_in(xl, pre_g[i], scale, shift, w_in, bd)
            zc_c, zs_c, ga_c, q_c, k_c, v_c, gb_c = _even_in(xc, pre_g[i], scale_c, shift_c, w_in, bd)
            a_l = _dft_mix(cm_l, sm_l, zc, zs, ga)
            n_l = _na_attention(q, k_, v, k_c, v_c, gb, _na_bias(na_rpb[j], kh))
            xl = _even_out(a_l, n_l, w_out, post_g[i], gate, xl)
            if not last:
                a_c = _dft_mix(cm_c, sm_c, zc_c, zs_c, ga_c)
                n_c = _ctx_attention(q_c, k_c, v_c, gb_c)
                xc = _even_out(a_c, n_c, w_out, post_g[i], gate_c, xc)
        else:
            w_in = w_in_odd[j].astype(BF16)
            w_out = w_out_odd[j].astype(BF16)
            ws = sgu_w[j].astype(BF16)
            gw = SGU_WIDTH // SGU_GROUPS
            bs_full = jnp.repeat(sgu_b[j].astype(F32).T, gw, axis=1)
            mats = _s5_matrices(s5_lam_re[j], s5_lam_im[j], s5_log_step[j], s5_b_re[j], s5_b_im[j],
                                s5_c_re[j], s5_c_im[j])
            sg_l, s_l, gd_l = _odd_in(xl, pre_g[i], scale, shift, w_in, ws, bs_full, sgu_g[j])
            sg_c, s_c, gd_c = _odd_in(xc, pre_g[i], scale_c, shift_c, w_in, ws, bs_full, sgu_g[j])
            y_l, y_c = _s5_scan(s_l, s_c, mats)
            wg = glu_w[j].astype(BF16)
            xl = _odd_out(sg_l, y_l, s_l, gd_l, s5_d[j], wg, glu_b[j], w_out, post_g[i], gate, xl)
            if not last:
                xc = _odd_out(sg_c, y_c, s_c, gd_c, s5_d[j], wg, glu_b[j], w_out, post_g[i], gate_c, xc)
    return xl
```

```python
import functools

import numpy as np
import jax
import jax.numpy as jnp
from jax import lax
from jax.experimental import pallas as pl
from jax.experimental.pallas import tpu as pltpu

F32 = jnp.float32
BF16 = jnp.bfloat16

EPS = 1e-6
GRID_W = 64
FNET_WIDTH = 256
FNET_GROUP_DIM = 64
NA_WIDTH = 768
NA_HEAD_DIM = 64
NA_HEADS = 12
NA_ROWS = 8
NA_COLS = 16
SGU_CHUNK = 128
SGU_WIDTH = 512
SGU_GROUPS = 4
S5_WIDTH = 512
S5_GROUP_DIM = 16
S5_GROUPS = 32
S5_STATE = 64
S5_CHUNK = 16
S5_TILE = S5_CHUNK * S5_GROUP_DIM

V7X_VMEM_BYTES = 64 * 1024 * 1024
VMEM_LIMIT = 48 * 1024 * 1024
ROW_TILE = 512


def _cparams(sem):
    return pltpu.CompilerParams(dimension_semantics=sem, vmem_limit_bytes=VMEM_LIMIT)


def _silu(x):
    return x * jax.nn.sigmoid(x)


def _dot(a, b):
    return jnp.dot(a, b, preferred_element_type=F32)


def _dot_nt(a, b):
    return lax.dot_general(a, b, (((1,), (1,)), ((), ())), preferred_element_type=F32)


def _ada_kernel(c_ref, w_ref, b_ref, o_ref):
    c = c_ref[...]
    o_ref[0] = jnp.dot(_silu(c), w_ref[0], preferred_element_type=F32,
                       precision=lax.Precision.HIGHEST) + b_ref[0]


def _ada_mod(cond, w_ada, b_ada):
    depth, d, n = w_ada.shape
    rows = cond.shape[0]
    tn = 1024
    return pl.pallas_call(
        _ada_kernel,
        grid=(depth, n // tn),
        in_specs=[pl.BlockSpec((rows, d), lambda i, j: (0, 0)),
                  pl.BlockSpec((1, d, tn), lambda i, j: (i, 0, j)),
                  pl.BlockSpec((1, 1, tn), lambda i, j: (i, 0, j))],
        out_specs=pl.BlockSpec((1, rows, tn), lambda i, j: (i, 0, j)),
        out_shape=jax.ShapeDtypeStruct((depth, rows, n), F32),
        compiler_params=_cparams(("parallel", "parallel")),
        name="ada_mod",
    )(cond, w_ada, b_ada.reshape(depth, 1, n))


def _norm_mod(x, g, scale, shift):
    y = x * lax.rsqrt(jnp.mean(x * x, axis=-1, keepdims=True) + EPS)
    return (y * g) * (1.0 + scale) + shift


def _post(y, post_g, gate, x):
    yn = y * lax.rsqrt(jnp.mean(y * y, axis=-1, keepdims=True) + EPS)
    return x + gate * (yn * post_g)


def _even_in_kernel(x_ref, g_ref, sc_ref, sh_ref, w_ref, bd_ref,
                    zc_ref, zs_ref, ga_ref, q_ref, k_ref, v_ref, gb_ref):
    h = _norm_mod(x_ref[0], g_ref[...], sc_ref[0], sh_ref[0]).astype(BF16)
    f0, f1 = 0, FNET_WIDTH
    fa = _dot(h, w_ref[:, f0:f1]).astype(BF16)
    z = _dot(fa, bd_ref[...]).astype(BF16)
    half = z.shape[0] // 2
    r = lax.broadcasted_iota(jnp.int32, (half, 2 * half), 0)
    c = lax.broadcasted_iota(jnp.int32, (half, 2 * half), 1)
    for par in range(2):
        zp = _dot((c == 2 * r + par).astype(BF16), z).astype(BF16)
        zc_ref[par] = zp[:, :FNET_WIDTH]
        zs_ref[par] = zp[:, FNET_WIDTH:]
    o = f1
    ga_ref[0] = _dot(h, w_ref[:, o:o + FNET_WIDTH]).astype(BF16)
    o += FNET_WIDTH
    q_ref[0] = (_dot(h, w_ref[:, o:o + NA_WIDTH]) * (NA_HEAD_DIM ** -0.5)).astype(BF16)
    o += NA_WIDTH
    k_ref[0] = _dot(h, w_ref[:, o:o + NA_WIDTH]).astype(BF16)
    o += NA_WIDTH
    v_ref[0] = _dot(h, w_ref[:, o:o + NA_WIDTH]).astype(BF16)
    o += NA_WIDTH
    gb_ref[0] = _dot(h, w_ref[:, o:o + NA_WIDTH]).astype(BF16)


def _even_in(x, pre_g, scale, shift, w, bd):
    b, l, d = x.shape
    n = w.shape[1]
    tm = min(ROW_TILE, l)
    tok = lambda width: pl.BlockSpec((1, tm, width), lambda bi, i: (bi, i, 0))
    zspec = pl.BlockSpec((2, tm // 2, FNET_WIDTH), lambda bi, i: (0, i, bi))
    vec = pl.BlockSpec((1, 1, d), lambda bi, i: (bi, 0, 0))
    sd = lambda width: jax.ShapeDtypeStruct((b, l, width), BF16)
    zsd = jax.ShapeDtypeStruct((2, l // 2, b * FNET_WIDTH), BF16)
    return pl.pallas_call(
        _even_in_kernel,
        grid=(b, l // tm),
        in_specs=[tok(d), pl.BlockSpec((1, d), lambda bi, i: (0, 0)), vec, vec,
                  pl.BlockSpec((d, n), lambda bi, i: (0, 0)),
                  pl.BlockSpec(bd.shape, lambda bi, i: (0, 0))],
        out_specs=[zspec, zspec, tok(FNET_WIDTH), tok(NA_WIDTH), tok(NA_WIDTH), tok(NA_WIDTH), tok(NA_WIDTH)],
        out_shape=[zsd, zsd, sd(FNET_WIDTH), sd(NA_WIDTH), sd(NA_WIDTH), sd(NA_WIDTH), sd(NA_WIDTH)],
        compiler_params=_cparams(("parallel", "parallel")),
        name="even_in",
    )(x, pre_g.reshape(1, d), scale, shift, w, bd)


def _dft_kernel(ce_ref, se_ref, co_ref, so_ref, zc_ref, zs_ref, ga_ref, o_ref, acc_e, acc_o, *, nb, norm):
    kk = pl.program_id(1)

    @pl.when(kk == 0)
    def _():
        acc_e[...] = jnp.zeros_like(acc_e)
        acc_o[...] = jnp.zeros_like(acc_o)

    acc_e[...] += _dot(ce_ref[...], zc_ref[0]) + _dot(se_ref[...], zs_ref[0])
    acc_o[...] += _dot(co_ref[...], zc_ref[1]) + _dot(so_ref[...], zs_ref[1])

    @pl.when(kk == pl.num_programs(1) - 1)
    def _():
        for bi in range(nb):
            cs = slice(bi * FNET_WIDTH, (bi + 1) * FNET_WIDTH)
            e, o = acc_e[:, cs], acc_o[:, cs]
            for half, y in enumerate((e + o, e - o)):
                g = ga_ref[bi, half].astype(F32)
                o_ref[bi, half] = (y * norm * _silu(g)).astype(BF16)


def _dft_mix(cm, sm, zc, zs, ga):
    b, l, _ = ga.shape
    lh = l // 2
    tm = min(512, lh)
    tk = min(256, lh)
    nk = lh // tk
    nc = b * FNET_WIDTH
    kern = functools.partial(_dft_kernel, nb=b, norm=float((l * FNET_GROUP_DIM) ** -0.5))
    mat_e = pl.BlockSpec((tm, tk), lambda i, k: (i, k))
    mat_o = pl.BlockSpec((tm, tk), lambda i, k: (i, nk + k))
    zspec = pl.BlockSpec((2, tk, nc), lambda i, k: (0, k, 0))
    halves = pl.BlockSpec((b, 2, tm, FNET_WIDTH), lambda i, k: (0, 0, i, 0))
    out = pl.pallas_call(
        kern,
        grid=(lh // tm, nk),
        in_specs=[mat_e, mat_e, mat_o, mat_o, zspec, zspec, halves],
        out_specs=halves,
        out_shape=jax.ShapeDtypeStruct((b, 2, lh, FNET_WIDTH), BF16),
        scratch_shapes=[pltpu.VMEM((tm, nc), F32), pltpu.VMEM((tm, nc), F32)],
        compiler_params=_cparams(("parallel", "arbitrary")),
        name="dft_mix",
    )(cm, sm, cm, sm, zc, zs, ga.reshape(b, 2, lh, FNET_WIDTH))
    return out.reshape(b, l, FNET_WIDTH)


def _dft_gen_kernel(cb_ref, sb_ref, ca_ref, sa_ref, c_ref, s_ref):
    ca, sa = ca_ref[0], sa_ref[0]
    cb, sb = cb_ref[...], sb_ref[...]
    c_ref[...] = (ca * cb - sa * sb).astype(BF16)
    s_ref[...] = (-(sa * cb + ca * sb)).astype(BF16)


def _dft_mats(l):
    lh = l // 2
    tr = min(256, lh)
    kh = jnp.arange(lh, dtype=jnp.int32)
    k = jnp.concatenate([2 * kh, 2 * kh + 1])
    w = 2.0 * np.pi / l
    ang_b = ((jnp.arange(tr, dtype=jnp.int32)[:, None] * k[None, :]) % l).astype(F32) * w
    ang_a = ((jnp.arange(lh // tr, dtype=jnp.int32)[:, None] * tr * k[None, :]) % l).astype(F32) * w
    ca, sa = jnp.cos(ang_a)[:, None, :], jnp.sin(ang_a)[:, None, :]
    base = pl.BlockSpec((tr, l), lambda i: (0, 0))
    rowv = pl.BlockSpec((1, 1, l), lambda i: (i, 0, 0))
    out = pl.BlockSpec((tr, l), lambda i: (i, 0))
    sd = jax.ShapeDtypeStruct((lh, l), BF16)
    return pl.pallas_call(
        _dft_gen_kernel,
        grid=(lh // tr,),
        in_specs=[base, base, rowv, rowv],
        out_specs=[out, out],
        out_shape=[sd, sd],
        compiler_params=_cparams(("parallel",)),
        name="dft_gen",
    )(jnp.cos(ang_b), jnp.sin(ang_b), ca, sa)


def _group_dft_mat():
    n = FNET_GROUP_DIM
    j = np.arange(n)
    ang = 2.0 * np.pi * ((j[:, None] * j[None, :]) % n) / n
    eye = np.eye(FNET_WIDTH // n)
    mat = np.concatenate([np.kron(eye, np.cos(ang)), np.kron(eye, np.sin(ang))], axis=1)
    return jnp.asarray(mat, F32).astype(BF16)


HEADS_PER_TILE = 4
HEAD_TILE = HEADS_PER_TILE * NA_HEAD_DIM
HEAD_TILES = NA_WIDTH // HEAD_TILE


NA_ROWS_PER_STEP = 4


def _na_kernel(q_ref, k_ref, v_ref, kc_ref, vc_ref, gb_ref, *rest, rows, kh, rps):
    bias_refs, o_ref = rest[:rps], rest[rps]
    nloc = kh * GRID_W
    row_head = lax.broadcasted_iota(jnp.int32, (HEAD_TILE, HEAD_TILE), 0) // GRID_W
    lane_head = lax.broadcasted_iota(jnp.int32, (HEAD_TILE, HEAD_TILE), 1) // NA_HEAD_DIM
    own = row_head == lane_head
    out_head = lax.broadcasted_iota(jnp.int32, (GRID_W, HEAD_TILE), 1) // NA_HEAD_DIM
    for rr in range(rps):
        r = pl.program_id(1) * rps + rr
        r0 = jnp.clip(r - kh // 2, 0, rows - kh)
        start = pl.multiple_of(r0 * GRID_W, GRID_W)
        qs = slice(rr * GRID_W, (rr + 1) * GRID_W)
        for t in range(HEAD_TILES):
            cs = slice(t * HEAD_TILE, (t + 1) * HEAD_TILE)
            q4 = q_ref[0, qs, cs]
            qm = jnp.where(own, jnp.concatenate([q4] * HEADS_PER_TILE, axis=0), 0.0).astype(BF16)
            kl = k_ref[0, pl.ds(start, nloc), cs]
            vl = v_ref[0, pl.ds(start, nloc), cs]
            s_loc = _dot_nt(qm, kl) + bias_refs[rr][0, t]
            s_ctx = _dot_nt(qm, kc_ref[0, :, cs])
            m = jnp.maximum(jnp.max(s_loc, axis=-1, keepdims=True), jnp.max(s_ctx, axis=-1, keepdims=True))
            p_loc = jnp.exp(s_loc - m)
            p_ctx = jnp.exp(s_ctx - m)
            den = jnp.sum(p_loc, axis=-1, keepdims=True) + jnp.sum(p_ctx, axis=-1, keepdims=True)
            o = (_dot(p_loc.astype(BF16), vl) + _dot(p_ctx.astype(BF16), vc_ref[0, :, cs])) / den
            o4 = o[0:GRID_W]
            for h in range(1, HEADS_PER_TILE):
                o4 = jnp.where(out_head == h, o[h * GRID_W:(h + 1) * GRID_W], o4)
            g = gb_ref[0, qs, cs].astype(F32)
            o_ref[0, qs, cs] = (o4 * _silu(g)).astype(BF16)


def _na_attention(q, k, v, kc, vc, gb, bias):
    b, l, w = q.shape
    lc = kc.shape[1]
    rows = l // GRID_W
    kh = min(NA_ROWS, rows)
    rps = NA_ROWS_PER_STEP

    def bias_spec(rr):
        def idx(bi, i):
            r = i * rps + rr
            r0 = jnp.clip(r - kh // 2, 0, rows - kh)
            return (r0 - r + (NA_ROWS - 1), 0, 0, 0)
        return pl.BlockSpec((1, HEAD_TILES, HEAD_TILE, kh * GRID_W), idx)

    row = pl.BlockSpec((1, rps * GRID_W, w), lambda bi, i: (bi, i, 0))
    full = lambda n: pl.BlockSpec((1, n, w), lambda bi, i: (bi, 0, 0))
    kern = functools.partial(_na_kernel, rows=rows, kh=kh, rps=rps)
    return pl.pallas_call(
        kern,
        grid=(b, rows // rps),
        in_specs=[row, full(l), full(l), full(lc), full(lc), row] + [bias_spec(rr) for rr in range(rps)],
        out_specs=row,
        out_shape=jax.ShapeDtypeStruct((b, l, w), BF16),
        compiler_params=_cparams(("parallel", "arbitrary")),
        name="na_attention",
    )(q, k, v, kc, vc, gb, *([bias] * rps))


def _na_bias(rpb, kh):
    cq = np.arange(GRID_W)
    ck = np.arange(GRID_W)
    c0 = np.clip(cq - NA_COLS // 2, 0, GRID_W - NA_COLS)
    col_in = (ck[None, :] >= c0[:, None]) & (ck[None, :] < c0[:, None] + NA_COLS)
    dc = np.clip(ck[None, :] - cq[:, None] + (NA_COLS - 1), 0, 2 * NA_COLS - 2)
    onehot = (dc.reshape(1, -1) == np.arange(2 * NA_COLS - 1)[:, None]).astype(np.float32)
    t = jnp.einsum('hrc,cx->hrx', rpb.astype(F32), jnp.asarray(onehot), precision=lax.Precision.HIGHEST)
    t = t.reshape(NA_HEADS, 2 * NA_ROWS - 1, GRID_W, GRID_W)
    t = jnp.where(col_in[None, None], t, -1e30)
    var = jnp.stack([t[:, s:s + kh] for s in range(NA_ROWS)])
    var = var.transpose(0, 1, 3, 2, 4)
    return var.reshape(NA_ROWS, HEAD_TILES, HEAD_TILE, kh * GRID_W)


def _ctx_attn_kernel(q_ref, k_ref, v_ref, gb_ref, o_ref, acc_ref):
    for h in range(NA_HEADS):
        lo, hi = h * NA_HEAD_DIM, (h + 1) * NA_HEAD_DIM
        s = _dot_nt(q_ref[0, :, lo:hi], k_ref[0, :, lo:hi])
        m = jnp.max(s, axis=-1, keepdims=True)
        p = jnp.exp(s - m)
        den = jnp.sum(p, axis=-1, keepdims=True)
        acc_ref[:, lo:hi] = _dot(p.astype(BF16), v_ref[0, :, lo:hi]) / den
    g = gb_ref[0].astype(F32)
    o_ref[0] = (acc_ref[...] * _silu(g)).astype(BF16)


def _ctx_attention(q, k, v, gb):
    b, lc, w = q.shape
    spec = pl.BlockSpec((1, lc, w), lambda bi: (bi, 0, 0))
    return pl.pallas_call(
        _ctx_attn_kernel,
        grid=(b,),
        in_specs=[spec, spec, spec, spec],
        out_specs=spec,
        out_shape=jax.ShapeDtypeStruct((b, lc, w), BF16),
        scratch_shapes=[pltpu.VMEM((lc, w), F32)],
        compiler_params=_cparams(("parallel",)),
        name="ctx_attention",
    )(q, k, v, gb)


def _even_out_kernel(a_ref, n_ref, w_ref, pg_ref, gate_ref, x_ref, o_ref):
    y = _dot(a_ref[0], w_ref[:FNET_WIDTH, :]) + _dot(n_ref[0], w_ref[FNET_WIDTH:, :])
    o_ref[0] = _post(y, pg_ref[...], gate_ref[0], x_ref[0])


def _even_out(a, n, w, post_g, gate, x):
    b, l, d = x.shape
    tm = min(ROW_TILE, l)
    tok = lambda width: pl.BlockSpec((1, tm, width), lambda bi, i: (bi, i, 0))
    return pl.pallas_call(
        _even_out_kernel,
        grid=(b, l // tm),
        in_specs=[tok(FNET_WIDTH), tok(NA_WIDTH),
                  pl.BlockSpec(w.shape, lambda bi, i: (0, 0)),
                  pl.BlockSpec((1, d), lambda bi, i: (0, 0)),
                  pl.BlockSpec((1, 1, d), lambda bi, i: (bi, 0, 0)),
                  tok(d)],
        out_specs=tok(d),
        out_shape=jax.ShapeDtypeStruct((b, l, d), F32),
        compiler_params=_cparams(("parallel", "parallel")),
        name="even_out",
    )(a, n, w, post_g.reshape(1, d), gate, x)


S5_SUPER = 8
N_SUPER = S5_GROUPS // S5_SUPER
SUPER_LANES = S5_SUPER * S5_GROUP_DIM


def _super_spec(tm):
    return pl.BlockSpec((N_SUPER, tm // S5_CHUNK, 1, S5_CHUNK, SUPER_LANES), lambda bi, i: (0, i, bi, 0, 0))


def _super_shape(b, l):
    return jax.ShapeDtypeStruct((N_SUPER, l // S5_CHUNK, b, S5_CHUNK, SUPER_LANES), BF16)


def _odd_in_kernel(x_ref, g_ref, sc_ref, sh_ref, w_ref, ws_ref, bs_ref, sg_g_ref,
                   sg_ref, s_ref, gd_ref, *, tm):
    h = _norm_mod(x_ref[0], g_ref[...], sc_ref[0], sh_ref[0]).astype(BF16)
    wd = SGU_WIDTH
    u = _dot(h, w_ref[:, 0:wd])
    v = _dot(h, w_ref[:, wd:2 * wd])
    gc = _dot(h, w_ref[:, 2 * wd:3 * wd])
    s = _dot(h, w_ref[:, 3 * wd:3 * wd + S5_WIDTH]).astype(BF16)
    for j in range(N_SUPER):
        sj = s[:, j * SUPER_LANES:(j + 1) * SUPER_LANES]
        s_ref[j, :, 0] = sj.reshape(tm // S5_CHUNK, S5_CHUNK, SUPER_LANES)
    gd_ref[0] = _dot(h, w_ref[:, 3 * wd + S5_WIDTH:]).astype(BF16)
    vc = v - jnp.mean(v, axis=-1, keepdims=True)
    vn = (vc * lax.rsqrt(jnp.mean(vc * vc, axis=-1, keepdims=True) + EPS) * sg_g_ref[...]).astype(BF16)
    gate = u * _silu(gc)
    gw = SGU_WIDTH // SGU_GROUPS
    for j in range(tm // SGU_CHUNK):
        rs = slice(j * SGU_CHUNK, (j + 1) * SGU_CHUNK)
        for g in range(SGU_GROUPS):
            cs = slice(g * gw, (g + 1) * gw)
            mixed = _dot(ws_ref[g], vn[rs, cs]) + bs_ref[:, cs]
            sg_ref[0, rs, cs] = (gate[rs, cs] * mixed).astype(BF16)


def _odd_in(x, pre_g, scale, shift, w, ws, bs_full, sgu_g):
    b, l, d = x.shape
    n = w.shape[1]
    tm = min(ROW_TILE, l)
    tok = lambda width: pl.BlockSpec((1, tm, width), lambda bi, i: (bi, i, 0))
    vec = pl.BlockSpec((1, 1, d), lambda bi, i: (bi, 0, 0))
    const2 = lambda a: pl.BlockSpec(a.shape, lambda bi, i: (0,) * a.ndim)
    sd = jax.ShapeDtypeStruct((b, l, SGU_WIDTH), BF16)
    g2 = sgu_g.reshape(1, SGU_WIDTH)
    return pl.pallas_call(
        functools.partial(_odd_in_kernel, tm=tm),
        grid=(b, l // tm),
        in_specs=[tok(d), pl.BlockSpec((1, d), lambda bi, i: (0, 0)), vec, vec,
                  pl.BlockSpec((d, n), lambda bi, i: (0, 0)), const2(ws), const2(bs_full), const2(g2)],
        out_specs=[tok(SGU_WIDTH), _super_spec(tm), tok(S5_WIDTH)],
        out_shape=[sd, _super_shape(b, l), sd],
        compiler_params=_cparams(("parallel", "parallel")),
        name="odd_in",
    )(x, pre_g.reshape(1, d), scale, shift, w, ws, bs_full, g2)


SUPER_K = S5_CHUNK * SUPER_LANES
SG_ROW_TILE = 512
REC_LANES = 1024


def _summary_kernel(u_ref, p_ref, o_ref, w_scr):
    @pl.when(pl.program_id(1) == 0)
    def _():
        row_group = (lax.broadcasted_iota(jnp.int32, (SUPER_K, S5_TILE), 0) % SUPER_LANES) // S5_GROUP_DIM
        p = p_ref[0]
        for h in range(S5_SUPER):
            w_scr[:, h * S5_TILE:(h + 1) * S5_TILE] = jnp.where(row_group == h, p, 0.0).astype(BF16)

    o_ref[0] = _dot(u_ref[0], w_scr[...])


def _s5_summary(u, p_flat, name):
    _, r, k = u.shape
    tr = min(SG_ROW_TILE, r)
    return pl.pallas_call(
        _summary_kernel,
        grid=(N_SUPER, r // tr),
        in_specs=[pl.BlockSpec((1, tr, k), lambda j, ri: (j, ri, 0)),
                  pl.BlockSpec((1, k, S5_TILE), lambda j, ri: (j, 0, 0))],
        out_specs=pl.BlockSpec((1, tr, k), lambda j, ri: (j, ri, 0)),
        out_shape=jax.ShapeDtypeStruct((N_SUPER, r, k), F32),
        scratch_shapes=[pltpu.VMEM((k, k), BF16)],
        compiler_params=_cparams(("parallel", "arbitrary")),
        name=name,
    )(u, p_flat)


def _apply_kernel(u_ref, h_ref, lag_ref, q_ref, o_ref, w_scr):
    @pl.when(pl.program_id(1) == 0)
    def _():
        for s in range(S5_CHUNK):
            lo = SUPER_LANES * (S5_CHUNK - 1 - s)
            w_scr[s * SUPER_LANES:(s + 1) * SUPER_LANES, :] = lag_ref[0, :, lo:lo + SUPER_K]

    acc = _dot(u_ref[0], w_scr[...]) + _dot(h_ref[0].astype(BF16), q_ref[0])
    o_ref[0] = acc.astype(o_ref.dtype)


def _s5_apply(u, h, lag_blocks, q4, name):
    _, r, k = u.shape
    tr = min(SG_ROW_TILE, r)
    rows = pl.BlockSpec((1, tr, k), lambda j, ri: (j, ri, 0))
    return pl.pallas_call(
        _apply_kernel,
        grid=(N_SUPER, r // tr),
        in_specs=[rows, rows,
                  pl.BlockSpec((1,) + lag_blocks.shape[1:], lambda j, ri: (j, 0, 0)),
                  pl.BlockSpec((1, k, k), lambda j, ri: (j, 0, 0))],
        out_specs=rows,
        out_shape=jax.ShapeDtypeStruct((N_SUPER, r, k), BF16),
        scratch_shapes=[pltpu.VMEM((k, k), BF16)],
        compiler_params=_cparams(("parallel", "arbitrary")),
        name=name,
    )(u, h, lag_blocks, q4)


def _rec_kernel(sc_ref, sl_ref, a_ref, hc_ref, hl_ref, *, nb, n_ctx, n_lat):
    half = S5_TILE // 2
    npieces = REC_LANES // S5_TILE
    fwd_lane = lax.broadcasted_iota(jnp.int32, (1, half), 1) < S5_STATE
    coef = [(a_ref[0, :, p * S5_TILE:p * S5_TILE + half], a_ref[0, :, p * S5_TILE + half:(p + 1) * S5_TILE])
            for p in range(npieces)]

    def step(s_ref, h_ref, c, carry, forward):
        rows = pl.ds(pl.multiple_of(c * nb, nb), nb)
        out = []
        for p in range(npieces):
            hre, him = carry[2 * p], carry[2 * p + 1]
            are, aim = coef[p]
            re_sl = slice(p * S5_TILE, p * S5_TILE + half)
            im_sl = slice(p * S5_TILE + half, (p + 1) * S5_TILE)
            if forward:
                h_ref[0, rows, re_sl] = jnp.where(fwd_lane, hre, 0.0)
                h_ref[0, rows, im_sl] = jnp.where(fwd_lane, him, 0.0)
            else:
                h_ref[0, rows, re_sl] = jnp.where(fwd_lane, h_ref[0, rows, re_sl], hre)
                h_ref[0, rows, im_sl] = jnp.where(fwd_lane, h_ref[0, rows, im_sl], him)
            sre, sim = s_ref[0, rows, re_sl], s_ref[0, rows, im_sl]
            out.append(are * hre - aim * him + sre)
            out.append(are * him + aim * hre + sim)
        return tuple(out)

    zero = tuple(jnp.zeros((nb, half), F32) for _ in range(2 * npieces))
    carry = lax.fori_loop(0, n_ctx, lambda c, cr: step(sc_ref, hc_ref, c, cr, True), zero)
    lax.fori_loop(0, n_lat, lambda c, cr: step(sl_ref, hl_ref, c, cr, True), carry)
    carry = lax.fori_loop(0, n_ctx, lambda i, cr: step(sc_ref, hc_ref, n_ctx - 1 - i, cr, False), zero)
    lax.fori_loop(0, n_lat, lambda i, cr: step(sl_ref, hl_ref, n_lat - 1 - i, cr, False), carry)


def _chunk_recurrence(s_c, s_l, a16, nb):
    _, rc, lanes = s_c.shape
    rl = s_l.shape[1]
    spec = lambda r: pl.BlockSpec((1, r, REC_LANES), lambda j, i: (j, 0, i))
    return pl.pallas_call(
        functools.partial(_rec_kernel, nb=nb, n_ctx=rc // nb, n_lat=rl // nb),
        grid=(N_SUPER, lanes // REC_LANES),
        in_specs=[spec(rc), spec(rl), spec(1)],
        out_specs=[spec(rc), spec(rl)],
        out_shape=[jax.ShapeDtypeStruct(s_c.shape, F32), jax.ShapeDtypeStruct(s_l.shape, F32)],
        compiler_params=_cparams(("parallel", "parallel")),
        name="s5_recurrence",
    )(s_c, s_l, a16)


def _cmul(ar, ai, br, bi):
    return ar * br - ai * bi, ar * bi + ai * br


def _s5_matrices(lam_re, lam_im, log_step, b_re, b_im, c_re, c_im):
    t = S5_CHUNK
    hp = lax.Precision.HIGHEST
    taus = jnp.arange(t + 1, dtype=F32)[:, None, None]
    er, ei, wr, wi, a_re, a_im, ks = [], [], [], [], [], [], []
    for d in range(2):
        lr = jnp.minimum(lam_re[d].astype(F32), -1e-4)
        li = lam_im[d].astype(F32)
        dt = jnp.exp(log_step[d].astype(F32))[:, None]
        mag = jnp.exp(lr * dt * taus)
        pr, pi = mag * jnp.cos(li * dt * taus), mag * jnp.sin(li * dt * taus)
        den = lr * lr + li * li
        qr = ((pr[1] - 1.0) * lr + pi[1] * li) / den
        qi = (pi[1] * lr - (pr[1] - 1.0) * li) / den
        bbr, bbi = _cmul(qr[..., None], qi[..., None], b_re[d].astype(F32), b_im[d].astype(F32))
        e_r, e_i = _cmul(pr[..., None], pi[..., None], bbr[None], bbi[None])
        cr, ci = c_re[d].astype(F32), c_im[d].astype(F32)
        w_r, w_i = _cmul(cr[None], ci[None], pr[:, :, None, :], pi[:, :, None, :])
        k = (jnp.einsum('gmp,tgpn->tgmn', cr, e_r[:t], precision=hp)
             - jnp.einsum('gmp,tgpn->tgmn', ci, e_i[:t], precision=hp))
        er.append(e_r); ei.append(e_i); wr.append(w_r); wi.append(w_i); ks.append(k)
        a_re.append(pr[t]); a_im.append(pi[t])
    rows = lambda e: e.transpose(1, 0, 3, 2)
    p_mat = jnp.concatenate([rows(er[0][:t][::-1]), rows(er[1][:t]),
                             rows(ei[0][:t][::-1]), rows(ei[1][:t])], axis=-1)
    cols = lambda w: w.transpose(1, 3, 0, 2)
    q_mat = jnp.concatenate([cols(wr[0][1:]), cols(wr[1][1:][::-1]),
                             -cols(wi[0][1:]), -cols(wi[1][1:][::-1])], axis=1)
    a16 = jnp.concatenate([a_re[0], a_re[1], a_im[0], a_im[1]], axis=-1)

    gd, sl = S5_GROUP_DIM, SUPER_LANES

    def spread(a, nblk):
        r = jnp.arange(nblk * gd)[:, None]
        c = jnp.arange(nblk * sl)[None, :]
        sel = ((r // gd == c // sl) & (r % gd == c % gd)).astype(BF16)
        return jnp.einsum('jrk,kc->jrc', a.astype(BF16), sel, preferred_element_type=F32).astype(BF16)

    def same_group(row_group, col_group):
        return (row_group[:, None] == col_group[None, :]).astype(BF16)[None]

    nlag = 2 * t - 1
    kf = ks[0].transpose(0, 1, 3, 2)
    kb = ks[1].transpose(0, 1, 3, 2)
    klag = jnp.concatenate([kb[1:][::-1], (kf[0] + kb[0])[None], kf[1:]], axis=0)
    k_flat = klag.transpose(1, 2, 0, 3).reshape(N_SUPER, sl, nlag * gd)
    lag_blocks = spread(k_flat, nlag) * same_group(jnp.arange(sl) // gd, (jnp.arange(nlag * sl) % sl) // gd)
    p_flat = p_mat.reshape(N_SUPER, S5_SUPER, t, gd, S5_TILE).transpose(0, 2, 1, 3, 4).reshape(N_SUPER, SUPER_K, S5_TILE)
    q_flat = q_mat.reshape(N_SUPER, S5_SUPER * S5_TILE, S5_TILE)
    q4 = spread(q_flat, t) * same_group(jnp.arange(SUPER_K) // S5_TILE, (jnp.arange(SUPER_K) % sl) // gd)
    return lag_blocks, p_flat.astype(BF16), q4, a16.reshape(N_SUPER, 1, SUPER_K)


def _s5_scan(s_l, s_c, mats):
    lag_blocks, p_flat, q4, a16 = mats
    b = s_l.shape[2]
    u_l = s_l.reshape(N_SUPER, -1, SUPER_K)
    u_c = s_c.reshape(N_SUPER, -1, SUPER_K)
    sum_l = _s5_summary(u_l, p_flat, "s5_summary")
    sum_c = _s5_summary(u_c, p_flat, "s5_summary_ctx")
    h_c, h_l = _chunk_recurrence(sum_c, sum_l, a16, b)
    y_l = _s5_apply(u_l, h_l, lag_blocks, q4, "s5_apply")
    y_c = _s5_apply(u_c, h_c, lag_blocks, q4, "s5_apply_ctx")
    return y_l.reshape(s_l.shape), y_c.reshape(s_c.shape)


def _gelu_tanh(x):
    return 0.5 * x * (1.0 + jnp.tanh(float(np.sqrt(2.0 / np.pi)) * (x + 0.044715 * (x * x * x))))


def _odd_out_kernel(sg_ref, y_ref, s_ref, gd_ref, dsk_ref, wg_ref, bg_ref, w_ref, pg_ref, gate_ref, x_ref, o_ref):
    tm = x_ref.shape[1]
    tokens = lambda ref: jnp.concatenate([ref[j, :, 0].reshape(tm, SUPER_LANES) for j in range(N_SUPER)], axis=-1)
    ys = tokens(y_ref).astype(F32)
    s = tokens(s_ref).astype(F32)
    z = _gelu_tanh(ys + dsk_ref[...] * s)
    zz = _dot(z.astype(BF16), wg_ref[...]) + bg_ref[...]
    gd = gd_ref[0].astype(F32)
    ss = zz[:, :S5_WIDTH] * jax.nn.sigmoid(zz[:, S5_WIDTH:]) * _silu(gd)
    y = _dot(sg_ref[0], w_ref[:SGU_WIDTH, :]) + _dot(ss.astype(BF16), w_ref[SGU_WIDTH:, :])
    o_ref[0] = _post(y, pg_ref[...], gate_ref[0], x_ref[0])


def _odd_out(sg, y, s, gd, d_skip, wg, bg, w, post_g, gate, x):
    b, l, d = x.shape
    tm = min(ROW_TILE, l)
    tok = lambda width: pl.BlockSpec((1, tm, width), lambda bi, i: (bi, i, 0))
    const2 = lambda a: pl.BlockSpec(a.shape, lambda bi, i: (0,) * a.ndim)
    dsk = d_skip.reshape(1, S5_WIDTH).astype(F32)
    bg2 = bg.reshape(1, 2 * S5_WIDTH).astype(F32)
    pg2 = post_g.reshape(1, d)
    return pl.pallas_call(
        _odd_out_kernel,
        grid=(b, l // tm),
        in_specs=[tok(SGU_WIDTH), _super_spec(tm), _super_spec(tm), tok(S5_WIDTH),
                  const2(dsk), const2(wg), const2(bg2), const2(w), const2(pg2),
                  pl.BlockSpec((1, 1, d), lambda bi, i: (bi, 0, 0)), tok(d)],
        out_specs=tok(d),
        out_shape=jax.ShapeDtypeStruct((b, l, d), F32),
        compiler_params=_cparams(("parallel", "parallel")),
        name="odd_out",
    )(sg, y, s, gd, dsk, wg, bg2, w, pg2, gate, x)


def kernel(x, c, ctx, c_ctx, w_ada, b_ada, pre_g, post_g, w_in_even, w_out_even, na_rpb,
           w_in_odd, w_out_odd, sgu_w, sgu_b, sgu_g, s5_lam_re, s5_lam_im, s5_log_step,
           s5_b_re, s5_b_im, s5_c_re, s5_c_im, s5_d, glu_w, glu_b):
    b, l, d = x.shape
    lc = ctx.shape[1]
    depth = w_ada.shape[0]
    rows = l // GRID_W
    kh = min(NA_ROWS, rows)

    n_rows = -(-(b + 1) // 8) * 8
    cond = jnp.zeros((n_rows, d), F32).at[:b].set(c).at[b].set(c_ctx)
    mod = _ada_mod(cond, w_ada, b_ada)

    bd = _group_dft_mat()
    cm_l, sm_l = _dft_mats(l)
    cm_c, sm_c = _dft_mats(lc)

    xl, xc = x, ctx
    for i in range(depth):
        last = i == depth - 1
        j = i // 2
        shift, scale, gate = (mod[i, :b, k * d:(k + 1) * d][:, None, :] for k in range(3))
        ctx_mod = jnp.broadcast_to(mod[i, b][None, None, :], (b, 1, 3 * d))
        shift_c, scale_c, gate_c = (ctx_mod[..., k * d:(k + 1) * d] for k in range(3))
        if i % 2 == 0:
            w_in = w_in_even[j].astype(BF16)
            w_out = w_out_even[j].astype(BF16)
            zc, zs, ga, q, k_, v, gb = _even_in(xl, pre_g[i], scale, shift, w_in, bd)
            zc_c, zs_c, ga_c, q_c, k_c, v_c, gb_c = _even_in(xc, pre_g[i], scale_c, shift_c, w_in, bd)
            a_l = _dft_mix(cm_l, sm_l, zc, zs, ga)
            n_l = _na_attention(q, k_, v, k_c, v_c, gb, _na_bias(na_rpb[j], kh))
            xl = _even_out(a_l, n_l, w_out, post_g[i], gate, xl)
            if not last:
                a_c = _dft_mix(cm_c, sm_c, zc_c, zs_c, ga_c)
                n_c = _ctx_attention(q_c, k_c, v_c, gb_c)
                xc = _even_out(a_c, n_c, w_out, post_g[i], gate_c, xc)
        else:
            w_in = w_in_odd[j].astype(BF16)
            w_out = w_out_odd[j].astype(BF16)
            ws = sgu_w[j].astype(BF16)
            gw = SGU_WIDTH // SGU_GROUPS
            bs_full = jnp.repeat(sgu_b[j].astype(F32).T, gw, axis=1)
            mats = _s5_matrices(s5_lam_re[j], s5_lam_im[j], s5_log_step[j], s5_b_re[j], s5_b_im[j],
                                s5_c_re[j], s5_c_im[j])
            sg_l, s_l, gd_l = _odd_in(xl, pre_g[i], scale, shift, w_in, ws, bs_full, sgu_g[j])
            sg_c, s_c, gd_c = _odd_in(xc, pre_g[i], scale_c, shift_c, w_in, ws, bs_full, sgu_g[j])
            y_l, y_c = _s5_scan(s_l, s_c, mats)
            wg = glu_w[j].astype(BF16)
            xl = _odd_out(sg_l, y_l, s_l, gd_l, s5_d[j], wg, glu_b[j], w_out, post_g[i], gate, xl)
            if not last:
                xc = _odd_out(sg_c, y_c, s_c, gd_c, s5_d[j], wg, glu_b[j], w_out, post_g[i], gate_c, xc)
    return xl
```

```python
import functools

import numpy as np
import jax
import jax.numpy as jnp
from jax import lax
from jax.experimental import pallas as pl
from jax.experimental.pallas import tpu as pltpu

F32 = jnp.float32
BF16 = jnp.bfloat16

EPS = 1e-6
GRID_W = 64
FNET_WIDTH = 256
FNET_GROUP_DIM = 64
NA_WIDTH = 768
NA_HEAD_DIM = 64
NA_HEADS = 12
NA_ROWS = 8
NA_COLS = 16
SGU_CHUNK = 128
SGU_WIDTH = 512
SGU_GROUPS = 4
S5_WIDTH = 512
S5_GROUP_DIM = 16
S5_GROUPS = 32
S5_STATE = 64
S5_CHUNK = 16
S5_TILE = S5_CHUNK * S5_GROUP_DIM

V7X_VMEM_BYTES = 64 * 1024 * 1024
VMEM_LIMIT = 48 * 1024 * 1024
ROW_TILE = 512


def _cparams(sem):
    return pltpu.CompilerParams(dimension_semantics=sem, vmem_limit_bytes=VMEM_LIMIT)


def _silu(x):
    return x * jax.nn.sigmoid(x)


def _dot(a, b):
    return jnp.dot(a, b, preferred_element_type=F32)


def _dot_nt(a, b):
    return lax.dot_general(a, b, (((1,), (1,)), ((), ())), preferred_element_type=F32)


def _ada_kernel(c_ref, w_ref, b_ref, o_ref):
    c = c_ref[...]
    o_ref[0] = jnp.dot(_silu(c), w_ref[0], preferred_element_type=F32,
                       precision=lax.Precision.HIGHEST) + b_ref[0]


def _ada_mod(cond, w_ada, b_ada):
    depth, d, n = w_ada.shape
    rows = cond.shape[0]
    tn = 1024
    return pl.pallas_call(
        _ada_kernel,
        grid=(depth, n // tn),
        in_specs=[pl.BlockSpec((rows, d), lambda i, j: (0, 0)),
                  pl.BlockSpec((1, d, tn), lambda i, j: (i, 0, j)),
                  pl.BlockSpec((1, 1, tn), lambda i, j: (i, 0, j))],
        out_specs=pl.BlockSpec((1, rows, tn), lambda i, j: (i, 0, j)),
        out_shape=jax.ShapeDtypeStruct((depth, rows, n), F32),
        compiler_params=_cparams(("parallel", "parallel")),
        name="ada_mod",
    )(cond, w_ada, b_ada.reshape(depth, 1, n))


def _norm_mod(x, g, scale, shift):
    y = x * lax.rsqrt(jnp.mean(x * x, axis=-1, keepdims=True) + EPS)
    return (y * g) * (1.0 + scale) + shift


def _post(y, post_g, gate, x):
    yn = y * lax.rsqrt(jnp.mean(y * y, axis=-1, keepdims=True) + EPS)
    return x + gate * (yn * post_g)


def _vec_spec(d):
    return pl.BlockSpec((1, 1, d), lambda bi, i: (bi, 0, 0))


def _const_spec(a):
    return pl.BlockSpec(a.shape, lambda bi, i: (0,) * a.ndim)


def _tok_spec(tm, width):
    return pl.BlockSpec((1, tm, width), lambda bi, i: (bi, i, 0))


def _even_in_body(x, g_ref, sc, sh, w_ref, bd_ref, z_scr, outs):
    zc_ref, zs_ref, ga_ref, q_ref, k_ref, v_ref, gb_ref = outs
    h = _norm_mod(x, g_ref[...], sc, sh).astype(BF16)
    fa = _dot(h, w_ref[:, 0:FNET_WIDTH]).astype(BF16)
    z = _dot(fa, bd_ref[...])
    planes = z.shape[1] // 128
    for p in range(planes):
        z_scr[p] = z[:, p * 128:(p + 1) * 128]
    half = z.shape[0] // 2
    for par in range(2):
        for p in range(planes):
            piece = z_scr[p, pl.ds(par, half, stride=2), :].astype(BF16)
            ref, q0 = (zc_ref, p) if p < planes // 2 else (zs_ref, p - planes // 2)
            ref[par, :, q0 * 128:(q0 + 1) * 128] = piece
    o = FNET_WIDTH
    ga_ref[0] = _dot(h, w_ref[:, o:o + FNET_WIDTH]).astype(BF16)
    o += FNET_WIDTH
    q_ref[0] = (_dot(h, w_ref[:, o:o + NA_WIDTH]) * (NA_HEAD_DIM ** -0.5)).astype(BF16)
    o += NA_WIDTH
    k_ref[0] = _dot(h, w_ref[:, o:o + NA_WIDTH]).astype(BF16)
    o += NA_WIDTH
    v_ref[0] = _dot(h, w_ref[:, o:o + NA_WIDTH]).astype(BF16)
    o += NA_WIDTH
    gb_ref[0] = _dot(h, w_ref[:, o:o + NA_WIDTH]).astype(BF16)


def _even_in_io(b, l, d, tm, pre_g, scale, shift, w, bd):
    zspec = pl.BlockSpec((2, tm // 2, FNET_WIDTH), lambda bi, i: (0, i, bi))
    zsd = jax.ShapeDtypeStruct((2, l // 2, b * FNET_WIDTH), BF16)
    sd = lambda width: jax.ShapeDtypeStruct((b, l, width), BF16)
    g2 = pre_g.reshape(1, d)
    widths = (FNET_WIDTH, NA_WIDTH, NA_WIDTH, NA_WIDTH, NA_WIDTH)
    return ([g2, scale, shift, w, bd],
            [_const_spec(g2), _vec_spec(d), _vec_spec(d), _const_spec(w), _const_spec(bd)],
            [zspec, zspec] + [_tok_spec(tm, wd) for wd in widths],
            [zsd, zsd] + [sd(wd) for wd in widths],
            [pltpu.VMEM((2 * FNET_WIDTH // 128, tm, 128), F32)])


def _even_out_body(a_ref, n_ref, w_ref, pg_ref, gate, x):
    y = _dot(a_ref[0], w_ref[:FNET_WIDTH, :]) + _dot(n_ref[0], w_ref[FNET_WIDTH:, :])
    return _post(y, pg_ref[...], gate, x)


def _even_out_io(d, tm, a, n, w, post_g, gate):
    pg2 = post_g.reshape(1, d)
    return ([a, n, w, pg2, gate],
            [_tok_spec(tm, FNET_WIDTH), _tok_spec(tm, NA_WIDTH), _const_spec(w), _const_spec(pg2), _vec_spec(d)])


S5_SUPER = 8
N_SUPER = S5_GROUPS // S5_SUPER
SUPER_LANES = S5_SUPER * S5_GROUP_DIM


def _super_spec(tm):
    return pl.BlockSpec((N_SUPER, tm // S5_CHUNK, 1, S5_CHUNK, SUPER_LANES), lambda bi, i: (0, i, bi, 0, 0))


def _super_shape(b, l):
    return jax.ShapeDtypeStruct((N_SUPER, l // S5_CHUNK, b, S5_CHUNK, SUPER_LANES), BF16)


def _odd_in_body(x, g_ref, sc, sh, w_ref, ws_ref, bs_ref, sg_g_ref, outs):
    sg_ref, s_ref, gd_ref = outs
    tm = x.shape[0]
    h = _norm_mod(x, g_ref[...], sc, sh).astype(BF16)
    wd = SGU_WIDTH
    u = _dot(h, w_ref[:, 0:wd])
    v = _dot(h, w_ref[:, wd:2 * wd])
    gc = _dot(h, w_ref[:, 2 * wd:3 * wd])
    s = _dot(h, w_ref[:, 3 * wd:3 * wd + S5_WIDTH]).astype(BF16)
    for j in range(N_SUPER):
        sj = s[:, j * SUPER_LANES:(j + 1) * SUPER_LANES]
        s_ref[j, :, 0] = sj.reshape(tm // S5_CHUNK, S5_CHUNK, SUPER_LANES)
    gd_ref[0] = _dot(h, w_ref[:, 3 * wd + S5_WIDTH:]).astype(BF16)
    vc = v - jnp.mean(v, axis=-1, keepdims=True)
    vn = (vc * lax.rsqrt(jnp.mean(vc * vc, axis=-1, keepdims=True) + EPS) * sg_g_ref[...]).astype(BF16)
    gate = u * _silu(gc)
    gw = SGU_WIDTH // SGU_GROUPS
    for j in range(tm // SGU_CHUNK):
        rs = slice(j * SGU_CHUNK, (j + 1) * SGU_CHUNK)
        for g in range(SGU_GROUPS):
            cs = slice(g * gw, (g + 1) * gw)
            mixed = _dot(ws_ref[g], vn[rs, cs]) + bs_ref[:, cs]
            sg_ref[0, rs, cs] = (gate[rs, cs] * mixed).astype(BF16)


def _odd_in_io(b, l, d, tm, pre_g, scale, shift, w, ws, bs_full, sgu_g):
    g2 = pre_g.reshape(1, d)
    sg2 = sgu_g.reshape(1, SGU_WIDTH)
    sd = jax.ShapeDtypeStruct((b, l, SGU_WIDTH), BF16)
    return ([g2, scale, shift, w, ws, bs_full, sg2],
            [_const_spec(g2), _vec_spec(d), _vec_spec(d), _const_spec(w), _const_spec(ws), _const_spec(bs_full),
             _const_spec(sg2)],
            [_tok_spec(tm, SGU_WIDTH), _super_spec(tm), _tok_spec(tm, S5_WIDTH)],
            [sd, _super_shape(b, l), sd],
            [])


def _gelu_tanh(x):
    return 0.5 * x * (1.0 + jnp.tanh(float(np.sqrt(2.0 / np.pi)) * (x + 0.044715 * (x * x * x))))


def _odd_out_body(sg_ref, y_ref, s_ref, gd_ref, dsk_ref, wg_ref, bg_ref, w_ref, pg_ref, gate, x):
    tm = x.shape[0]
    tokens = lambda ref: jnp.concatenate([ref[j, :, 0].reshape(tm, SUPER_LANES) for j in range(N_SUPER)], axis=-1)
    ys = tokens(y_ref).astype(F32)
    s = tokens(s_ref).astype(F32)
    z = _gelu_tanh(ys + dsk_ref[...] * s)
    zz = _dot(z.astype(BF16), wg_ref[...]) + bg_ref[...]
    gd = gd_ref[0].astype(F32)
    ss = zz[:, :S5_WIDTH] * jax.nn.sigmoid(zz[:, S5_WIDTH:]) * _silu(gd)
    y = _dot(sg_ref[0], w_ref[:SGU_WIDTH, :]) + _dot(ss.astype(BF16), w_ref[SGU_WIDTH:, :])
    return _post(y, pg_ref[...], gate, x)


def _odd_out_io(d, tm, sg, y, s, gd, d_skip, wg, bg, w, post_g, gate):
    dsk = d_skip.reshape(1, S5_WIDTH).astype(F32)
    bg2 = bg.reshape(1, 2 * S5_WIDTH).astype(F32)
    pg2 = post_g.reshape(1, d)
    return ([sg, y, s, gd, dsk, wg, bg2, w, pg2, gate],
            [_tok_spec(tm, SGU_WIDTH), _super_spec(tm), _super_spec(tm), _tok_spec(tm, S5_WIDTH),
             _const_spec(dsk), _const_spec(wg), _const_spec(bg2), _const_spec(w), _const_spec(pg2), _vec_spec(d)])


_OUT_BODY = {"even": (_even_out_body, 5), "odd": (_odd_out_body, 10)}
_IN_BODY = {"even": (_even_in_body, 5, 7), "odd": (_odd_in_body, 7, 3)}


def _proj_kernel(*refs, out_kind, in_kind):
    pos = 0
    if out_kind is not None:
        body, n_ops = _OUT_BODY[out_kind]
        ops = refs[pos:pos + n_ops]
        pos += n_ops
    x_ref = refs[pos]
    pos += 1
    if in_kind is not None:
        in_body, n_in, n_out = _IN_BODY[in_kind]
        in_ops = refs[pos:pos + n_in]
        pos += n_in
    x = x_ref[0]
    if out_kind is not None:
        x = body(*ops[:-1], ops[-1][0], x)
        refs[pos][0] = x
        pos += 1
    if in_kind is not None:
        outs = refs[pos:pos + n_out]
        scratch = refs[pos + n_out:]
        g_ref, sc_ref, sh_ref = in_ops[:3]
        in_body(x, g_ref, sc_ref[0], sh_ref[0], *in_ops[3:], *scratch, outs)


def _projection(x, out_part, in_part, name):
    b, l, d = x.shape
    tm = min(ROW_TILE, l)
    operands, in_specs, out_specs, out_shapes, scratch = [], [], [], [], []
    if out_part is not None:
        operands += out_part[1]
        in_specs += out_part[2]
    operands.append(x)
    in_specs.append(_tok_spec(tm, d))
    if in_part is not None:
        operands += in_part[1]
        in_specs += in_part[2]
    if out_part is not None:
        out_specs.append(_tok_spec(tm, d))
        out_shapes.append(jax.ShapeDtypeStruct((b, l, d), F32))
    if in_part is not None:
        out_specs += in_part[3]
        out_shapes += in_part[4]
        scratch = in_part[5]
    kern = functools.partial(_proj_kernel, out_kind=out_part and out_part[0], in_kind=in_part and in_part[0])
    return pl.pallas_call(
        kern,
        grid=(b, l // tm),
        in_specs=in_specs,
        out_specs=out_specs,
        out_shape=out_shapes,
        scratch_shapes=scratch,
        compiler_params=_cparams(("parallel", "parallel")),
        name=name,
    )(*operands)


def _dft_kernel(ce_ref, se_ref, co_ref, so_ref, zc_ref, zs_ref, ga_ref, o_ref, acc_e, acc_o, *, nb, norm):
    kk = pl.program_id(1)

    @pl.when(kk == 0)
    def _():
        acc_e[...] = jnp.zeros_like(acc_e)
        acc_o[...] = jnp.zeros_like(acc_o)

    acc_e[...] += _dot(ce_ref[...], zc_ref[0]) + _dot(se_ref[...], zs_ref[0])
    acc_o[...] += _dot(co_ref[...], zc_ref[1]) + _dot(so_ref[...], zs_ref[1])

    @pl.when(kk == pl.num_programs(1) - 1)
    def _():
        for bi in range(nb):
            cs = slice(bi * FNET_WIDTH, (bi + 1) * FNET_WIDTH)
            e, o = acc_e[:, cs], acc_o[:, cs]
            for half, y in enumerate((e + o, e - o)):
                g = ga_ref[bi, half].astype(F32)
                o_ref[bi, half] = (y * norm * _silu(g)).astype(BF16)


def _dft_mix(cm, sm, zc, zs, ga):
    b, l, _ = ga.shape
    lh = l // 2
    tm = min(512, lh)
    tk = min(256, lh)
    nk = lh // tk
    nc = b * FNET_WIDTH
    kern = functools.partial(_dft_kernel, nb=b, norm=float((l * FNET_GROUP_DIM) ** -0.5))
    mat_e = pl.BlockSpec((tm, tk), lambda i, k: (i, k))
    mat_o = pl.BlockSpec((tm, tk), lambda i, k: (i, nk + k))
    zspec = pl.BlockSpec((2, tk, nc), lambda i, k: (0, k, 0))
    halves = pl.BlockSpec((b, 2, tm, FNET_WIDTH), lambda i, k: (0, 0, i, 0))
    out = pl.pallas_call(
        kern,
        grid=(lh // tm, nk),
        in_specs=[mat_e, mat_e, mat_o, mat_o, zspec, zspec, halves],
        out_specs=halves,
        out_shape=jax.ShapeDtypeStruct((b, 2, lh, FNET_WIDTH), BF16),
        scratch_shapes=[pltpu.VMEM((tm, nc), F32), pltpu.VMEM((tm, nc), F32)],
        compiler_params=_cparams(("parallel", "arbitrary")),
        name="dft_mix",
    )(cm, sm, cm, sm, zc, zs, ga.reshape(b, 2, lh, FNET_WIDTH))
    return out.reshape(b, l, FNET_WIDTH)


def _dft_gen_kernel(cb_ref, sb_ref, ca_ref, sa_ref, c_ref, s_ref):
    ca, sa = ca_ref[0], sa_ref[0]
    cb, sb = cb_ref[...], sb_ref[...]
    c_ref[...] = (ca * cb - sa * sb).astype(BF16)
    s_ref[...] = (-(sa * cb + ca * sb)).astype(BF16)


def _dft_mats(l):
    lh = l // 2
    tr = min(256, lh)
    kh = jnp.arange(lh, dtype=jnp.int32)
    k = jnp.concatenate([2 * kh, 2 * kh + 1])
    w = 2.0 * np.pi / l
    ang_b = ((jnp.arange(tr, dtype=jnp.int32)[:, None] * k[None, :]) % l).astype(F32) * w
    ang_a = ((jnp.arange(lh // tr, dtype=jnp.int32)[:, None] * tr * k[None, :]) % l).astype(F32) * w
    ca, sa = jnp.cos(ang_a)[:, None, :], jnp.sin(ang_a)[:, None, :]
    base = pl.BlockSpec((tr, l), lambda i: (0, 0))
    rowv = pl.BlockSpec((1, 1, l), lambda i: (i, 0, 0))
    out = pl.BlockSpec((tr, l), lambda i: (i, 0))
    sd = jax.ShapeDtypeStruct((lh, l), BF16)
    return pl.pallas_call(
        _dft_gen_kernel,
        grid=(lh // tr,),
        in_specs=[base, base, rowv, rowv],
        out_specs=[out, out],
        out_shape=[sd, sd],
        compiler_params=_cparams(("parallel",)),
        name="dft_gen",
    )(jnp.cos(ang_b), jnp.sin(ang_b), ca, sa)


def _group_dft_mat():
    n = FNET_GROUP_DIM
    j = np.arange(n)
    ang = 2.0 * np.pi * ((j[:, None] * j[None, :]) % n) / n
    eye = np.eye(FNET_WIDTH // n)
    mat = np.concatenate([np.kron(eye, np.cos(ang)), np.kron(eye, np.sin(ang))], axis=1)
    return jnp.asarray(mat, F32).astype(BF16)


HEADS_PER_TILE = 4
HEAD_TILE = HEADS_PER_TILE * NA_HEAD_DIM
HEAD_TILES = NA_WIDTH // HEAD_TILE


NA_ROWS_PER_STEP = 8


def _na_kernel(q_ref, k_ref, v_ref, kc_ref, vc_ref, gb_ref, bias_ref, o_ref, *, rows, kh, rps):
    nloc = kh * GRID_W
    row_head = lax.broadcasted_iota(jnp.int32, (HEAD_TILE, HEAD_TILE), 0) // GRID_W
    lane_head = lax.broadcasted_iota(jnp.int32, (HEAD_TILE, HEAD_TILE), 1) // NA_HEAD_DIM
    own = row_head == lane_head
    out_head = lax.broadcasted_iota(jnp.int32, (GRID_W, HEAD_TILE), 1) // NA_HEAD_DIM
    for rr in range(rps):
        r = pl.program_id(1) * rps + rr
        r0 = jnp.clip(r - kh // 2, 0, rows - kh)
        start = pl.multiple_of(r0 * GRID_W, GRID_W)
        qs = slice(rr * GRID_W, (rr + 1) * GRID_W)
        for t in range(HEAD_TILES):
            cs = slice(t * HEAD_TILE, (t + 1) * HEAD_TILE)
            q4 = q_ref[0, qs, cs]
            qm = jnp.where(own, jnp.concatenate([q4] * HEADS_PER_TILE, axis=0), 0.0).astype(BF16)
            kl = k_ref[0, pl.ds(start, nloc), cs]
            vl = v_ref[0, pl.ds(start, nloc), cs]
            s_loc = _dot_nt(qm, kl) + bias_ref[r0 - r + (NA_ROWS - 1), t]
            s_ctx = _dot_nt(qm, kc_ref[0, :, cs])
            m = jnp.maximum(jnp.max(s_loc, axis=-1, keepdims=True), jnp.max(s_ctx, axis=-1, keepdims=True))
            p_loc = jnp.exp(s_loc - m)
            p_ctx = jnp.exp(s_ctx - m)
            den = jnp.sum(p_loc, axis=-1, keepdims=True) + jnp.sum(p_ctx, axis=-1, keepdims=True)
            o = (_dot(p_loc.astype(BF16), vl) + _dot(p_ctx.astype(BF16), vc_ref[0, :, cs])) / den
            o4 = o[0:GRID_W]
            for h in range(1, HEADS_PER_TILE):
                o4 = jnp.where(out_head == h, o[h * GRID_W:(h + 1) * GRID_W], o4)
            g = gb_ref[0, qs, cs].astype(F32)
            o_ref[0, qs, cs] = (o4 * _silu(g)).astype(BF16)


def _na_attention(q, k, v, kc, vc, gb, bias):
    b, l, w = q.shape
    lc = kc.shape[1]
    rows = l // GRID_W
    kh = min(NA_ROWS, rows)
    rps = NA_ROWS_PER_STEP

    once = pl.Buffered(1)
    row = pl.BlockSpec((1, rps * GRID_W, w), lambda bi, i: (bi, i, 0))
    full = lambda n: pl.BlockSpec((1, n, w), lambda bi, i: (bi, 0, 0), pipeline_mode=once)
    table = pl.BlockSpec(bias.shape, lambda bi, i: (0, 0, 0, 0), pipeline_mode=once)
    kern = functools.partial(_na_kernel, rows=rows, kh=kh, rps=rps)
    return pl.pallas_call(
        kern,
        grid=(b, rows // rps),
        in_specs=[row, full(l), full(l), full(lc), full(lc), row, table],
        out_specs=row,
        out_shape=jax.ShapeDtypeStruct((b, l, w), BF16),
        compiler_params=_cparams(("parallel", "arbitrary")),
        name="na_attention",
    )(q, k, v, kc, vc, gb, bias)


def _na_bias(rpb, kh):
    cq = np.arange(GRID_W)
    ck = np.arange(GRID_W)
    c0 = np.clip(cq - NA_COLS // 2, 0, GRID_W - NA_COLS)
    col_in = (ck[None, :] >= c0[:, None]) & (ck[None, :] < c0[:, None] + NA_COLS)
    dc = np.clip(ck[None, :] - cq[:, None] + (NA_COLS - 1), 0, 2 * NA_COLS - 2)
    onehot = (dc.reshape(1, -1) == np.arange(2 * NA_COLS - 1)[:, None]).astype(np.float32)
    t = jnp.einsum('hrc,cx->hrx', rpb.astype(F32), jnp.asarray(onehot), precision=lax.Precision.HIGHEST)
    t = t.reshape(NA_HEADS, 2 * NA_ROWS - 1, GRID_W, GRID_W)
    t = jnp.where(col_in[None, None], t, -1e30)
    var = jnp.stack([t[:, s:s + kh] for s in range(NA_ROWS)])
    var = var.transpose(0, 1, 3, 2, 4)
    return var.reshape(NA_ROWS, HEAD_TILES, HEAD_TILE, kh * GRID_W)


def _ctx_attn_kernel(q_ref, k_ref, v_ref, gb_ref, o_ref, acc_ref):
    for h in range(NA_HEADS):
        lo, hi = h * NA_HEAD_DIM, (h + 1) * NA_HEAD_DIM
        s = _dot_nt(q_ref[0, :, lo:hi], k_ref[0, :, lo:hi])
        m = jnp.max(s, axis=-1, keepdims=True)
        p = jnp.exp(s - m)
        den = jnp.sum(p, axis=-1, keepdims=True)
        acc_ref[:, lo:hi] = _dot(p.astype(BF16), v_ref[0, :, lo:hi]) / den
    g = gb_ref[0].astype(F32)
    o_ref[0] = (acc_ref[...] * _silu(g)).astype(BF16)


def _ctx_attention(q, k, v, gb):
    b, lc, w = q.shape
    spec = pl.BlockSpec((1, lc, w), lambda bi: (bi, 0, 0))
    return pl.pallas_call(
        _ctx_attn_kernel,
        grid=(b,),
        in_specs=[spec, spec, spec, spec],
        out_specs=spec,
        out_shape=jax.ShapeDtypeStruct((b, lc, w), BF16),
        scratch_shapes=[pltpu.VMEM((lc, w), F32)],
        compiler_params=_cparams(("parallel",)),
        name="ctx_attention",
    )(q, k, v, gb)


SUPER_K = S5_CHUNK * SUPER_LANES
SG_ROW_TILE = 512
REC_LANES = 1024


def _summary_kernel(u_ref, p_ref, o_ref, w_scr):
    @pl.when(pl.program_id(1) == 0)
    def _():
        row_group = (lax.broadcasted_iota(jnp.int32, (SUPER_K, S5_TILE), 0) % SUPER_LANES) // S5_GROUP_DIM
        p = p_ref[0]
        for h in range(S5_SUPER):
            w_scr[:, h * S5_TILE:(h + 1) * S5_TILE] = jnp.where(row_group == h, p, 0.0).astype(BF16)

    o_ref[0] = _dot(u_ref[0], w_scr[...])


def _s5_summary(u, p_flat, name):
    _, r, k = u.shape
    tr = min(SG_ROW_TILE, r)
    return pl.pallas_call(
        _summary_kernel,
        grid=(N_SUPER, r // tr),
        in_specs=[pl.BlockSpec((1, tr, k), lambda j, ri: (j, ri, 0)),
                  pl.BlockSpec((1, k, S5_TILE), lambda j, ri: (j, 0, 0))],
        out_specs=pl.BlockSpec((1, tr, k), lambda j, ri: (j, ri, 0)),
        out_shape=jax.ShapeDtypeStruct((N_SUPER, r, k), F32),
        scratch_shapes=[pltpu.VMEM((k, k), BF16)],
        compiler_params=_cparams(("parallel", "arbitrary")),
        name=name,
    )(u, p_flat)


def _apply_kernel(u_ref, h_ref, lag_ref, q_ref, sel_ref, o_ref, w_scr, q_scr):
    @pl.when(pl.program_id(1) == 0)
    def _():
        for s in range(S5_CHUNK):
            lo = SUPER_LANES * (S5_CHUNK - 1 - s)
            w_scr[s * SUPER_LANES:(s + 1) * SUPER_LANES, :] = lag_ref[0, :, lo:lo + SUPER_K]
        col_group = (lax.broadcasted_iota(jnp.int32, (S5_TILE, SUPER_K), 1) % SUPER_LANES) // S5_GROUP_DIM
        for g in range(S5_SUPER):
            rs = slice(g * S5_TILE, (g + 1) * S5_TILE)
            q_scr[rs, :] = jnp.where(col_group == g, _dot(q_ref[0, rs, :], sel_ref[...]), 0.0).astype(BF16)

    acc = _dot(u_ref[0], w_scr[...]) + _dot(h_ref[0].astype(BF16), q_scr[...])
    o_ref[0] = acc.astype(o_ref.dtype)


def _s5_apply(u, h, lag_blocks, q_flat, name):
    _, r, k = u.shape
    tr = min(SG_ROW_TILE, r)
    rows = pl.BlockSpec((1, tr, k), lambda j, ri: (j, ri, 0))
    ri_, ci_ = np.arange(S5_TILE)[:, None], np.arange(k)[None, :]
    sel = (ri_ // S5_GROUP_DIM == ci_ // SUPER_LANES) & (ri_ % S5_GROUP_DIM == ci_ % S5_GROUP_DIM)
    sel = jnp.asarray(sel, F32).astype(BF16)
    return pl.pallas_call(
        _apply_kernel,
        grid=(N_SUPER, r // tr),
        in_specs=[rows, rows,
                  pl.BlockSpec((1,) + lag_blocks.shape[1:], lambda j, ri: (j, 0, 0)),
                  pl.BlockSpec((1, k, S5_TILE), lambda j, ri: (j, 0, 0)),
                  pl.BlockSpec(sel.shape, lambda j, ri: (0, 0))],
        out_specs=rows,
        out_shape=jax.ShapeDtypeStruct((N_SUPER, r, k), BF16),
        scratch_shapes=[pltpu.VMEM((k, k), BF16), pltpu.VMEM((k, k), BF16)],
        compiler_params=_cparams(("parallel", "arbitrary")),
        name=name,
    )(u, h, lag_blocks, q_flat, sel)


def _rec_kernel(sc_ref, sl_ref, a_ref, hc_ref, hl_ref, *, nb, n_ctx, n_lat):
    half = S5_TILE // 2
    npieces = REC_LANES // S5_TILE
    fwd_lane = lax.broadcasted_iota(jnp.int32, (1, half), 1) < S5_STATE
    coef = [(a_ref[0, :, p * S5_TILE:p * S5_TILE + half], a_ref[0, :, p * S5_TILE + half:(p + 1) * S5_TILE])
            for p in range(npieces)]

    def step(s_ref, h_ref, c, carry, forward):
        rows = pl.ds(pl.multiple_of(c * nb, nb), nb)
        out = []
        for p in range(npieces):
            hre, him = carry[2 * p], carry[2 * p + 1]
            are, aim = coef[p]
            re_sl = slice(p * S5_TILE, p * S5_TILE + half)
            im_sl = slice(p * S5_TILE + half, (p + 1) * S5_TILE)
            if forward:
                h_ref[0, rows, re_sl] = jnp.where(fwd_lane, hre, 0.0)
                h_ref[0, rows, im_sl] = jnp.where(fwd_lane, him, 0.0)
            else:
                h_ref[0, rows, re_sl] = jnp.where(fwd_lane, h_ref[0, rows, re_sl], hre)
                h_ref[0, rows, im_sl] = jnp.where(fwd_lane, h_ref[0, rows, im_sl], him)
            sre, sim = s_ref[0, rows, re_sl], s_ref[0, rows, im_sl]
            out.append(are * hre - aim * him + sre)
            out.append(are * him + aim * hre + sim)
        return tuple(out)

    zero = tuple(jnp.zeros((nb, half), F32) for _ in range(2 * npieces))
    carry = lax.fori_loop(0, n_ctx, lambda c, cr: step(sc_ref, hc_ref, c, cr, True), zero)
    lax.fori_loop(0, n_lat, lambda c, cr: step(sl_ref, hl_ref, c, cr, True), carry)
    carry = lax.fori_loop(0, n_ctx, lambda i, cr: step(sc_ref, hc_ref, n_ctx - 1 - i, cr, False), zero)
    lax.fori_loop(0, n_lat, lambda i, cr: step(sl_ref, hl_ref, n_lat - 1 - i, cr, False), carry)


def _chunk_recurrence(s_c, s_l, a16, nb):
    _, rc, lanes = s_c.shape
    rl = s_l.shape[1]
    spec = lambda r: pl.BlockSpec((1, r, REC_LANES), lambda j, i: (j, 0, i))
    return pl.pallas_call(
        functools.partial(_rec_kernel, nb=nb, n_ctx=rc // nb, n_lat=rl // nb),
        grid=(N_SUPER, lanes // REC_LANES),
        in_specs=[spec(rc), spec(rl), spec(1)],
        out_specs=[spec(rc), spec(rl)],
        out_shape=[jax.ShapeDtypeStruct(s_c.shape, F32), jax.ShapeDtypeStruct(s_l.shape, F32)],
        compiler_params=_cparams(("parallel", "parallel")),
        name="s5_recurrence",
    )(s_c, s_l, a16)


def _cmul(ar, ai, br, bi):
    return ar * br - ai * bi, ar * bi + ai * br


def _s5_matrices(lam_re, lam_im, log_step, b_re, b_im, c_re, c_im):
    t = S5_CHUNK
    hp = lax.Precision.HIGHEST
    taus = jnp.arange(t + 1, dtype=F32)[:, None, None]
    er, ei, wr, wi, a_re, a_im, ks = [], [], [], [], [], [], []
    for d in range(2):
        lr = jnp.minimum(lam_re[d].astype(F32), -1e-4)
        li = lam_im[d].astype(F32)
        dt = jnp.exp(log_step[d].astype(F32))[:, None]
        mag = jnp.exp(lr * dt * taus)
        pr, pi = mag * jnp.cos(li * dt * taus), mag * jnp.sin(li * dt * taus)
        den = lr * lr + li * li
        qr = ((pr[1] - 1.0) * lr + pi[1] * li) / den
        qi = (pi[1] * lr - (pr[1] - 1.0) * li) / den
        bbr, bbi = _cmul(qr[..., None], qi[..., None], b_re[d].astype(F32), b_im[d].astype(F32))
        e_r, e_i = _cmul(pr[..., None], pi[..., None], bbr[None], bbi[None])
        cr, ci = c_re[d].astype(F32), c_im[d].astype(F32)
        w_r, w_i = _cmul(cr[None], ci[None], pr[:, :, None, :], pi[:, :, None, :])
        k = (jnp.einsum('gmp,tgpn->tgmn', cr, e_r[:t], precision=hp)
             - jnp.einsum('gmp,tgpn->tgmn', ci, e_i[:t], precision=hp))
        er.append(e_r); ei.append(e_i); wr.append(w_r); wi.append(w_i); ks.append(k)
        a_re.append(pr[t]); a_im.append(pi[t])
    rows = lambda e: e.transpose(1, 0, 3, 2)
    p_mat = jnp.concatenate([rows(er[0][:t][::-1]), rows(er[1][:t]),
                             rows(ei[0][:t][::-1]), rows(ei[1][:t])], axis=-1)
    cols = lambda w: w.transpose(1, 3, 0, 2)
    q_mat = jnp.concatenate([cols(wr[0][1:]), cols(wr[1][1:][::-1]),
                             -cols(wi[0][1:]), -cols(wi[1][1:][::-1])], axis=1)
    a16 = jnp.concatenate([a_re[0], a_re[1], a_im[0], a_im[1]], axis=-1)

    gd, sl = S5_GROUP_DIM, SUPER_LANES

    def spread(a, nblk):
        r = jnp.arange(nblk * gd)[:, None]
        c = jnp.arange(nblk * sl)[None, :]
        sel = ((r // gd == c // sl) & (r % gd == c % gd)).astype(BF16)
        return jnp.einsum('jrk,kc->jrc', a.astype(BF16), sel, preferred_element_type=F32).astype(BF16)

    def same_group(row_group, col_group):
        return (row_group[:, None] == col_group[None, :]).astype(BF16)[None]

    nlag = 2 * t - 1
    kf = ks[0].transpose(0, 1, 3, 2)
    kb = ks[1].transpose(0, 1, 3, 2)
    klag = jnp.concatenate([kb[1:][::-1], (kf[0] + kb[0])[None], kf[1:]], axis=0)
    k_flat = klag.transpose(1, 2, 0, 3).reshape(N_SUPER, sl, nlag * gd)
    lag_blocks = spread(k_flat, nlag) * same_group(jnp.arange(sl) // gd, (jnp.arange(nlag * sl) % sl) // gd)
    p_flat = p_mat.reshape(N_SUPER, S5_SUPER, t, gd, S5_TILE).transpose(0, 2, 1, 3, 4).reshape(N_SUPER, SUPER_K, S5_TILE)
    q_flat = q_mat.reshape(N_SUPER, S5_SUPER * S5_TILE, S5_TILE)
    return lag_blocks, p_flat.astype(BF16), q_flat.astype(BF16), a16.reshape(N_SUPER, 1, SUPER_K)


def _s5_scan(s_l, s_c, mats):
    lag_blocks, p_flat, q_flat, a16 = mats
    b = s_l.shape[2]
    u_l = s_l.reshape(N_SUPER, -1, SUPER_K)
    u_c = s_c.reshape(N_SUPER, -1, SUPER_K)
    sum_l = _s5_summary(u_l, p_flat, "s5_summary")
    sum_c = _s5_summary(u_c, p_flat, "s5_summary_ctx")
    h_c, h_l = _chunk_recurrence(sum_c, sum_l, a16, b)
    y_l = _s5_apply(u_l, h_l, lag_blocks, q_flat, "s5_apply")
    y_c = _s5_apply(u_c, h_c, lag_blocks, q_flat, "s5_apply_ctx")
    return y_l.reshape(s_l.shape), y_c.reshape(s_c.shape)


def kernel(x, c, ctx, c_ctx, w_ada, b_ada, pre_g, post_g, w_in_even, w_out_even, na_rpb,
           w_in_odd, w_out_odd, sgu_w, sgu_b, sgu_g, s5_lam_re, s5_lam_im, s5_log_step,
           s5_b_re, s5_b_im, s5_c_re, s5_c_im, s5_d, glu_w, glu_b):
    b, l, d = x.shape
    lc = ctx.shape[1]
    depth = w_ada.shape[0]
    rows = l // GRID_W
    kh = min(NA_ROWS, rows)

    n_rows = -(-(b + 1) // 8) * 8
    cond = jnp.zeros((n_rows, d), F32).at[:b].set(c).at[b].set(c_ctx)
    mod = _ada_mod(cond, w_ada, b_ada)

    bd = _group_dft_mat()
    cm_l, sm_l = _dft_mats(l)
    cm_c, sm_c = _dft_mats(lc)

    w_in = [(w_in_even if i % 2 == 0 else w_in_odd)[i // 2].astype(BF16) for i in range(depth)]
    n_odd = w_in_odd.shape[0]
    gw = SGU_WIDTH // SGU_GROUPS
    sgu_ws = [sgu_w[j].astype(BF16) for j in range(n_odd)]
    sgu_bs = [jnp.repeat(sgu_b[j].astype(F32).T, gw, axis=1) for j in range(n_odd)]
    tm_l, tm_c = min(ROW_TILE, l), min(ROW_TILE, lc)

    def modulation(i):
        lat = tuple(mod[i, :b, k * d:(k + 1) * d][:, None, :] for k in range(3))
        ctx_mod = jnp.broadcast_to(mod[i, b][None, None, :], (b, 1, 3 * d))
        return lat, tuple(ctx_mod[..., k * d:(k + 1) * d] for k in range(3))

    def in_part(i, seq, shift, scale):
        tm = min(ROW_TILE, seq)
        if i % 2 == 0:
            return ("even",) + _even_in_io(b, seq, d, tm, pre_g[i], scale, shift, w_in[i], bd)
        j = i // 2
        return ("odd",) + _odd_in_io(b, seq, d, tm, pre_g[i], scale, shift, w_in[i], sgu_ws[j], sgu_bs[j], sgu_g[j])

    (shift, scale, gate), (shift_c, scale_c, gate_c) = modulation(0)
    xl, xc = x, ctx
    res_l = _projection(xl, None, in_part(0, l, shift, scale), "proj_in")
    res_c = _projection(xc, None, in_part(0, lc, shift_c, scale_c), "proj_in_ctx")
    for i in range(depth):
        last = i == depth - 1
        j = i // 2
        if i % 2 == 0:
            w_out = w_out_even[j].astype(BF16)
            zc, zs, ga, q, k_, v, gb = res_l
            zc_c, zs_c, ga_c, q_c, k_c, v_c, gb_c = res_c
            a_l = _dft_mix(cm_l, sm_l, zc, zs, ga)
            n_l = _na_attention(q, k_, v, k_c, v_c, gb, _na_bias(na_rpb[j], kh))
            out_l = ("even",) + _even_out_io(d, tm_l, a_l, n_l, w_out, post_g[i], gate)
            if not last:
                a_c = _dft_mix(cm_c, sm_c, zc_c, zs_c, ga_c)
                n_c = _ctx_attention(q_c, k_c, v_c, gb_c)
                out_c = ("even",) + _even_out_io(d, tm_c, a_c, n_c, w_out, post_g[i], gate_c)
        else:
            w_out = w_out_odd[j].astype(BF16)
            mats = _s5_matrices(s5_lam_re[j], s5_lam_im[j], s5_log_step[j], s5_b_re[j], s5_b_im[j],
                                s5_c_re[j], s5_c_im[j])
            sg_l, s_l, gd_l = res_l
            sg_c, s_c, gd_c = res_c
            y_l, y_c = _s5_scan(s_l, s_c, mats)
            wg = glu_w[j].astype(BF16)
            out_l = ("odd",) + _odd_out_io(d, tm_l, sg_l, y_l, s_l, gd_l, s5_d[j], wg, glu_b[j], w_out, post_g[i], gate)
            if not last:
                out_c = ("odd",) + _odd_out_io(d, tm_c, sg_c, y_c, s_c, gd_c, s5_d[j], wg, glu_b[j], w_out,
                                               post_g[i], gate_c)
        if last:
            (xl,) = _projection(xl, out_l, None, "proj_out")
        else:
            (shift, scale, gate), (shift_c, scale_c, gate_c) = modulation(i + 1)
            xl, *res_l = _projection(xl, out_l, in_part(i + 1, l, shift, scale), "proj_out_in")
            xc, *res_c = _projection(xc, out_c, in_part(i + 1, lc, shift_c, scale_c), "proj_out_in_ctx")
    return xl
```

```python
import functools

import numpy as np
import jax
import jax.numpy as jnp
from jax import lax
from jax.experimental import pallas as pl
from jax.experimental.pallas import tpu as pltpu

F32 = jnp.float32
BF16 = jnp.bfloat16

EPS = 1e-6
GRID_W = 64
FNET_WIDTH = 256
FNET_GROUP_DIM = 64
NA_WIDTH = 768
NA_HEAD_DIM = 64
NA_HEADS = 12
NA_ROWS = 8
NA_COLS = 16
SGU_CHUNK = 128
SGU_WIDTH = 512
SGU_GROUPS = 4
S5_WIDTH = 512
S5_GROUP_DIM = 16
S5_GROUPS = 32
S5_STATE = 64
S5_CHUNK = 16
S5_TILE = S5_CHUNK * S5_GROUP_DIM
LOG2E = float(np.log2(np.e))
QK_SCALE = NA_HEAD_DIM ** -0.5 * LOG2E

V7X_VMEM_BYTES = 64 * 1024 * 1024
VMEM_LIMIT = 48 * 1024 * 1024
ROW_TILE = 512


def _cparams(sem):
    return pltpu.CompilerParams(dimension_semantics=sem, vmem_limit_bytes=VMEM_LIMIT)


def _silu(x):
    return x * jax.nn.sigmoid(x)


def _dot(a, b):
    return jnp.dot(a, b, preferred_element_type=F32)


def _dot_nt(a, b):
    return lax.dot_general(a, b, (((1,), (1,)), ((), ())), preferred_element_type=F32)


def _ada_kernel(c_ref, w_ref, b_ref, o_ref):
    c = c_ref[...]
    o_ref[0] = jnp.dot(_silu(c), w_ref[0], preferred_element_type=F32,
                       precision=lax.Precision.HIGHEST) + b_ref[0]


def _ada_mod(cond, w_ada, b_ada):
    depth, d, n = w_ada.shape
    rows = cond.shape[0]
    tn = 1024
    return pl.pallas_call(
        _ada_kernel,
        grid=(depth, n // tn),
        in_specs=[pl.BlockSpec((rows, d), lambda i, j: (0, 0)),
                  pl.BlockSpec((1, d, tn), lambda i, j: (i, 0, j)),
                  pl.BlockSpec((1, 1, tn), lambda i, j: (i, 0, j))],
        out_specs=pl.BlockSpec((1, rows, tn), lambda i, j: (i, 0, j)),
        out_shape=jax.ShapeDtypeStruct((depth, rows, n), F32),
        compiler_params=_cparams(("parallel", "parallel")),
        name="ada_mod",
    )(cond, w_ada, b_ada.reshape(depth, 1, n))


def _norm_mod(x, g, scale, shift):
    y = x * lax.rsqrt(jnp.mean(x * x, axis=-1, keepdims=True) + EPS)
    return (y * g) * (1.0 + scale) + shift


def _post(y, post_g, gate, x):
    yn = y * lax.rsqrt(jnp.mean(y * y, axis=-1, keepdims=True) + EPS)
    return x + gate * (yn * post_g)


def _vec_spec(d):
    return pl.BlockSpec((1, 1, d), lambda bi, i: (bi, 0, 0))


def _const_spec(a):
    return pl.BlockSpec(a.shape, lambda bi, i: (0,) * a.ndim)


def _tok_spec(tm, width):
    return pl.BlockSpec((1, tm, width), lambda bi, i: (bi, i, 0))


def _even_in_body(x, g_ref, sc, sh, w_ref, bd_ref, z_scr, outs):
    zc_ref, zs_ref, ga_ref, q_ref, k_ref, v_ref, gb_ref = outs
    h = _norm_mod(x, g_ref[...], sc, sh).astype(BF16)
    fa = _dot(h, w_ref[:, 0:FNET_WIDTH]).astype(BF16)
    z = _dot(fa, bd_ref[...])
    planes = z.shape[1] // 128
    for p in range(planes):
        z_scr[p] = z[:, p * 128:(p + 1) * 128]
    half = z.shape[0] // 2
    for par in range(2):
        for p in range(planes):
            piece = z_scr[p, pl.ds(par, half, stride=2), :].astype(BF16)
            ref, q0 = (zc_ref, p) if p < planes // 2 else (zs_ref, p - planes // 2)
            ref[par, :, q0 * 128:(q0 + 1) * 128] = piece
    o = FNET_WIDTH
    ga_ref[0] = _dot(h, w_ref[:, o:o + FNET_WIDTH]).astype(BF16)
    o += FNET_WIDTH
    q_ref[0] = (_dot(h, w_ref[:, o:o + NA_WIDTH]) * QK_SCALE).astype(BF16)
    o += NA_WIDTH
    k_ref[0] = _dot(h, w_ref[:, o:o + NA_WIDTH]).astype(BF16)
    o += NA_WIDTH
    v_ref[0] = _dot(h, w_ref[:, o:o + NA_WIDTH]).astype(BF16)
    o += NA_WIDTH
    gb_ref[0] = _dot(h, w_ref[:, o:o + NA_WIDTH]).astype(BF16)


def _even_in_io(b, l, d, tm, pre_g, scale, shift, w, bd):
    zspec = pl.BlockSpec((2, tm // 2, FNET_WIDTH), lambda bi, i: (0, i, bi))
    zsd = jax.ShapeDtypeStruct((2, l // 2, b * FNET_WIDTH), BF16)
    sd = lambda width: jax.ShapeDtypeStruct((b, l, width), BF16)
    g2 = pre_g.reshape(1, d)
    widths = (FNET_WIDTH, NA_WIDTH, NA_WIDTH, NA_WIDTH, NA_WIDTH)
    return ([g2, scale, shift, w, bd],
            [_const_spec(g2), _vec_spec(d), _vec_spec(d), _const_spec(w), _const_spec(bd)],
            [zspec, zspec] + [_tok_spec(tm, wd) for wd in widths],
            [zsd, zsd] + [sd(wd) for wd in widths],
            [pltpu.VMEM((2 * FNET_WIDTH // 128, tm, 128), F32)])


def _even_out_body(a_ref, n_ref, w_ref, pg_ref, gate, x):
    y = _dot(a_ref[0], w_ref[:FNET_WIDTH, :]) + _dot(n_ref[0], w_ref[FNET_WIDTH:, :])
    return _post(y, pg_ref[...], gate, x)


def _even_out_io(d, tm, a, n, w, post_g, gate):
    pg2 = post_g.reshape(1, d)
    return ([a, n, w, pg2, gate],
            [_tok_spec(tm, FNET_WIDTH), _tok_spec(tm, NA_WIDTH), _const_spec(w), _const_spec(pg2), _vec_spec(d)])


S5_SUPER = 8
N_SUPER = S5_GROUPS // S5_SUPER
SUPER_LANES = S5_SUPER * S5_GROUP_DIM


def _super_spec(tm):
    return pl.BlockSpec((N_SUPER, tm // S5_CHUNK, 1, S5_CHUNK, SUPER_LANES), lambda bi, i: (0, i, bi, 0, 0))


def _super_shape(b, l):
    return jax.ShapeDtypeStruct((N_SUPER, l // S5_CHUNK, b, S5_CHUNK, SUPER_LANES), BF16)


def _odd_in_body(x, g_ref, sc, sh, w_ref, ws_ref, bs_ref, sg_g_ref, outs):
    sg_ref, s_ref, gd_ref = outs
    tm = x.shape[0]
    h = _norm_mod(x, g_ref[...], sc, sh).astype(BF16)
    wd = SGU_WIDTH
    u = _dot(h, w_ref[:, 0:wd])
    v = _dot(h, w_ref[:, wd:2 * wd])
    gc = _dot(h, w_ref[:, 2 * wd:3 * wd])
    s = _dot(h, w_ref[:, 3 * wd:3 * wd + S5_WIDTH]).astype(BF16)
    for j in range(N_SUPER):
        sj = s[:, j * SUPER_LANES:(j + 1) * SUPER_LANES]
        s_ref[j, :, 0] = sj.reshape(tm // S5_CHUNK, S5_CHUNK, SUPER_LANES)
    gd_ref[0] = _dot(h, w_ref[:, 3 * wd + S5_WIDTH:]).astype(BF16)
    vc = v - jnp.mean(v, axis=-1, keepdims=True)
    vn = (vc * lax.rsqrt(jnp.mean(vc * vc, axis=-1, keepdims=True) + EPS) * sg_g_ref[...]).astype(BF16)
    gate = u * _silu(gc)
    gw = SGU_WIDTH // SGU_GROUPS
    for j in range(tm // SGU_CHUNK):
        rs = slice(j * SGU_CHUNK, (j + 1) * SGU_CHUNK)
        for g in range(SGU_GROUPS):
            cs = slice(g * gw, (g + 1) * gw)
            mixed = _dot(ws_ref[g], vn[rs, cs]) + bs_ref[:, cs]
            sg_ref[0, rs, cs] = (gate[rs, cs] * mixed).astype(BF16)


def _odd_in_io(b, l, d, tm, pre_g, scale, shift, w, ws, bs_full, sgu_g):
    g2 = pre_g.reshape(1, d)
    sg2 = sgu_g.reshape(1, SGU_WIDTH)
    sd = jax.ShapeDtypeStruct((b, l, SGU_WIDTH), BF16)
    return ([g2, scale, shift, w, ws, bs_full, sg2],
            [_const_spec(g2), _vec_spec(d), _vec_spec(d), _const_spec(w), _const_spec(ws), _const_spec(bs_full),
             _const_spec(sg2)],
            [_tok_spec(tm, SGU_WIDTH), _super_spec(tm), _tok_spec(tm, S5_WIDTH)],
            [sd, _super_shape(b, l), sd],
            [])


def _gelu_tanh(x):
    return 0.5 * x * (1.0 + jnp.tanh(float(np.sqrt(2.0 / np.pi)) * (x + 0.044715 * (x * x * x))))


def _odd_out_body(sg_ref, y_ref, s_ref, gd_ref, dsk_ref, wg_ref, bg_ref, w_ref, pg_ref, gate, x):
    tm = x.shape[0]
    tokens = lambda ref: jnp.concatenate([ref[j, :, 0].reshape(tm, SUPER_LANES) for j in range(N_SUPER)], axis=-1)
    ys = tokens(y_ref).astype(F32)
    s = tokens(s_ref).astype(F32)
    z = _gelu_tanh(ys + dsk_ref[...] * s)
    zz = _dot(z.astype(BF16), wg_ref[...]) + bg_ref[...]
    gd = gd_ref[0].astype(F32)
    ss = zz[:, :S5_WIDTH] * jax.nn.sigmoid(zz[:, S5_WIDTH:]) * _silu(gd)
    y = _dot(sg_ref[0], w_ref[:SGU_WIDTH, :]) + _dot(ss.astype(BF16), w_ref[SGU_WIDTH:, :])
    return _post(y, pg_ref[...], gate, x)


def _odd_out_io(d, tm, sg, y, s, gd, d_skip, wg, bg, w, post_g, gate):
    dsk = d_skip.reshape(1, S5_WIDTH).astype(F32)
    bg2 = bg.reshape(1, 2 * S5_WIDTH).astype(F32)
    pg2 = post_g.reshape(1, d)
    return ([sg, y, s, gd, dsk, wg, bg2, w, pg2, gate],
            [_tok_spec(tm, SGU_WIDTH), _super_spec(tm), _super_spec(tm), _tok_spec(tm, S5_WIDTH),
             _const_spec(dsk), _const_spec(wg), _const_spec(bg2), _const_spec(w), _const_spec(pg2), _vec_spec(d)])


_OUT_BODY = {"even": (_even_out_body, 5), "odd": (_odd_out_body, 10)}
_IN_BODY = {"even": (_even_in_body, 5, 7), "odd": (_odd_in_body, 7, 3)}


def _proj_kernel(*refs, out_kind, in_kind):
    pos = 0
    if out_kind is not None:
        body, n_ops = _OUT_BODY[out_kind]
        ops = refs[pos:pos + n_ops]
        pos += n_ops
    x_ref = refs[pos]
    pos += 1
    if in_kind is not None:
        in_body, n_in, n_out = _IN_BODY[in_kind]
        in_ops = refs[pos:pos + n_in]
        pos += n_in
    x = x_ref[0]
    if out_kind is not None:
        x = body(*ops[:-1], ops[-1][0], x)
        refs[pos][0] = x
        pos += 1
    if in_kind is not None:
        outs = refs[pos:pos + n_out]
        scratch = refs[pos + n_out:]
        g_ref, sc_ref, sh_ref = in_ops[:3]
        in_body(x, g_ref, sc_ref[0], sh_ref[0], *in_ops[3:], *scratch, outs)


def _projection(x, out_part, in_part, name):
    b, l, d = x.shape
    tm = min(ROW_TILE, l)
    operands, in_specs, out_specs, out_shapes, scratch = [], [], [], [], []
    if out_part is not None:
        operands += out_part[1]
        in_specs += out_part[2]
    operands.append(x)
    in_specs.append(_tok_spec(tm, d))
    if in_part is not None:
        operands += in_part[1]
        in_specs += in_part[2]
    if out_part is not None:
        out_specs.append(_tok_spec(tm, d))
        out_shapes.append(jax.ShapeDtypeStruct((b, l, d), F32))
    if in_part is not None:
        out_specs += in_part[3]
        out_shapes += in_part[4]
        scratch = in_part[5]
    kern = functools.partial(_proj_kernel, out_kind=out_part and out_part[0], in_kind=in_part and in_part[0])
    return pl.pallas_call(
        kern,
        grid=(b, l // tm),
        in_specs=in_specs,
        out_specs=out_specs,
        out_shape=out_shapes,
        scratch_shapes=scratch,
        compiler_params=_cparams(("parallel", "parallel")),
        name=name,
    )(*operands)


def _dft_kernel(ce_ref, se_ref, co_ref, so_ref, zc_ref, zs_ref, ga_ref, o_ref, acc_e, acc_o, *, nb, norm):
    kk = pl.program_id(1)

    @pl.when(kk == 0)
    def _():
        acc_e[...] = jnp.zeros_like(acc_e)
        acc_o[...] = jnp.zeros_like(acc_o)

    acc_e[...] += _dot(ce_ref[...], zc_ref[0]) + _dot(se_ref[...], zs_ref[0])
    acc_o[...] += _dot(co_ref[...], zc_ref[1]) + _dot(so_ref[...], zs_ref[1])

    @pl.when(kk == pl.num_programs(1) - 1)
    def _():
        for bi in range(nb):
            cs = slice(bi * FNET_WIDTH, (bi + 1) * FNET_WIDTH)
            e, o = acc_e[:, cs], acc_o[:, cs]
            for half, y in enumerate((e + o, e - o)):
                g = ga_ref[bi, half].astype(F32)
                o_ref[bi, half] = (y * norm * _silu(g)).astype(BF16)


def _dft_mix(cm, sm, zc, zs, ga):
    b, l, _ = ga.shape
    lh = l // 2
    tm = min(512, lh)
    tk = min(256, lh)
    nk = lh // tk
    nc = b * FNET_WIDTH
    kern = functools.partial(_dft_kernel, nb=b, norm=float((l * FNET_GROUP_DIM) ** -0.5))
    mat_e = pl.BlockSpec((tm, tk), lambda i, k: (i, k))
    mat_o = pl.BlockSpec((tm, tk), lambda i, k: (i, nk + k))
    zspec = pl.BlockSpec((2, tk, nc), lambda i, k: (0, k, 0))
    halves = pl.BlockSpec((b, 2, tm, FNET_WIDTH), lambda i, k: (0, 0, i, 0))
    out = pl.pallas_call(
        kern,
        grid=(lh // tm, nk),
        in_specs=[mat_e, mat_e, mat_o, mat_o, zspec, zspec, halves],
        out_specs=halves,
        out_shape=jax.ShapeDtypeStruct((b, 2, lh, FNET_WIDTH), BF16),
        scratch_shapes=[pltpu.VMEM((tm, nc), F32), pltpu.VMEM((tm, nc), F32)],
        compiler_params=_cparams(("parallel", "arbitrary")),
        name="dft_mix",
    )(cm, sm, cm, sm, zc, zs, ga.reshape(b, 2, lh, FNET_WIDTH))
    return out.reshape(b, l, FNET_WIDTH)


def _dft_gen_kernel(cb_ref, sb_ref, ca_ref, sa_ref, c_ref, s_ref):
    ca, sa = ca_ref[0], sa_ref[0]
    cb, sb = cb_ref[...], sb_ref[...]
    c_ref[...] = (ca * cb - sa * sb).astype(BF16)
    s_ref[...] = (-(sa * cb + ca * sb)).astype(BF16)


def _dft_mats(l):
    lh = l // 2
    tr = min(256, lh)
    kh = jnp.arange(lh, dtype=jnp.int32)
    k = jnp.concatenate([2 * kh, 2 * kh + 1])
    w = 2.0 * np.pi / l
    ang_b = ((jnp.arange(tr, dtype=jnp.int32)[:, None] * k[None, :]) % l).astype(F32) * w
    ang_a = ((jnp.arange(lh // tr, dtype=jnp.int32)[:, None] * tr * k[None, :]) % l).astype(F32) * w
    ca, sa = jnp.cos(ang_a)[:, None, :], jnp.sin(ang_a)[:, None, :]
    base = pl.BlockSpec((tr, l), lambda i: (0, 0))
    rowv = pl.BlockSpec((1, 1, l), lambda i: (i, 0, 0))
    out = pl.BlockSpec((tr, l), lambda i: (i, 0))
    sd = jax.ShapeDtypeStruct((lh, l), BF16)
    return pl.pallas_call(
        _dft_gen_kernel,
        grid=(lh // tr,),
        in_specs=[base, base, rowv, rowv],
        out_specs=[out, out],
        out_shape=[sd, sd],
        compiler_params=_cparams(("parallel",)),
        name="dft_gen",
    )(jnp.cos(ang_b), jnp.sin(ang_b), ca, sa)


def _group_dft_mat():
    n = FNET_GROUP_DIM
    j = np.arange(n)
    ang = 2.0 * np.pi * ((j[:, None] * j[None, :]) % n) / n
    eye = np.eye(FNET_WIDTH // n)
    mat = np.concatenate([np.kron(eye, np.cos(ang)), np.kron(eye, np.sin(ang))], axis=1)
    return jnp.asarray(mat, F32).astype(BF16)


HEADS_PER_TILE = 4
HEAD_TILE = HEADS_PER_TILE * NA_HEAD_DIM
HEAD_TILES = NA_WIDTH // HEAD_TILE


def _head_masks():
    row_head = lax.broadcasted_iota(jnp.int32, (HEAD_TILE, HEAD_TILE), 0) // GRID_W
    lane_head = lax.broadcasted_iota(jnp.int32, (HEAD_TILE, HEAD_TILE), 1) // NA_HEAD_DIM
    return row_head == lane_head, lax.broadcasted_iota(jnp.int32, (GRID_W, HEAD_TILE), 1) // NA_HEAD_DIM


def _stacked_queries(q4, own):
    return jnp.where(own, jnp.concatenate([q4] * HEADS_PER_TILE, axis=0), 0.0).astype(BF16)


def _own_head_blocks(o, out_head):
    o4 = o[0:GRID_W]
    for h in range(1, HEADS_PER_TILE):
        o4 = jnp.where(out_head == h, o[h * GRID_W:(h + 1) * GRID_W], o4)
    return o4


NA_ROWS_PER_STEP = 8


def _na_kernel(q_ref, k_ref, v_ref, kc_ref, vc_ref, gb_ref, bias_ref, o_ref, *, rows, kh, rps):
    nloc = kh * GRID_W
    own, out_head = _head_masks()
    for rr in range(rps):
        r = pl.program_id(1) * rps + rr
        r0 = jnp.clip(r - kh // 2, 0, rows - kh)
        start = pl.multiple_of(r0 * GRID_W, GRID_W)
        qs = slice(rr * GRID_W, (rr + 1) * GRID_W)
        for t in range(HEAD_TILES):
            cs = slice(t * HEAD_TILE, (t + 1) * HEAD_TILE)
            qm = _stacked_queries(q_ref[0, qs, cs], own)
            kl = k_ref[0, pl.ds(start, nloc), cs]
            vl = v_ref[0, pl.ds(start, nloc), cs]
            var = r0 - r + (NA_ROWS - 1)
            bias = jnp.concatenate([bias_ref[t, var + 2 * m_] for m_ in range(kh // 2)], axis=-1)
            s_loc = _dot_nt(qm, kl) + bias
            s_ctx = _dot_nt(qm, kc_ref[0, :, cs])
            m = jnp.maximum(jnp.max(s_loc, axis=-1, keepdims=True), jnp.max(s_ctx, axis=-1, keepdims=True))
            p_loc = jnp.exp2(s_loc - m)
            p_ctx = jnp.exp2(s_ctx - m)
            den = jnp.sum(p_loc, axis=-1, keepdims=True) + jnp.sum(p_ctx, axis=-1, keepdims=True)
            o = (_dot(p_loc.astype(BF16), vl) + _dot(p_ctx.astype(BF16), vc_ref[0, :, cs])) / den
            g = gb_ref[0, qs, cs].astype(F32)
            o_ref[0, qs, cs] = (_own_head_blocks(o, out_head) * _silu(g)).astype(BF16)


def _na_attention(q, k, v, kc, vc, gb, bias):
    b, l, w = q.shape
    lc = kc.shape[1]
    rows = l // GRID_W
    kh = min(NA_ROWS, rows)
    rps = NA_ROWS_PER_STEP

    row = pl.BlockSpec((1, rps * GRID_W, w), lambda bi, i: (bi, i, 0))
    full = lambda n: pl.BlockSpec((1, n, w), lambda bi, i: (bi, 0, 0))
    table = pl.BlockSpec(bias.shape, lambda bi, i: (0, 0, 0, 0), pipeline_mode=pl.Buffered(1))
    kern = functools.partial(_na_kernel, rows=rows, kh=kh, rps=rps)
    return pl.pallas_call(
        kern,
        grid=(b, rows // rps),
        in_specs=[row, full(l), full(l), full(lc), full(lc), row, table],
        out_specs=row,
        out_shape=jax.ShapeDtypeStruct((b, l, w), BF16),
        compiler_params=_cparams(("parallel", "arbitrary")),
        name="na_attention",
    )(q, k, v, kc, vc, gb, bias)


def _na_bias(rpb, kh):
    assert kh % 2 == 0
    cq = np.arange(GRID_W)
    ck = np.arange(GRID_W)
    c0 = np.clip(cq - NA_COLS // 2, 0, GRID_W - NA_COLS)
    col_in = (ck[None, :] >= c0[:, None]) & (ck[None, :] < c0[:, None] + NA_COLS)
    dc = np.clip(ck[None, :] - cq[:, None] + (NA_COLS - 1), 0, 2 * NA_COLS - 2)
    onehot = (dc.reshape(1, -1) == np.arange(2 * NA_COLS - 1)[:, None]).astype(np.float32)
    t = jnp.einsum('hrc,cx->hrx', rpb.astype(F32) * LOG2E, jnp.asarray(onehot), precision=lax.Precision.HIGHEST)
    n_dr = 2 * NA_ROWS - 1
    t = t.reshape(NA_HEADS, n_dr, GRID_W, GRID_W)
    t = jnp.where(col_in[None, None], t, -1e30)
    pairs = jnp.concatenate([t[:, :-1], t[:, 1:]], axis=-1)
    pairs = pairs.reshape(HEAD_TILES, HEADS_PER_TILE, n_dr - 1, GRID_W, 2 * GRID_W).transpose(0, 2, 1, 3, 4)
    return pairs.reshape(HEAD_TILES, n_dr - 1, HEAD_TILE, 2 * GRID_W)


def _ctx_attn_kernel(q_ref, k_ref, v_ref, gb_ref, o_ref):
    own, out_head = _head_masks()
    for t in range(HEAD_TILES):
        cs = slice(t * HEAD_TILE, (t + 1) * HEAD_TILE)
        for q0 in range(0, q_ref.shape[1], GRID_W):
            qs = slice(q0, q0 + GRID_W)
            s = _dot_nt(_stacked_queries(q_ref[0, qs, cs], own), k_ref[0, :, cs])
            p = jnp.exp2(s - jnp.max(s, axis=-1, keepdims=True))
            o = _dot(p.astype(BF16), v_ref[0, :, cs]) / jnp.sum(p, axis=-1, keepdims=True)
            g = gb_ref[0, qs, cs].astype(F32)
            o_ref[0, qs, cs] = (_own_head_blocks(o, out_head) * _silu(g)).astype(BF16)


def _ctx_attention(q, k, v, gb):
    b, lc, w = q.shape
    spec = pl.BlockSpec((1, lc, w), lambda bi: (bi, 0, 0))
    return pl.pallas_call(
        _ctx_attn_kernel,
        grid=(b,),
        in_specs=[spec, spec, spec, spec],
        out_specs=spec,
        out_shape=jax.ShapeDtypeStruct((b, lc, w), BF16),
        compiler_params=_cparams(("parallel",)),
        name="ctx_attention",
    )(q, k, v, gb)


SUPER_K = S5_CHUNK * SUPER_LANES
SG_ROW_TILE = 512
REC_LANES = 1024


def _summary_kernel(u_ref, p_ref, o_ref, w_scr):
    @pl.when(pl.program_id(1) == 0)
    def _():
        row_group = (lax.broadcasted_iota(jnp.int32, (SUPER_K, S5_TILE), 0) % SUPER_LANES) // S5_GROUP_DIM
        p = p_ref[0]
        for h in range(S5_SUPER):
            w_scr[:, h * S5_TILE:(h + 1) * S5_TILE] = jnp.where(row_group == h, p, 0.0).astype(BF16)

    o_ref[0] = _dot(u_ref[0], w_scr[...])


def _s5_summary(u, p_flat, name):
    _, r, k = u.shape
    tr = min(SG_ROW_TILE, r)
    return pl.pallas_call(
        _summary_kernel,
        grid=(N_SUPER, r // tr),
        in_specs=[pl.BlockSpec((1, tr, k), lambda j, ri: (j, ri, 0)),
                  pl.BlockSpec((1, k, S5_TILE), lambda j, ri: (j, 0, 0))],
        out_specs=pl.BlockSpec((1, tr, k), lambda j, ri: (j, ri, 0)),
        out_shape=jax.ShapeDtypeStruct((N_SUPER, r, k), F32),
        scratch_shapes=[pltpu.VMEM((k, k), BF16)],
        compiler_params=_cparams(("parallel", "arbitrary")),
        name=name,
    )(u, p_flat)


def _apply_kernel(u_ref, h_ref, lag_ref, q_ref, sel_ref, o_ref, w_scr, q_scr):
    @pl.when(pl.program_id(1) == 0)
    def _():
        for s in range(S5_CHUNK):
            lo = SUPER_LANES * (S5_CHUNK - 1 - s)
            w_scr[s * SUPER_LANES:(s + 1) * SUPER_LANES, :] = lag_ref[0, :, lo:lo + SUPER_K]
        col_group = (lax.broadcasted_iota(jnp.int32, (S5_TILE, SUPER_K), 1) % SUPER_LANES) // S5_GROUP_DIM
        for g in range(S5_SUPER):
            rs = slice(g * S5_TILE, (g + 1) * S5_TILE)
            q_scr[rs, :] = jnp.where(col_group == g, _dot(q_ref[0, rs, :], sel_ref[...]), 0.0).astype(BF16)

    acc = _dot(u_ref[0], w_scr[...]) + _dot(h_ref[0].astype(BF16), q_scr[...])
    o_ref[0] = acc.astype(o_ref.dtype)


def _s5_apply(u, h, lag_blocks, q_flat, name):
    _, r, k = u.shape
    tr = min(SG_ROW_TILE, r)
    rows = pl.BlockSpec((1, tr, k), lambda j, ri: (j, ri, 0))
    ri_, ci_ = np.arange(S5_TILE)[:, None], np.arange(k)[None, :]
    sel = (ri_ // S5_GROUP_DIM == ci_ // SUPER_LANES) & (ri_ % S5_GROUP_DIM == ci_ % S5_GROUP_DIM)
    sel = jnp.asarray(sel, F32).astype(BF16)
    return pl.pallas_call(
        _apply_kernel,
        grid=(N_SUPER, r // tr),
        in_specs=[rows, rows,
                  pl.BlockSpec((1,) + lag_blocks.shape[1:], lambda j, ri: (j, 0, 0)),
                  pl.BlockSpec((1, k, S5_TILE), lambda j, ri: (j, 0, 0)),
                  pl.BlockSpec(sel.shape, lambda j, ri: (0, 0))],
        out_specs=rows,
        out_shape=jax.ShapeDtypeStruct((N_SUPER, r, k), BF16),
        scratch_shapes=[pltpu.VMEM((k, k), BF16), pltpu.VMEM((k, k), BF16)],
        compiler_params=_cparams(("parallel", "arbitrary")),
        name=name,
    )(u, h, lag_blocks, q_flat, sel)


def _rec_kernel(sc_ref, sl_ref, a_ref, hc_ref, hl_ref, *, nb, n_ctx, n_lat):
    half = S5_TILE // 2
    npieces = REC_LANES // S5_TILE
    fwd_lane = lax.broadcasted_iota(jnp.int32, (1, half), 1) < S5_STATE
    coef = [(a_ref[0, :, p * S5_TILE:p * S5_TILE + half], a_ref[0, :, p * S5_TILE + half:(p + 1) * S5_TILE])
            for p in range(npieces)]

    def step(s_ref, h_ref, c, carry, forward):
        rows = pl.ds(pl.multiple_of(c * nb, nb), nb)
        out = []
        for p in range(npieces):
            hre, him = carry[2 * p], carry[2 * p + 1]
            are, aim = coef[p]
            re_sl = slice(p * S5_TILE, p * S5_TILE + half)
            im_sl = slice(p * S5_TILE + half, (p + 1) * S5_TILE)
            if forward:
                h_ref[0, rows, re_sl] = jnp.where(fwd_lane, hre, 0.0)
                h_ref[0, rows, im_sl] = jnp.where(fwd_lane, him, 0.0)
            else:
                h_ref[0, rows, re_sl] = jnp.where(fwd_lane, h_ref[0, rows, re_sl], hre)
                h_ref[0, rows, im_sl] = jnp.where(fwd_lane, h_ref[0, rows, im_sl], him)
            sre, sim = s_ref[0, rows, re_sl], s_ref[0, rows, im_sl]
            out.append(are * hre - aim * him + sre)
            out.append(are * him + aim * hre + sim)
        return tuple(out)

    zero = tuple(jnp.zeros((nb, half), F32) for _ in range(2 * npieces))
    carry = lax.fori_loop(0, n_ctx, lambda c, cr: step(sc_ref, hc_ref, c, cr, True), zero)
    lax.fori_loop(0, n_lat, lambda c, cr: step(sl_ref, hl_ref, c, cr, True), carry)
    carry = lax.fori_loop(0, n_ctx, lambda i, cr: step(sc_ref, hc_ref, n_ctx - 1 - i, cr, False), zero)
    lax.fori_loop(0, n_lat, lambda i, cr: step(sl_ref, hl_ref, n_lat - 1 - i, cr, False), carry)


def _chunk_recurrence(s_c, s_l, a16, nb):
    _, rc, lanes = s_c.shape
    rl = s_l.shape[1]
    spec = lambda r: pl.BlockSpec((1, r, REC_LANES), lambda j, i: (j, 0, i))
    return pl.pallas_call(
        functools.partial(_rec_kernel, nb=nb, n_ctx=rc // nb, n_lat=rl // nb),
        grid=(N_SUPER, lanes // REC_LANES),
        in_specs=[spec(rc), spec(rl), spec(1)],
        out_specs=[spec(rc), spec(rl)],
        out_shape=[jax.ShapeDtypeStruct(s_c.shape, F32), jax.ShapeDtypeStruct(s_l.shape, F32)],
        compiler_params=_cparams(("parallel", "parallel")),
        name="s5_recurrence",
    )(s_c, s_l, a16)


def _cmul(ar, ai, br, bi):
    return ar * br - ai * bi, ar * bi + ai * br


def _s5_matrices(lam_re, lam_im, log_step, b_re, b_im, c_re, c_im):
    t = S5_CHUNK
    hp = lax.Precision.HIGHEST
    taus = jnp.arange(t + 1, dtype=F32)[:, None, None]
    er, ei, wr, wi, a_re, a_im, ks = [], [], [], [], [], [], []
    for d in range(2):
        lr = jnp.minimum(lam_re[d].astype(F32), -1e-4)
        li = lam_im[d].astype(F32)
        dt = jnp.exp(log_step[d].astype(F32))[:, None]
        mag = jnp.exp(lr * dt * taus)
        pr, pi = mag * jnp.cos(li * dt * taus), mag * jnp.sin(li * dt * taus)
        den = lr * lr + li * li
        qr = ((pr[1] - 1.0) * lr + pi[1] * li) / den
        qi = (pi[1] * lr - (pr[1] - 1.0) * li) / den
        bbr, bbi = _cmul(qr[..., None], qi[..., None], b_re[d].astype(F32), b_im[d].astype(F32))
        e_r, e_i = _cmul(pr[..., None], pi[..., None], bbr[None], bbi[None])
        cr, ci = c_re[d].astype(F32), c_im[d].astype(F32)
        w_r, w_i = _cmul(cr[None], ci[None], pr[:, :, None, :], pi[:, :, None, :])
        k = (jnp.einsum('gmp,tgpn->tgmn', cr, e_r[:t], precision=hp)
             - jnp.einsum('gmp,tgpn->tgmn', ci, e_i[:t], precision=hp))
        er.append(e_r); ei.append(e_i); wr.append(w_r); wi.append(w_i); ks.append(k)
        a_re.append(pr[t]); a_im.append(pi[t])
    rows = lambda e: e.transpose(1, 0, 3, 2)
    p_mat = jnp.concatenate([rows(er[0][:t][::-1]), rows(er[1][:t]),
                             rows(ei[0][:t][::-1]), rows(ei[1][:t])], axis=-1)
    cols = lambda w: w.transpose(1, 3, 0, 2)
    q_mat = jnp.concatenate([cols(wr[0][1:]), cols(wr[1][1:][::-1]),
                             -cols(wi[0][1:]), -cols(wi[1][1:][::-1])], axis=1)
    a16 = jnp.concatenate([a_re[0], a_re[1], a_im[0], a_im[1]], axis=-1)

    gd, sl = S5_GROUP_DIM, SUPER_LANES

    def spread(a, nblk):
        r = jnp.arange(nblk * gd)[:, None]
        c = jnp.arange(nblk * sl)[None, :]
        sel = ((r // gd == c // sl) & (r % gd == c % gd)).astype(BF16)
        return jnp.einsum('jrk,kc->jrc', a.astype(BF16), sel, preferred_element_type=F32).astype(BF16)

    def same_group(row_group, col_group):
        return (row_group[:, None] == col_group[None, :]).astype(BF16)[None]

    nlag = 2 * t - 1
    kf = ks[0].transpose(0, 1, 3, 2)
    kb = ks[1].transpose(0, 1, 3, 2)
    klag = jnp.concatenate([kb[1:][::-1], (kf[0] + kb[0])[None], kf[1:]], axis=0)
    k_flat = klag.transpose(1, 2, 0, 3).reshape(N_SUPER, sl, nlag * gd)
    lag_blocks = spread(k_flat, nlag) * same_group(jnp.arange(sl) // gd, (jnp.arange(nlag * sl) % sl) // gd)
    p_flat = p_mat.reshape(N_SUPER, S5_SUPER, t, gd, S5_TILE).transpose(0, 2, 1, 3, 4).reshape(N_SUPER, SUPER_K, S5_TILE)
    q_flat = q_mat.reshape(N_SUPER, S5_SUPER * S5_TILE, S5_TILE)
    return lag_blocks, p_flat.astype(BF16), q_flat.astype(BF16), a16.reshape(N_SUPER, 1, SUPER_K)


def _s5_scan(s_l, s_c, mats):
    lag_blocks, p_flat, q_flat, a16 = mats
    b = s_l.shape[2]
    u_l = s_l.reshape(N_SUPER, -1, SUPER_K)
    u_c = s_c.reshape(N_SUPER, -1, SUPER_K)
    sum_l = _s5_summary(u_l, p_flat, "s5_summary")
    sum_c = _s5_summary(u_c, p_flat, "s5_summary_ctx")
    h_c, h_l = _chunk_recurrence(sum_c, sum_l, a16, b)
    y_l = _s5_apply(u_l, h_l, lag_blocks, q_flat, "s5_apply")
    y_c = _s5_apply(u_c, h_c, lag_blocks, q_flat, "s5_apply_ctx")
    return y_l.reshape(s_l.shape), y_c.reshape(s_c.shape)


def kernel(x, c, ctx, c_ctx, w_ada, b_ada, pre_g, post_g, w_in_even, w_out_even, na_rpb,
           w_in_odd, w_out_odd, sgu_w, sgu_b, sgu_g, s5_lam_re, s5_lam_im, s5_log_step,
           s5_b_re, s5_b_im, s5_c_re, s5_c_im, s5_d, glu_w, glu_b):
    b, l, d = x.shape
    lc = ctx.shape[1]
    depth = w_ada.shape[0]
    rows = l // GRID_W
    kh = min(NA_ROWS, rows)

    n_rows = -(-(b + 1) // 8) * 8
    cond = jnp.zeros((n_rows, d), F32).at[:b].set(c).at[b].set(c_ctx)
    mod = _ada_mod(cond, w_ada, b_ada)

    bd = _group_dft_mat()
    cm_l, sm_l = _dft_mats(l)
    cm_c, sm_c = _dft_mats(lc)

    w_in = [(w_in_even if i % 2 == 0 else w_in_odd)[i // 2].astype(BF16) for i in range(depth)]
    n_odd = w_in_odd.shape[0]
    gw = SGU_WIDTH // SGU_GROUPS
    sgu_ws = [sgu_w[j].astype(BF16) for j in range(n_odd)]
    sgu_bs = [jnp.repeat(sgu_b[j].astype(F32).T, gw, axis=1) for j in range(n_odd)]
    tm_l, tm_c = min(ROW_TILE, l), min(ROW_TILE, lc)

    def modulation(i):
        lat = tuple(mod[i, :b, k * d:(k + 1) * d][:, None, :] for k in range(3))
        ctx_mod = jnp.broadcast_to(mod[i, b][None, None, :], (b, 1, 3 * d))
        return lat, tuple(ctx_mod[..., k * d:(k + 1) * d] for k in range(3))

    def in_part(i, seq, shift, scale):
        tm = min(ROW_TILE, seq)
        if i % 2 == 0:
            return ("even",) + _even_in_io(b, seq, d, tm, pre_g[i], scale, shift, w_in[i], bd)
        j = i // 2
        return ("odd",) + _odd_in_io(b, seq, d, tm, pre_g[i], scale, shift, w_in[i], sgu_ws[j], sgu_bs[j], sgu_g[j])

    (shift, scale, gate), (shift_c, scale_c, gate_c) = modulation(0)
    xl, xc = x, ctx
    res_l = _projection(xl, None, in_part(0, l, shift, scale), "proj_in")
    res_c = _projection(xc, None, in_part(0, lc, shift_c, scale_c), "proj_in_ctx")
    for i in range(depth):
        last = i == depth - 1
        j = i // 2
        if i % 2 == 0:
            w_out = w_out_even[j].astype(BF16)
            zc, zs, ga, q, k_, v, gb = res_l
            zc_c, zs_c, ga_c, q_c, k_c, v_c, gb_c = res_c
            a_l = _dft_mix(cm_l, sm_l, zc, zs, ga)
            n_l = _na_attention(q, k_, v, k_c, v_c, gb, _na_bias(na_rpb[j], kh))
            out_l = ("even",) + _even_out_io(d, tm_l, a_l, n_l, w_out, post_g[i], gate)
            if not last:
                a_c = _dft_mix(cm_c, sm_c, zc_c, zs_c, ga_c)
                n_c = _ctx_attention(q_c, k_c, v_c, gb_c)
                out_c = ("even",) + _even_out_io(d, tm_c, a_c, n_c, w_out, post_g[i], gate_c)
        else:
            w_out = w_out_odd[j].astype(BF16)
            mats = _s5_matrices(s5_lam_re[j], s5_lam_im[j], s5_log_step[j], s5_b_re[j], s5_b_im[j],
                                s5_c_re[j], s5_c_im[j])
            sg_l, s_l, gd_l = res_l
            sg_c, s_c, gd_c = res_c
            y_l, y_c = _s5_scan(s_l, s_c, mats)
            wg = glu_w[j].astype(BF16)
            out_l = ("odd",) + _odd_out_io(d, tm_l, sg_l, y_l, s_l, gd_l, s5_d[j], wg, glu_b[j], w_out, post_g[i], gate)
            if not last:
                out_c = ("odd",) + _odd_out_io(d, tm_c, sg_c, y_c, s_c, gd_c, s5_d[j], wg, glu_b[j], w_out,
                                               post_g[i], gate_c)
        if last:
            (xl,) = _projection(xl, out_l, None, "proj_out")
        else:
            (shift, scale, gate), (shift_c, scale_c, gate_c) = modulation(i + 1)
            xl, *res_l = _projection(xl, out_l, in_part(i + 1, l, shift, scale), "proj_out_in")
            xc, *res_c = _projection(xc, out_c, in_part(i + 1, lc, shift_c, scale_c), "proj_out_in_ctx")
    return xl
```

```python
import functools

import numpy as np
import jax
import jax.numpy as jnp
from jax import lax
from jax.experimental import pallas as pl
from jax.experimental.pallas import tpu as pltpu

F32 = jnp.float32
BF16 = jnp.bfloat16

EPS = 1e-6
GRID_W = 64
FNET_WIDTH = 256
FNET_GROUP_DIM = 64
NA_WIDTH = 768
NA_HEAD_DIM = 64
NA_HEADS = 12
NA_ROWS = 8
NA_COLS = 16
SGU_CHUNK = 128
SGU_WIDTH = 512
SGU_GROUPS = 4
S5_WIDTH = 512
S5_GROUP_DIM = 16
S5_GROUPS = 32
S5_STATE = 64
S5_CHUNK = 16
S5_TILE = S5_CHUNK * S5_GROUP_DIM
LOG2E = float(np.log2(np.e))
QK_SCALE = NA_HEAD_DIM ** -0.5 * LOG2E

V7X_VMEM_BYTES = 64 * 1024 * 1024
VMEM_LIMIT = 48 * 1024 * 1024
ROW_TILE = 512


def _cparams(sem):
    return pltpu.CompilerParams(dimension_semantics=sem, vmem_limit_bytes=VMEM_LIMIT)


def _silu(x):
    return x * jax.nn.sigmoid(x)


def _dot(a, b):
    return jnp.dot(a, b, preferred_element_type=F32)


def _dot_nt(a, b):
    return lax.dot_general(a, b, (((1,), (1,)), ((), ())), preferred_element_type=F32)


def _ada_kernel(c_ref, w_ref, b_ref, o_ref):
    c = c_ref[...]
    o_ref[0] = jnp.dot(_silu(c), w_ref[0], preferred_element_type=F32,
                       precision=lax.Precision.HIGHEST) + b_ref[0]


def _ada_mod(cond, w_ada, b_ada):
    depth, d, n = w_ada.shape
    rows = cond.shape[0]
    tn = 1024
    return pl.pallas_call(
        _ada_kernel,
        grid=(depth, n // tn),
        in_specs=[pl.BlockSpec((rows, d), lambda i, j: (0, 0)),
                  pl.BlockSpec((1, d, tn), lambda i, j: (i, 0, j)),
                  pl.BlockSpec((1, 1, tn), lambda i, j: (i, 0, j))],
        out_specs=pl.BlockSpec((1, rows, tn), lambda i, j: (i, 0, j)),
        out_shape=jax.ShapeDtypeStruct((depth, rows, n), F32),
        compiler_params=_cparams(("parallel", "parallel")),
        name="ada_mod",
    )(cond, w_ada, b_ada.reshape(depth, 1, n))


def _norm_mod(x, g, scale, shift):
    y = x * lax.rsqrt(jnp.mean(x * x, axis=-1, keepdims=True) + EPS)
    return (y * g) * (1.0 + scale) + shift


def _post(y, post_g, gate, x):
    yn = y * lax.rsqrt(jnp.mean(y * y, axis=-1, keepdims=True) + EPS)
    return x + gate * (yn * post_g)


def _vec_spec(d):
    return pl.BlockSpec((1, 1, d), lambda bi, i: (bi, 0, 0))


def _const_spec(a):
    return pl.BlockSpec(a.shape, lambda bi, i: (0,) * a.ndim)


def _tok_spec(tm, width):
    return pl.BlockSpec((1, tm, width), lambda bi, i: (bi, i, 0))


def _even_in_body(x, g_ref, sc, sh, w_ref, bd_ref, z_scr, outs):
    zc_ref, zs_ref, ga_ref, q_ref, k_ref, v_ref, gb_ref = outs
    h = _norm_mod(x, g_ref[...], sc, sh).astype(BF16)
    fa = _dot(h, w_ref[:, 0:FNET_WIDTH]).astype(BF16)
    z = _dot(fa, bd_ref[...])
    planes = z.shape[1] // 128
    for p in range(planes):
        z_scr[p] = z[:, p * 128:(p + 1) * 128]
    half = z.shape[0] // 2
    for par in range(2):
        for p in range(planes):
            piece = z_scr[p, pl.ds(par, half, stride=2), :].astype(BF16)
            ref, q0 = (zc_ref, p) if p < planes // 2 else (zs_ref, p - planes // 2)
            ref[par, :, q0 * 128:(q0 + 1) * 128] = piece
    o = FNET_WIDTH
    ga_ref[0] = _dot(h, w_ref[:, o:o + FNET_WIDTH]).astype(BF16)
    o += FNET_WIDTH
    q_ref[0] = (_dot(h, w_ref[:, o:o + NA_WIDTH]) * QK_SCALE).astype(BF16)
    o += NA_WIDTH
    k_ref[0] = _dot(h, w_ref[:, o:o + NA_WIDTH]).astype(BF16)
    o += NA_WIDTH
    v_ref[0] = _dot(h, w_ref[:, o:o + NA_WIDTH]).astype(BF16)
    o += NA_WIDTH
    gb_ref[0] = _dot(h, w_ref[:, o:o + NA_WIDTH]).astype(BF16)


def _even_in_io(b, l, d, tm, pre_g, scale, shift, w, bd):
    zspec = pl.BlockSpec((2, tm // 2, FNET_WIDTH), lambda bi, i: (0, i, bi))
    zsd = jax.ShapeDtypeStruct((2, l // 2, b * FNET_WIDTH), BF16)
    sd = lambda width: jax.ShapeDtypeStruct((b, l, width), BF16)
    g2 = pre_g.reshape(1, d)
    widths = (FNET_WIDTH, NA_WIDTH, NA_WIDTH, NA_WIDTH, NA_WIDTH)
    return ([g2, scale, shift, w, bd],
            [_const_spec(g2), _vec_spec(d), _vec_spec(d), _const_spec(w), _const_spec(bd)],
            [zspec, zspec] + [_tok_spec(tm, wd) for wd in widths],
            [zsd, zsd] + [sd(wd) for wd in widths],
            [pltpu.VMEM((2 * FNET_WIDTH // 128, tm, 128), F32)])


def _even_out_body(a_ref, n_ref, w_ref, pg_ref, gate, x):
    y = _dot(a_ref[0], w_ref[:FNET_WIDTH, :]) + _dot(n_ref[0], w_ref[FNET_WIDTH:, :])
    return _post(y, pg_ref[...], gate, x)


def _even_out_io(d, tm, a, n, w, post_g, gate):
    pg2 = post_g.reshape(1, d)
    return ([a, n, w, pg2, gate],
            [_tok_spec(tm, FNET_WIDTH), _tok_spec(tm, NA_WIDTH), _const_spec(w), _const_spec(pg2), _vec_spec(d)])


S5_SUPER = 8
N_SUPER = S5_GROUPS // S5_SUPER
SUPER_LANES = S5_SUPER * S5_GROUP_DIM


def _super_spec(tm):
    return pl.BlockSpec((N_SUPER, tm // S5_CHUNK, 1, S5_CHUNK, SUPER_LANES), lambda bi, i: (0, i, bi, 0, 0))


def _super_shape(b, l):
    return jax.ShapeDtypeStruct((N_SUPER, l // S5_CHUNK, b, S5_CHUNK, SUPER_LANES), F32)


def _odd_in_body(x, g_ref, sc, sh, w_ref, ws_ref, bs_ref, sg_g_ref, outs):
    sg_ref, s_ref, gd_ref = outs
    tm = x.shape[0]
    h = _norm_mod(x, g_ref[...], sc, sh).astype(BF16)
    wd = SGU_WIDTH
    u = _dot(h, w_ref[:, 0:wd])
    v = _dot(h, w_ref[:, wd:2 * wd])
    gc = _dot(h, w_ref[:, 2 * wd:3 * wd])
    s = _dot(h, w_ref[:, 3 * wd:3 * wd + S5_WIDTH])
    for j in range(N_SUPER):
        sj = s[:, j * SUPER_LANES:(j + 1) * SUPER_LANES]
        s_ref[j, :, 0] = sj.reshape(tm // S5_CHUNK, S5_CHUNK, SUPER_LANES)
    gd_ref[0] = _dot(h, w_ref[:, 3 * wd + S5_WIDTH:]).astype(BF16)
    vc = v - jnp.mean(v, axis=-1, keepdims=True)
    vn = (vc * lax.rsqrt(jnp.mean(vc * vc, axis=-1, keepdims=True) + EPS) * sg_g_ref[...]).astype(BF16)
    gate = u * _silu(gc)
    gw = SGU_WIDTH // SGU_GROUPS
    for j in range(tm // SGU_CHUNK):
        rs = slice(j * SGU_CHUNK, (j + 1) * SGU_CHUNK)
        for g in range(SGU_GROUPS):
            cs = slice(g * gw, (g + 1) * gw)
            mixed = _dot(ws_ref[g], vn[rs, cs]) + bs_ref[:, cs]
            sg_ref[0, rs, cs] = (gate[rs, cs] * mixed).astype(BF16)


def _odd_in_io(b, l, d, tm, pre_g, scale, shift, w, ws, bs_full, sgu_g):
    g2 = pre_g.reshape(1, d)
    sg2 = sgu_g.reshape(1, SGU_WIDTH)
    sd = jax.ShapeDtypeStruct((b, l, SGU_WIDTH), BF16)
    return ([g2, scale, shift, w, ws, bs_full, sg2],
            [_const_spec(g2), _vec_spec(d), _vec_spec(d), _const_spec(w), _const_spec(ws), _const_spec(bs_full),
             _const_spec(sg2)],
            [_tok_spec(tm, SGU_WIDTH), _super_spec(tm), _tok_spec(tm, S5_WIDTH)],
            [sd, _super_shape(b, l), sd],
            [])


def _gelu_tanh(x):
    return 0.5 * x * (1.0 + jnp.tanh(float(np.sqrt(2.0 / np.pi)) * (x + 0.044715 * (x * x * x))))


def _odd_out_body(sg_ref, y_ref, s_ref, gd_ref, dsk_ref, wg_ref, bg_ref, w_ref, pg_ref, gate, x):
    tm = x.shape[0]
    tokens = lambda ref: jnp.concatenate([ref[j, :, 0].reshape(tm, SUPER_LANES) for j in range(N_SUPER)], axis=-1)
    ys = tokens(y_ref)
    s = tokens(s_ref)
    z = _gelu_tanh(ys + dsk_ref[...] * s)
    zz = _dot(z.astype(BF16), wg_ref[...]) + bg_ref[...]
    gd = gd_ref[0].astype(F32)
    ss = zz[:, :S5_WIDTH] * jax.nn.sigmoid(zz[:, S5_WIDTH:]) * _silu(gd)
    y = _dot(sg_ref[0], w_ref[:SGU_WIDTH, :]) + _dot(ss.astype(BF16), w_ref[SGU_WIDTH:, :])
    return _post(y, pg_ref[...], gate, x)


def _odd_out_io(d, tm, sg, y, s, gd, d_skip, wg, bg, w, post_g, gate):
    dsk = d_skip.reshape(1, S5_WIDTH).astype(F32)
    bg2 = bg.reshape(1, 2 * S5_WIDTH).astype(F32)
    pg2 = post_g.reshape(1, d)
    return ([sg, y, s, gd, dsk, wg, bg2, w, pg2, gate],
            [_tok_spec(tm, SGU_WIDTH), _super_spec(tm), _super_spec(tm), _tok_spec(tm, S5_WIDTH),
             _const_spec(dsk), _const_spec(wg), _const_spec(bg2), _const_spec(w), _const_spec(pg2), _vec_spec(d)])


_OUT_BODY = {"even": (_even_out_body, 5), "odd": (_odd_out_body, 10)}
_IN_BODY = {"even": (_even_in_body, 5, 7), "odd": (_odd_in_body, 7, 3)}


def _proj_kernel(*refs, out_kind, in_kind):
    pos = 0
    if out_kind is not None:
        body, n_ops = _OUT_BODY[out_kind]
        ops = refs[pos:pos + n_ops]
        pos += n_ops
    x_ref = refs[pos]
    pos += 1
    if in_kind is not None:
        in_body, n_in, n_out = _IN_BODY[in_kind]
        in_ops = refs[pos:pos + n_in]
        pos += n_in
    x = x_ref[0]
    if out_kind is not None:
        x = body(*ops[:-1], ops[-1][0], x)
        refs[pos][0] = x
        pos += 1
    if in_kind is not None:
        outs = refs[pos:pos + n_out]
        scratch = refs[pos + n_out:]
        g_ref, sc_ref, sh_ref = in_ops[:3]
        in_body(x, g_ref, sc_ref[0], sh_ref[0], *in_ops[3:], *scratch, outs)


def _projection(x, out_part, in_part, name):
    b, l, d = x.shape
    tm = min(ROW_TILE, l)
    operands, in_specs, out_specs, out_shapes, scratch = [], [], [], [], []
    if out_part is not None:
        operands += out_part[1]
        in_specs += out_part[2]
    operands.append(x)
    in_specs.append(_tok_spec(tm, d))
    if in_part is not None:
        operands += in_part[1]
        in_specs += in_part[2]
    if out_part is not None:
        out_specs.append(_tok_spec(tm, d))
        out_shapes.append(jax.ShapeDtypeStruct((b, l, d), F32))
    if in_part is not None:
        out_specs += in_part[3]
        out_shapes += in_part[4]
        scratch = in_part[5]
    kern = functools.partial(_proj_kernel, out_kind=out_part and out_part[0], in_kind=in_part and in_part[0])
    return pl.pallas_call(
        kern,
        grid=(b, l // tm),
        in_specs=in_specs,
        out_specs=out_specs,
        out_shape=out_shapes,
        scratch_shapes=scratch,
        compiler_params=_cparams(("parallel", "parallel")),
        name=name,
    )(*operands)


def _dft_kernel(ce_ref, se_ref, co_ref, so_ref, zc_ref, zs_ref, ga_ref, o_ref, acc_e, acc_o, *, nb, norm):
    kk = pl.program_id(1)

    @pl.when(kk == 0)
    def _():
        acc_e[...] = jnp.zeros_like(acc_e)
        acc_o[...] = jnp.zeros_like(acc_o)

    acc_e[...] += _dot(ce_ref[...], zc_ref[0]) + _dot(se_ref[...], zs_ref[0])
    acc_o[...] += _dot(co_ref[...], zc_ref[1]) + _dot(so_ref[...], zs_ref[1])

    @pl.when(kk == pl.num_programs(1) - 1)
    def _():
        for bi in range(nb):
            cs = slice(bi * FNET_WIDTH, (bi + 1) * FNET_WIDTH)
            e, o = acc_e[:, cs], acc_o[:, cs]
            for half, y in enumerate((e + o, e - o)):
                g = ga_ref[bi, half].astype(F32)
                o_ref[bi, half] = (y * norm * _silu(g)).astype(BF16)


def _dft_mix(cm, sm, zc, zs, ga):
    b, l, _ = ga.shape
    lh = l // 2
    tm = min(512, lh)
    tk = min(256, lh)
    nk = lh // tk
    nc = b * FNET_WIDTH
    kern = functools.partial(_dft_kernel, nb=b, norm=float((l * FNET_GROUP_DIM) ** -0.5))
    mat_e = pl.BlockSpec((tm, tk), lambda i, k: (i, k))
    mat_o = pl.BlockSpec((tm, tk), lambda i, k: (i, nk + k))
    zspec = pl.BlockSpec((2, tk, nc), lambda i, k: (0, k, 0))
    halves = pl.BlockSpec((b, 2, tm, FNET_WIDTH), lambda i, k: (0, 0, i, 0))
    out = pl.pallas_call(
        kern,
        grid=(lh // tm, nk),
        in_specs=[mat_e, mat_e, mat_o, mat_o, zspec, zspec, halves],
        out_specs=halves,
        out_shape=jax.ShapeDtypeStruct((b, 2, lh, FNET_WIDTH), BF16),
        scratch_shapes=[pltpu.VMEM((tm, nc), F32), pltpu.VMEM((tm, nc), F32)],
        compiler_params=_cparams(("parallel", "arbitrary")),
        name="dft_mix",
    )(cm, sm, cm, sm, zc, zs, ga.reshape(b, 2, lh, FNET_WIDTH))
    return out.reshape(b, l, FNET_WIDTH)


def _dft_gen_kernel(cb_ref, sb_ref, ca_ref, sa_ref, c_ref, s_ref):
    ca, sa = ca_ref[0], sa_ref[0]
    cb, sb = cb_ref[...], sb_ref[...]
    c_ref[...] = (ca * cb - sa * sb).astype(BF16)
    s_ref[...] = (-(sa * cb + ca * sb)).astype(BF16)


def _dft_mats(l):
    lh = l // 2
    tr = min(256, lh)
    kh = jnp.arange(lh, dtype=jnp.int32)
    k = jnp.concatenate([2 * kh, 2 * kh + 1])
    w = 2.0 * np.pi / l
    ang_b = ((jnp.arange(tr, dtype=jnp.int32)[:, None] * k[None, :]) % l).astype(F32) * w
    ang_a = ((jnp.arange(lh // tr, dtype=jnp.int32)[:, None] * tr * k[None, :]) % l).astype(F32) * w
    ca, sa = jnp.cos(ang_a)[:, None, :], jnp.sin(ang_a)[:, None, :]
    base = pl.BlockSpec((tr, l), lambda i: (0, 0))
    rowv = pl.BlockSpec((1, 1, l), lambda i: (i, 0, 0))
    out = pl.BlockSpec((tr, l), lambda i: (i, 0))
    sd = jax.ShapeDtypeStruct((lh, l), BF16)
    return pl.pallas_call(
        _dft_gen_kernel,
        grid=(lh // tr,),
        in_specs=[base, base, rowv, rowv],
        out_specs=[out, out],
        out_shape=[sd, sd],
        compiler_params=_cparams(("parallel",)),
        name="dft_gen",
    )(jnp.cos(ang_b), jnp.sin(ang_b), ca, sa)


def _group_dft_mat():
    n = FNET_GROUP_DIM
    j = np.arange(n)
    ang = 2.0 * np.pi * ((j[:, None] * j[None, :]) % n) / n
    eye = np.eye(FNET_WIDTH // n)
    mat = np.concatenate([np.kron(eye, np.cos(ang)), np.kron(eye, np.sin(ang))], axis=1)
    return jnp.asarray(mat, F32).astype(BF16)


HEADS_PER_TILE = 4
HEAD_TILE = HEADS_PER_TILE * NA_HEAD_DIM
HEAD_TILES = NA_WIDTH // HEAD_TILE


def _head_masks():
    row_head = lax.broadcasted_iota(jnp.int32, (HEAD_TILE, HEAD_TILE), 0) // GRID_W
    lane_head = lax.broadcasted_iota(jnp.int32, (HEAD_TILE, HEAD_TILE), 1) // NA_HEAD_DIM
    return row_head == lane_head, lax.broadcasted_iota(jnp.int32, (GRID_W, HEAD_TILE), 1) // NA_HEAD_DIM


def _stacked_queries(q4, own):
    return jnp.where(own, jnp.concatenate([q4] * HEADS_PER_TILE, axis=0), 0.0).astype(BF16)


def _own_head_blocks(o, out_head):
    o4 = o[0:GRID_W]
    for h in range(1, HEADS_PER_TILE):
        o4 = jnp.where(out_head == h, o[h * GRID_W:(h + 1) * GRID_W], o4)
    return o4


NA_ROWS_PER_STEP = 8


def _na_kernel(q_ref, k_ref, v_ref, kc_ref, vc_ref, gb_ref, bias_ref, o_ref, *, rows, kh, rps):
    nloc = kh * GRID_W
    own, out_head = _head_masks()
    for rr in range(rps):
        r = pl.program_id(1) * rps + rr
        r0 = jnp.clip(r - kh // 2, 0, rows - kh)
        start = pl.multiple_of(r0 * GRID_W, GRID_W)
        qs = slice(rr * GRID_W, (rr + 1) * GRID_W)
        for t in range(HEAD_TILES):
            cs = slice(t * HEAD_TILE, (t + 1) * HEAD_TILE)
            qm = _stacked_queries(q_ref[0, qs, cs], own)
            kl = k_ref[0, pl.ds(start, nloc), cs]
            vl = v_ref[0, pl.ds(start, nloc), cs]
            var = r0 - r + (NA_ROWS - 1)
            bias = jnp.concatenate([bias_ref[t, var + 2 * m_] for m_ in range(kh // 2)], axis=-1)
            s_loc = _dot_nt(qm, kl) + bias
            s_ctx = _dot_nt(qm, kc_ref[0, :, cs])
            m = jnp.maximum(jnp.max(s_loc, axis=-1, keepdims=True), jnp.max(s_ctx, axis=-1, keepdims=True))
            p_loc = jnp.exp2(s_loc - m)
            p_ctx = jnp.exp2(s_ctx - m)
            den = jnp.sum(p_loc, axis=-1, keepdims=True) + jnp.sum(p_ctx, axis=-1, keepdims=True)
            o = (_dot(p_loc.astype(BF16), vl) + _dot(p_ctx.astype(BF16), vc_ref[0, :, cs])) / den
            g = gb_ref[0, qs, cs].astype(F32)
            o_ref[0, qs, cs] = (_own_head_blocks(o, out_head) * _silu(g)).astype(BF16)


def _na_attention(q, k, v, kc, vc, gb, bias):
    b, l, w = q.shape
    lc = kc.shape[1]
    rows = l // GRID_W
    kh = min(NA_ROWS, rows)
    rps = NA_ROWS_PER_STEP

    row = pl.BlockSpec((1, rps * GRID_W, w), lambda bi, i: (bi, i, 0))
    full = lambda n: pl.BlockSpec((1, n, w), lambda bi, i: (bi, 0, 0))
    table = pl.BlockSpec(bias.shape, lambda bi, i: (0, 0, 0, 0), pipeline_mode=pl.Buffered(1))
    kern = functools.partial(_na_kernel, rows=rows, kh=kh, rps=rps)
    return pl.pallas_call(
        kern,
        grid=(b, rows // rps),
        in_specs=[row, full(l), full(l), full(lc), full(lc), row, table],
        out_specs=row,
        out_shape=jax.ShapeDtypeStruct((b, l, w), BF16),
        compiler_params=_cparams(("parallel", "arbitrary")),
        name="na_attention",
    )(q, k, v, kc, vc, gb, bias)


def _na_bias(rpb, kh):
    assert kh % 2 == 0
    cq = np.arange(GRID_W)
    ck = np.arange(GRID_W)
    c0 = np.clip(cq - NA_COLS // 2, 0, GRID_W - NA_COLS)
    col_in = (ck[None, :] >= c0[:, None]) & (ck[None, :] < c0[:, None] + NA_COLS)
    dc = np.clip(ck[None, :] - cq[:, None] + (NA_COLS - 1), 0, 2 * NA_COLS - 2)
    onehot = (dc.reshape(1, -1) == np.arange(2 * NA_COLS - 1)[:, None]).astype(np.float32)
    t = jnp.einsum('hrc,cx->hrx', rpb.astype(F32) * LOG2E, jnp.asarray(onehot), precision=lax.Precision.HIGHEST)
    n_dr = 2 * NA_ROWS - 1
    t = t.reshape(NA_HEADS, n_dr, GRID_W, GRID_W)
    t = jnp.where(col_in[None, None], t, -1e30)
    pairs = jnp.concatenate([t[:, :-1], t[:, 1:]], axis=-1)
    pairs = pairs.reshape(HEAD_TILES, HEADS_PER_TILE, n_dr - 1, GRID_W, 2 * GRID_W).transpose(0, 2, 1, 3, 4)
    return pairs.reshape(HEAD_TILES, n_dr - 1, HEAD_TILE, 2 * GRID_W)


def _ctx_attn_kernel(q_ref, k_ref, v_ref, gb_ref, o_ref):
    own, out_head = _head_masks()
    for t in range(HEAD_TILES):
        cs = slice(t * HEAD_TILE, (t + 1) * HEAD_TILE)
        for q0 in range(0, q_ref.shape[1], GRID_W):
            qs = slice(q0, q0 + GRID_W)
            s = _dot_nt(_stacked_queries(q_ref[0, qs, cs], own), k_ref[0, :, cs])
            p = jnp.exp2(s - jnp.max(s, axis=-1, keepdims=True))
            o = _dot(p.astype(BF16), v_ref[0, :, cs]) / jnp.sum(p, axis=-1, keepdims=True)
            g = gb_ref[0, qs, cs].astype(F32)
            o_ref[0, qs, cs] = (_own_head_blocks(o, out_head) * _silu(g)).astype(BF16)


def _ctx_attention(q, k, v, gb):
    b, lc, w = q.shape
    spec = pl.BlockSpec((1, lc, w), lambda bi: (bi, 0, 0))
    return pl.pallas_call(
        _ctx_attn_kernel,
        grid=(b,),
        in_specs=[spec, spec, spec, spec],
        out_specs=spec,
        out_shape=jax.ShapeDtypeStruct((b, lc, w), BF16),
        compiler_params=_cparams(("parallel",)),
        name="ctx_attention",
    )(q, k, v, gb)


SUPER_K = S5_CHUNK * SUPER_LANES
SG_ROW_TILE = 512
SG_APPLY_ROWS = 256
REC_LANES = 1024


def _gather_chunks(u_ref, u_scr):
    rows = u_scr.shape[0]
    for s in range(S5_CHUNK):
        u_scr[:, s * SUPER_LANES:(s + 1) * SUPER_LANES] = u_ref[0, pl.ds(s, rows, stride=S5_CHUNK), :].astype(BF16)


def _summary_kernel(u_ref, p_ref, o_ref, w_scr, u_scr):
    @pl.when(pl.program_id(1) == 0)
    def _():
        row_group = (lax.broadcasted_iota(jnp.int32, (SUPER_K, S5_TILE), 0) % SUPER_LANES) // S5_GROUP_DIM
        p = p_ref[0]
        for h in range(S5_SUPER):
            w_scr[:, h * S5_TILE:(h + 1) * S5_TILE] = jnp.where(row_group == h, p, 0.0).astype(BF16)

    _gather_chunks(u_ref, u_scr)
    o_ref[0] = _dot(u_scr[...], w_scr[...])


def _s5_summary(u, p_flat, name):
    k = SUPER_K
    r = u.shape[1] // S5_CHUNK
    tr = min(SG_ROW_TILE, r)
    return pl.pallas_call(
        _summary_kernel,
        grid=(N_SUPER, r // tr),
        in_specs=[pl.BlockSpec((1, tr * S5_CHUNK, SUPER_LANES), lambda j, ri: (j, ri, 0)),
                  pl.BlockSpec((1, k, S5_TILE), lambda j, ri: (j, 0, 0))],
        out_specs=pl.BlockSpec((1, tr, k), lambda j, ri: (j, ri, 0)),
        out_shape=jax.ShapeDtypeStruct((N_SUPER, r, k), F32),
        scratch_shapes=[pltpu.VMEM((k, k), BF16), pltpu.VMEM((tr, k), BF16)],
        compiler_params=_cparams(("parallel", "arbitrary")),
        name=name,
    )(u, p_flat)


def _apply_kernel(u_ref, h_ref, lag_ref, q_ref, sel_ref, o_ref, w_scr, q_scr, u_scr):
    @pl.when(pl.program_id(1) == 0)
    def _():
        for s in range(S5_CHUNK):
            lo = SUPER_LANES * (S5_CHUNK - 1 - s)
            w_scr[s * SUPER_LANES:(s + 1) * SUPER_LANES, :] = lag_ref[0, :, lo:lo + SUPER_K]
        col_group = (lax.broadcasted_iota(jnp.int32, (S5_TILE, SUPER_K), 1) % SUPER_LANES) // S5_GROUP_DIM
        for g in range(S5_SUPER):
            rs = slice(g * S5_TILE, (g + 1) * S5_TILE)
            q_scr[rs, :] = jnp.where(col_group == g, _dot(q_ref[0, rs, :], sel_ref[...]), 0.0).astype(BF16)

    _gather_chunks(u_ref, u_scr)
    acc = _dot(u_scr[...], w_scr[...]) + _dot(h_ref[0].astype(BF16), q_scr[...])
    rows = u_scr.shape[0]
    for s in range(S5_CHUNK):
        o_ref[0, pl.ds(s, rows, stride=S5_CHUNK), :] = acc[:, s * SUPER_LANES:(s + 1) * SUPER_LANES]


def _s5_apply(u, h, lag_blocks, q_flat, name):
    k = SUPER_K
    r = h.shape[1]
    tr = min(SG_APPLY_ROWS, r)
    rows = pl.BlockSpec((1, tr, k), lambda j, ri: (j, ri, 0))
    tok_rows = pl.BlockSpec((1, tr * S5_CHUNK, SUPER_LANES), lambda j, ri: (j, ri, 0))
    ri_, ci_ = np.arange(S5_TILE)[:, None], np.arange(k)[None, :]
    sel = (ri_ // S5_GROUP_DIM == ci_ // SUPER_LANES) & (ri_ % S5_GROUP_DIM == ci_ % S5_GROUP_DIM)
    sel = jnp.asarray(sel, F32).astype(BF16)
    return pl.pallas_call(
        _apply_kernel,
        grid=(N_SUPER, r // tr),
        in_specs=[tok_rows, rows,
                  pl.BlockSpec((1,) + lag_blocks.shape[1:], lambda j, ri: (j, 0, 0)),
                  pl.BlockSpec((1, k, S5_TILE), lambda j, ri: (j, 0, 0)),
                  pl.BlockSpec(sel.shape, lambda j, ri: (0, 0))],
        out_specs=tok_rows,
        out_shape=jax.ShapeDtypeStruct(u.shape, F32),
        scratch_shapes=[pltpu.VMEM((k, k), BF16), pltpu.VMEM((k, k), BF16), pltpu.VMEM((tr, k), BF16)],
        compiler_params=_cparams(("parallel", "arbitrary")),
        name=name,
    )(u, h, lag_blocks, q_flat, sel)


def _rec_kernel(sc_ref, sl_ref, a_ref, hc_ref, hl_ref, *, nb, n_ctx, n_lat):
    half = S5_TILE // 2
    npieces = REC_LANES // S5_TILE
    fwd_lane = lax.broadcasted_iota(jnp.int32, (1, half), 1) < S5_STATE
    coef = [(a_ref[0, :, p * S5_TILE:p * S5_TILE + half], a_ref[0, :, p * S5_TILE + half:(p + 1) * S5_TILE])
            for p in range(npieces)]

    def step(s_ref, h_ref, c, carry, forward):
        rows = pl.ds(pl.multiple_of(c * nb, nb), nb)
        out = []
        for p in range(npieces):
            hre, him = carry[2 * p], carry[2 * p + 1]
            are, aim = coef[p]
            re_sl = slice(p * S5_TILE, p * S5_TILE + half)
            im_sl = slice(p * S5_TILE + half, (p + 1) * S5_TILE)
            if forward:
                h_ref[0, rows, re_sl] = jnp.where(fwd_lane, hre, 0.0)
                h_ref[0, rows, im_sl] = jnp.where(fwd_lane, him, 0.0)
            else:
                h_ref[0, rows, re_sl] = jnp.where(fwd_lane, h_ref[0, rows, re_sl], hre)
                h_ref[0, rows, im_sl] = jnp.where(fwd_lane, h_ref[0, rows, im_sl], him)
            sre, sim = s_ref[0, rows, re_sl], s_ref[0, rows, im_sl]
            out.append(are * hre - aim * him + sre)
            out.append(are * him + aim * hre + sim)
        return tuple(out)

    zero = tuple(jnp.zeros((nb, half), F32) for _ in range(2 * npieces))
    carry = lax.fori_loop(0, n_ctx, lambda c, cr: step(sc_ref, hc_ref, c, cr, True), zero)
    lax.fori_loop(0, n_lat, lambda c, cr: step(sl_ref, hl_ref, c, cr, True), carry)
    carry = lax.fori_loop(0, n_ctx, lambda i, cr: step(sc_ref, hc_ref, n_ctx - 1 - i, cr, False), zero)
    lax.fori_loop(0, n_lat, lambda i, cr: step(sl_ref, hl_ref, n_lat - 1 - i, cr, False), carry)


def _chunk_recurrence(s_c, s_l, a16, nb):
    _, rc, lanes = s_c.shape
    rl = s_l.shape[1]
    spec = lambda r: pl.BlockSpec((1, r, REC_LANES), lambda j, i: (j, 0, i))
    return pl.pallas_call(
        functools.partial(_rec_kernel, nb=nb, n_ctx=rc // nb, n_lat=rl // nb),
        grid=(N_SUPER, lanes // REC_LANES),
        in_specs=[spec(rc), spec(rl), spec(1)],
        out_specs=[spec(rc), spec(rl)],
        out_shape=[jax.ShapeDtypeStruct(s_c.shape, F32), jax.ShapeDtypeStruct(s_l.shape, F32)],
        compiler_params=_cparams(("parallel", "parallel")),
        name="s5_recurrence",
    )(s_c, s_l, a16)


def _cmul(ar, ai, br, bi):
    return ar * br - ai * bi, ar * bi + ai * br


def _s5_matrices(lam_re, lam_im, log_step, b_re, b_im, c_re, c_im):
    t = S5_CHUNK
    hp = lax.Precision.HIGHEST
    taus = jnp.arange(t + 1, dtype=F32)[:, None, None]
    er, ei, wr, wi, a_re, a_im, ks = [], [], [], [], [], [], []
    for d in range(2):
        lr = jnp.minimum(lam_re[d].astype(F32), -1e-4)
        li = lam_im[d].astype(F32)
        dt = jnp.exp(log_step[d].astype(F32))[:, None]
        mag = jnp.exp(lr * dt * taus)
        pr, pi = mag * jnp.cos(li * dt * taus), mag * jnp.sin(li * dt * taus)
        den = lr * lr + li * li
        qr = ((pr[1] - 1.0) * lr + pi[1] * li) / den
        qi = (pi[1] * lr - (pr[1] - 1.0) * li) / den
        bbr, bbi = _cmul(qr[..., None], qi[..., None], b_re[d].astype(F32), b_im[d].astype(F32))
        e_r, e_i = _cmul(pr[..., None], pi[..., None], bbr[None], bbi[None])
        cr, ci = c_re[d].astype(F32), c_im[d].astype(F32)
        w_r, w_i = _cmul(cr[None], ci[None], pr[:, :, None, :], pi[:, :, None, :])
        k = (jnp.einsum('gmp,tgpn->tgmn', cr, e_r[:t], precision=hp)
             - jnp.einsum('gmp,tgpn->tgmn', ci, e_i[:t], precision=hp))
        er.append(e_r); ei.append(e_i); wr.append(w_r); wi.append(w_i); ks.append(k)
        a_re.append(pr[t]); a_im.append(pi[t])
    rows = lambda e: e.transpose(1, 0, 3, 2)
    p_mat = jnp.concatenate([rows(er[0][:t][::-1]), rows(er[1][:t]),
                             rows(ei[0][:t][::-1]), rows(ei[1][:t])], axis=-1)
    cols = lambda w: w.transpose(1, 3, 0, 2)
    q_mat = jnp.concatenate([cols(wr[0][1:]), cols(wr[1][1:][::-1]),
                             -cols(wi[0][1:]), -cols(wi[1][1:][::-1])], axis=1)
    a16 = jnp.concatenate([a_re[0], a_re[1], a_im[0], a_im[1]], axis=-1)

    gd, sl = S5_GROUP_DIM, SUPER_LANES

    def spread(a, nblk):
        r = jnp.arange(nblk * gd)[:, None]
        c = jnp.arange(nblk * sl)[None, :]
        sel = ((r // gd == c // sl) & (r % gd == c % gd)).astype(BF16)
        return jnp.einsum('jrk,kc->jrc', a.astype(BF16), sel, preferred_element_type=F32).astype(BF16)

    def same_group(row_group, col_group):
        return (row_group[:, None] == col_group[None, :]).astype(BF16)[None]

    nlag = 2 * t - 1
    kf = ks[0].transpose(0, 1, 3, 2)
    kb = ks[1].transpose(0, 1, 3, 2)
    klag = jnp.concatenate([kb[1:][::-1], (kf[0] + kb[0])[None], kf[1:]], axis=0)
    k_flat = klag.transpose(1, 2, 0, 3).reshape(N_SUPER, sl, nlag * gd)
    lag_blocks = spread(k_flat, nlag) * same_group(jnp.arange(sl) // gd, (jnp.arange(nlag * sl) % sl) // gd)
    p_flat = p_mat.reshape(N_SUPER, S5_SUPER, t, gd, S5_TILE).transpose(0, 2, 1, 3, 4).reshape(N_SUPER, SUPER_K, S5_TILE)
    q_flat = q_mat.reshape(N_SUPER, S5_SUPER * S5_TILE, S5_TILE)
    return lag_blocks, p_flat.astype(BF16), q_flat.astype(BF16), a16.reshape(N_SUPER, 1, SUPER_K)


def _s5_scan(s_l, s_c, mats):
    lag_blocks, p_flat, q_flat, a16 = mats
    b = s_l.shape[2]
    u_l = s_l.reshape(N_SUPER, -1, SUPER_LANES)
    u_c = s_c.reshape(N_SUPER, -1, SUPER_LANES)
    sum_l = _s5_summary(u_l, p_flat, "s5_summary")
    sum_c = _s5_summary(u_c, p_flat, "s5_summary_ctx")
    h_c, h_l = _chunk_recurrence(sum_c, sum_l, a16, b)
    y_l = _s5_apply(u_l, h_l, lag_blocks, q_flat, "s5_apply")
    y_c = _s5_apply(u_c, h_c, lag_blocks, q_flat, "s5_apply_ctx")
    return y_l.reshape(s_l.shape), y_c.reshape(s_c.shape)


def kernel(x, c, ctx, c_ctx, w_ada, b_ada, pre_g, post_g, w_in_even, w_out_even, na_rpb,
           w_in_odd, w_out_odd, sgu_w, sgu_b, sgu_g, s5_lam_re, s5_lam_im, s5_log_step,
           s5_b_re, s5_b_im, s5_c_re, s5_c_im, s5_d, glu_w, glu_b):
    b, l, d = x.shape
    lc = ctx.shape[1]
    depth = w_ada.shape[0]
    rows = l // GRID_W
    kh = min(NA_ROWS, rows)

    n_rows = -(-(b + 1) // 8) * 8
    cond = jnp.zeros((n_rows, d), F32).at[:b].set(c).at[b].set(c_ctx)
    mod = _ada_mod(cond, w_ada, b_ada)

    bd = _group_dft_mat()
    cm_l, sm_l = _dft_mats(l)
    cm_c, sm_c = _dft_mats(lc)

    w_in = [(w_in_even if i % 2 == 0 else w_in_odd)[i // 2].astype(BF16) for i in range(depth)]
    n_odd = w_in_odd.shape[0]
    gw = SGU_WIDTH // SGU_GROUPS
    sgu_ws = [sgu_w[j].astype(BF16) for j in range(n_odd)]
    sgu_bs = [jnp.repeat(sgu_b[j].astype(F32).T, gw, axis=1) for j in range(n_odd)]
    tm_l, tm_c = min(ROW_TILE, l), min(ROW_TILE, lc)

    def modulation(i):
        lat = tuple(mod[i, :b, k * d:(k + 1) * d][:, None, :] for k in range(3))
        ctx_mod = jnp.broadcast_to(mod[i, b][None, None, :], (b, 1, 3 * d))
        return lat, tuple(ctx_mod[..., k * d:(k + 1) * d] for k in range(3))

    def in_part(i, seq, shift, scale):
        tm = min(ROW_TILE, seq)
        if i % 2 == 0:
            return ("even",) + _even_in_io(b, seq, d, tm, pre_g[i], scale, shift, w_in[i], bd)
        j = i // 2
        return ("odd",) + _odd_in_io(b, seq, d, tm, pre_g[i], scale, shift, w_in[i], sgu_ws[j], sgu_bs[j], sgu_g[j])

    (shift, scale, gate), (shift_c, scale_c, gate_c) = modulation(0)
    xl, xc = x, ctx
    res_l = _projection(xl, None, in_part(0, l, shift, scale), "proj_in")
    res_c = _projection(xc, None, in_part(0, lc, shift_c, scale_c), "proj_in_ctx")
    for i in range(depth):
        last = i == depth - 1
        j = i // 2
        if i % 2 == 0:
            w_out = w_out_even[j].astype(BF16)
            zc, zs, ga, q, k_, v, gb = res_l
            zc_c, zs_c, ga_c, q_c, k_c, v_c, gb_c = res_c
            a_l = _dft_mix(cm_l, sm_l, zc, zs, ga)
            n_l = _na_attention(q, k_, v, k_c, v_c, gb, _na_bias(na_rpb[j], kh))
            out_l = ("even",) + _even_out_io(d, tm_l, a_l, n_l, w_out, post_g[i], gate)
            if not last:
                a_c = _dft_mix(cm_c, sm_c, zc_c, zs_c, ga_c)
                n_c = _ctx_attention(q_c, k_c, v_c, gb_c)
                out_c = ("even",) + _even_out_io(d, tm_c, a_c, n_c, w_out, post_g[i], gate_c)
        else:
            w_out = w_out_odd[j].astype(BF16)
            mats = _s5_matrices(s5_lam_re[j], s5_lam_im[j], s5_log_step[j], s5_b_re[j], s5_b_im[j],
                                s5_c_re[j], s5_c_im[j])
            sg_l, s_l, gd_l = res_l
            sg_c, s_c, gd_c = res_c
            y_l, y_c = _s5_scan(s_l, s_c, mats)
            wg = glu_w[j].astype(BF16)
            out_l = ("odd",) + _odd_out_io(d, tm_l, sg_l, y_l, s_l, gd_l, s5_d[j], wg, glu_b[j], w_out, post_g[i], gate)
            if not last:
                out_c = ("odd",) + _odd_out_io(d, tm_c, sg_c, y_c, s_c, gd_c, s5_d[j], wg, glu_b[j], w_out,
                                               post_g[i], gate_c)
        if last:
            (xl,) = _projection(xl, out_l, None, "proj_out")
        else:
            (shift, scale, gate), (shift_c, scale_c, gate_c) = modulation(i + 1)
            xl, *res_l = _projection(xl, out_l, in_part(i + 1, l, shift, scale), "proj_out_in")
            xc, *res_c = _projection(xc, out_c, in_part(i + 1, lc, shift_c, scale_c), "proj_out_in_ctx")
    return xl
```

```python
import functools

import numpy as np
import jax
import jax.numpy as jnp
from jax import lax
from jax.experimental import pallas as pl
from jax.experimental.pallas import tpu as pltpu

F32 = jnp.float32
BF16 = jnp.bfloat16

EPS = 1e-6
GRID_W = 64
FNET_WIDTH = 256
FNET_GROUP_DIM = 64
NA_WIDTH = 768
NA_HEAD_DIM = 64
NA_HEADS = 12
NA_ROWS = 8
NA_COLS = 16
SGU_CHUNK = 128
SGU_WIDTH = 512
SGU_GROUPS = 4
S5_WIDTH = 512
S5_GROUP_DIM = 16
S5_GROUPS = 32
S5_STATE = 64
S5_CHUNK = 16
S5_TILE = S5_CHUNK * S5_GROUP_DIM
LOG2E = float(np.log2(np.e))
QK_SCALE = NA_HEAD_DIM ** -0.5 * LOG2E

V7X_VMEM_BYTES = 64 * 1024 * 1024
VMEM_LIMIT = 48 * 1024 * 1024
ROW_TILE = 512


def _cparams(sem):
    return pltpu.CompilerParams(dimension_semantics=sem, vmem_limit_bytes=VMEM_LIMIT)


def _silu(x):
    return x * jax.nn.sigmoid(x)


def _dot(a, b):
    return jnp.dot(a, b, preferred_element_type=F32)


def _dot_nt(a, b):
    return lax.dot_general(a, b, (((1,), (1,)), ((), ())), preferred_element_type=F32)


def _ada_kernel(c_ref, w_ref, b_ref, o_ref):
    c = c_ref[...]
    o_ref[0] = jnp.dot(_silu(c), w_ref[0], preferred_element_type=F32,
                       precision=lax.Precision.HIGHEST) + b_ref[0]


def _ada_mod(cond, w_ada, b_ada):
    depth, d, n = w_ada.shape
    rows = cond.shape[0]
    tn = 1024
    return pl.pallas_call(
        _ada_kernel,
        grid=(depth, n // tn),
        in_specs=[pl.BlockSpec((rows, d), lambda i, j: (0, 0)),
                  pl.BlockSpec((1, d, tn), lambda i, j: (i, 0, j)),
                  pl.BlockSpec((1, 1, tn), lambda i, j: (i, 0, j))],
        out_specs=pl.BlockSpec((1, rows, tn), lambda i, j: (i, 0, j)),
        out_shape=jax.ShapeDtypeStruct((depth, rows, n), F32),
        compiler_params=_cparams(("parallel", "parallel")),
        name="ada_mod",
    )(cond, w_ada, b_ada.reshape(depth, 1, n))


def _norm_mod(x, g, scale, shift):
    y = x * lax.rsqrt(jnp.mean(x * x, axis=-1, keepdims=True) + EPS)
    return (y * g) * (1.0 + scale) + shift


def _post(y, post_g, gate, x):
    yn = y * lax.rsqrt(jnp.mean(y * y, axis=-1, keepdims=True) + EPS)
    return x + gate * (yn * post_g)


def _vec_spec(d):
    return pl.BlockSpec((1, 1, d), lambda bi, i: (bi, 0, 0))


def _const_spec(a):
    return pl.BlockSpec(a.shape, lambda bi, i: (0,) * a.ndim)


def _layer_spec(a, j):
    return pl.BlockSpec((1,) + a.shape[1:], lambda bi, i: (j,) + (0,) * (a.ndim - 1))


def _tok_spec(tm, width):
    return pl.BlockSpec((1, tm, width), lambda bi, i: (bi, i, 0))


def _even_in_body(x, g_ref, sc, sh, w_ref, bd_ref, z_scr, outs):
    zc_ref, zs_ref, ga_ref, q_ref, k_ref, v_ref, gb_ref = outs
    h = _norm_mod(x, g_ref[...], sc, sh).astype(BF16)
    fa = _dot(h, w_ref[0, :, 0:FNET_WIDTH]).astype(BF16)
    z = _dot(fa, bd_ref[...])
    planes = z.shape[1] // 128
    for p in range(planes):
        z_scr[p] = z[:, p * 128:(p + 1) * 128]
    half = z.shape[0] // 2
    for par in range(2):
        for p in range(planes):
            piece = z_scr[p, pl.ds(par, half, stride=2), :].astype(BF16)
            ref, q0 = (zc_ref, p) if p < planes // 2 else (zs_ref, p - planes // 2)
            ref[par, :, q0 * 128:(q0 + 1) * 128] = piece
    o = FNET_WIDTH
    ga_ref[0] = _dot(h, w_ref[0, :, o:o + FNET_WIDTH]).astype(BF16)
    o += FNET_WIDTH
    q_ref[0] = (_dot(h, w_ref[0, :, o:o + NA_WIDTH]) * QK_SCALE).astype(BF16)
    o += NA_WIDTH
    k_ref[0] = _dot(h, w_ref[0, :, o:o + NA_WIDTH]).astype(BF16)
    o += NA_WIDTH
    v_ref[0] = _dot(h, w_ref[0, :, o:o + NA_WIDTH]).astype(BF16)
    o += NA_WIDTH
    gb_ref[0] = _dot(h, w_ref[0, :, o:o + NA_WIDTH]).astype(BF16)


def _even_in_io(b, l, d, tm, pre_g, scale, shift, w, j, bd):
    zspec = pl.BlockSpec((2, tm // 2, FNET_WIDTH), lambda bi, i: (0, i, bi))
    zsd = jax.ShapeDtypeStruct((2, l // 2, b * FNET_WIDTH), BF16)
    sd = lambda width: jax.ShapeDtypeStruct((b, l, width), BF16)
    g2 = pre_g.reshape(1, d)
    widths = (FNET_WIDTH, NA_WIDTH, NA_WIDTH, NA_WIDTH, NA_WIDTH)
    return ([g2, scale, shift, w, bd],
            [_const_spec(g2), _vec_spec(d), _vec_spec(d), _layer_spec(w, j), _const_spec(bd)],
            [zspec, zspec] + [_tok_spec(tm, wd) for wd in widths],
            [zsd, zsd] + [sd(wd) for wd in widths],
            [pltpu.VMEM((2 * FNET_WIDTH // 128, tm, 128), F32)])


def _even_out_body(a_ref, n_ref, w_ref, pg_ref, gate, x):
    y = _dot(a_ref[0], w_ref[0, :FNET_WIDTH, :]) + _dot(n_ref[0], w_ref[0, FNET_WIDTH:, :])
    return _post(y, pg_ref[...], gate, x)


def _even_out_io(d, tm, a, n, w, j, post_g, gate):
    pg2 = post_g.reshape(1, d)
    return ([a, n, w, pg2, gate],
            [_tok_spec(tm, FNET_WIDTH), _tok_spec(tm, NA_WIDTH), _layer_spec(w, j), _const_spec(pg2), _vec_spec(d)])


S5_SUPER = 8
N_SUPER = S5_GROUPS // S5_SUPER
SUPER_LANES = S5_SUPER * S5_GROUP_DIM


def _super_spec(tm):
    return pl.BlockSpec((N_SUPER, tm // S5_CHUNK, 1, S5_CHUNK, SUPER_LANES), lambda bi, i: (0, i, bi, 0, 0))


def _super_shape(b, l):
    return jax.ShapeDtypeStruct((N_SUPER, l // S5_CHUNK, b, S5_CHUNK, SUPER_LANES), F32)


def _odd_in_body(x, g_ref, sc, sh, w_ref, ws_ref, bs_ref, sg_g_ref, outs):
    sg_ref, s_ref, gd_ref = outs
    tm = x.shape[0]
    h = _norm_mod(x, g_ref[...], sc, sh).astype(BF16)
    wd = SGU_WIDTH
    u = _dot(h, w_ref[0, :, 0:wd])
    v = _dot(h, w_ref[0, :, wd:2 * wd])
    gc = _dot(h, w_ref[0, :, 2 * wd:3 * wd])
    s = _dot(h, w_ref[0, :, 3 * wd:3 * wd + S5_WIDTH])
    for j in range(N_SUPER):
        sj = s[:, j * SUPER_LANES:(j + 1) * SUPER_LANES]
        s_ref[j, :, 0] = sj.reshape(tm // S5_CHUNK, S5_CHUNK, SUPER_LANES)
    gd_ref[0] = _dot(h, w_ref[0, :, 3 * wd + S5_WIDTH:]).astype(BF16)
    vc = v - jnp.mean(v, axis=-1, keepdims=True)
    vn = (vc * lax.rsqrt(jnp.mean(vc * vc, axis=-1, keepdims=True) + EPS) * sg_g_ref[...]).astype(BF16)
    gate = u * _silu(gc)
    gw = SGU_WIDTH // SGU_GROUPS
    for j in range(tm // SGU_CHUNK):
        rs = slice(j * SGU_CHUNK, (j + 1) * SGU_CHUNK)
        for g in range(SGU_GROUPS):
            cs = slice(g * gw, (g + 1) * gw)
            mixed = _dot(ws_ref[0, g], vn[rs, cs]) + bs_ref[:, cs]
            sg_ref[0, rs, cs] = (gate[rs, cs] * mixed).astype(BF16)


def _odd_in_io(b, l, d, tm, pre_g, scale, shift, w, ws, j, bs_full, sgu_g):
    g2 = pre_g.reshape(1, d)
    sg2 = sgu_g.reshape(1, SGU_WIDTH)
    sd = jax.ShapeDtypeStruct((b, l, SGU_WIDTH), BF16)
    return ([g2, scale, shift, w, ws, bs_full, sg2],
            [_const_spec(g2), _vec_spec(d), _vec_spec(d), _layer_spec(w, j), _layer_spec(ws, j), _const_spec(bs_full),
             _const_spec(sg2)],
            [_tok_spec(tm, SGU_WIDTH), _super_spec(tm), _tok_spec(tm, S5_WIDTH)],
            [sd, _super_shape(b, l), sd],
            [])


def _gelu_tanh(x):
    return 0.5 * x * (1.0 + jnp.tanh(float(np.sqrt(2.0 / np.pi)) * (x + 0.044715 * (x * x * x))))


def _odd_out_body(sg_ref, y_ref, s_ref, gd_ref, dsk_ref, wg_ref, bg_ref, w_ref, pg_ref, gate, x):
    tm = x.shape[0]
    tokens = lambda ref: jnp.concatenate([ref[j, :, 0].reshape(tm, SUPER_LANES) for j in range(N_SUPER)], axis=-1)
    ys = tokens(y_ref)
    s = tokens(s_ref)
    z = _gelu_tanh(ys + dsk_ref[...] * s)
    zz = _dot(z.astype(BF16), wg_ref[0]) + bg_ref[...]
    gd = gd_ref[0].astype(F32)
    ss = zz[:, :S5_WIDTH] * jax.nn.sigmoid(zz[:, S5_WIDTH:]) * _silu(gd)
    y = _dot(sg_ref[0], w_ref[0, :SGU_WIDTH, :]) + _dot(ss.astype(BF16), w_ref[0, SGU_WIDTH:, :])
    return _post(y, pg_ref[...], gate, x)


def _odd_out_io(d, tm, sg, y, s, gd, d_skip, wg, bg, w, j, post_g, gate):
    dsk = d_skip.reshape(1, S5_WIDTH).astype(F32)
    bg2 = bg.reshape(1, 2 * S5_WIDTH).astype(F32)
    pg2 = post_g.reshape(1, d)
    return ([sg, y, s, gd, dsk, wg, bg2, w, pg2, gate],
            [_tok_spec(tm, SGU_WIDTH), _super_spec(tm), _super_spec(tm), _tok_spec(tm, S5_WIDTH),
             _const_spec(dsk), _layer_spec(wg, j), _const_spec(bg2), _layer_spec(w, j), _const_spec(pg2), _vec_spec(d)])


_OUT_BODY = {"even": (_even_out_body, 5), "odd": (_odd_out_body, 10)}
_IN_BODY = {"even": (_even_in_body, 5, 7), "odd": (_odd_in_body, 7, 3)}


def _proj_kernel(*refs, out_kind, in_kind):
    pos = 0
    if out_kind is not None:
        body, n_ops = _OUT_BODY[out_kind]
        ops = refs[pos:pos + n_ops]
        pos += n_ops
    x_ref = refs[pos]
    pos += 1
    if in_kind is not None:
        in_body, n_in, n_out = _IN_BODY[in_kind]
        in_ops = refs[pos:pos + n_in]
        pos += n_in
    x = x_ref[0]
    if out_kind is not None:
        x = body(*ops[:-1], ops[-1][0], x)
        refs[pos][0] = x
        pos += 1
    if in_kind is not None:
        outs = refs[pos:pos + n_out]
        scratch = refs[pos + n_out:]
        g_ref, sc_ref, sh_ref = in_ops[:3]
        in_body(x, g_ref, sc_ref[0], sh_ref[0], *in_ops[3:], *scratch, outs)


def _projection(x, out_part, in_part, name):
    b, l, d = x.shape
    tm = min(ROW_TILE, l)
    operands, in_specs, out_specs, out_shapes, scratch = [], [], [], [], []
    if out_part is not None:
        operands += out_part[1]
        in_specs += out_part[2]
    operands.append(x)
    in_specs.append(_tok_spec(tm, d))
    if in_part is not None:
        operands += in_part[1]
        in_specs += in_part[2]
    if out_part is not None:
        out_specs.append(_tok_spec(tm, d))
        out_shapes.append(jax.ShapeDtypeStruct((b, l, d), F32))
    if in_part is not None:
        out_specs += in_part[3]
        out_shapes += in_part[4]
        scratch = in_part[5]
    kern = functools.partial(_proj_kernel, out_kind=out_part and out_part[0], in_kind=in_part and in_part[0])
    return pl.pallas_call(
        kern,
        grid=(b, l // tm),
        in_specs=in_specs,
        out_specs=out_specs,
        out_shape=out_shapes,
        scratch_shapes=scratch,
        compiler_params=_cparams(("parallel", "parallel")),
        name=name,
    )(*operands)


def _dft_kernel(ce_ref, se_ref, co_ref, so_ref, zc_ref, zs_ref, ga_ref, o_ref, acc_e, acc_o, *, nb, norm):
    kk = pl.program_id(1)

    @pl.when(kk == 0)
    def _():
        acc_e[...] = jnp.zeros_like(acc_e)
        acc_o[...] = jnp.zeros_like(acc_o)

    acc_e[...] += _dot(ce_ref[...], zc_ref[0]) + _dot(se_ref[...], zs_ref[0])
    acc_o[...] += _dot(co_ref[...], zc_ref[1]) + _dot(so_ref[...], zs_ref[1])

    @pl.when(kk == pl.num_programs(1) - 1)
    def _():
        for bi in range(nb):
            cs = slice(bi * FNET_WIDTH, (bi + 1) * FNET_WIDTH)
            e, o = acc_e[:, cs], acc_o[:, cs]
            for half, y in enumerate((e + o, e - o)):
                g = ga_ref[bi, half].astype(F32)
                o_ref[bi, half] = (y * norm * _silu(g)).astype(BF16)


def _dft_mix(cm, sm, zc, zs, ga):
    b, l, _ = ga.shape
    lh = l // 2
    tm = min(512, lh)
    tk = min(256, lh)
    nk = lh // tk
    nc = b * FNET_WIDTH
    kern = functools.partial(_dft_kernel, nb=b, norm=float((l * FNET_GROUP_DIM) ** -0.5))
    mat_e = pl.BlockSpec((tm, tk), lambda i, k: (i, k))
    mat_o = pl.BlockSpec((tm, tk), lambda i, k: (i, nk + k))
    zspec = pl.BlockSpec((2, tk, nc), lambda i, k: (0, k, 0))
    halves = pl.BlockSpec((b, 2, tm, FNET_WIDTH), lambda i, k: (0, 0, i, 0))
    out = pl.pallas_call(
        kern,
        grid=(lh // tm, nk),
        in_specs=[mat_e, mat_e, mat_o, mat_o, zspec, zspec, halves],
        out_specs=halves,
        out_shape=jax.ShapeDtypeStruct((b, 2, lh, FNET_WIDTH), BF16),
        scratch_shapes=[pltpu.VMEM((tm, nc), F32), pltpu.VMEM((tm, nc), F32)],
        compiler_params=_cparams(("parallel", "arbitrary")),
        name="dft_mix",
    )(cm, sm, cm, sm, zc, zs, ga.reshape(b, 2, lh, FNET_WIDTH))
    return out.reshape(b, l, FNET_WIDTH)


def _dft_gen_kernel(cb_ref, sb_ref, ca_ref, sa_ref, c_ref, s_ref):
    ca, sa = ca_ref[0], sa_ref[0]
    cb, sb = cb_ref[...], sb_ref[...]
    c_ref[...] = (ca * cb - sa * sb).astype(BF16)
    s_ref[...] = (-(sa * cb + ca * sb)).astype(BF16)


def _dft_mats(l):
    lh = l // 2
    tr = min(256, lh)
    kh = jnp.arange(lh, dtype=jnp.int32)
    k = jnp.concatenate([2 * kh, 2 * kh + 1])
    w = 2.0 * np.pi / l
    ang_b = ((jnp.arange(tr, dtype=jnp.int32)[:, None] * k[None, :]) % l).astype(F32) * w
    ang_a = ((jnp.arange(lh // tr, dtype=jnp.int32)[:, None] * tr * k[None, :]) % l).astype(F32) * w
    ca, sa = jnp.cos(ang_a)[:, None, :], jnp.sin(ang_a)[:, None, :]
    base = pl.BlockSpec((tr, l), lambda i: (0, 0))
    rowv = pl.BlockSpec((1, 1, l), lambda i: (i, 0, 0))
    out = pl.BlockSpec((tr, l), lambda i: (i, 0))
    sd = jax.ShapeDtypeStruct((lh, l), BF16)
    return pl.pallas_call(
        _dft_gen_kernel,
        grid=(lh // tr,),
        in_specs=[base, base, rowv, rowv],
        out_specs=[out, out],
        out_shape=[sd, sd],
        compiler_params=_cparams(("parallel",)),
        name="dft_gen",
    )(jnp.cos(ang_b), jnp.sin(ang_b), ca, sa)


def _group_dft_mat():
    n = FNET_GROUP_DIM
    j = np.arange(n)
    ang = 2.0 * np.pi * ((j[:, None] * j[None, :]) % n) / n
    eye = np.eye(FNET_WIDTH // n)
    mat = np.concatenate([np.kron(eye, np.cos(ang)), np.kron(eye, np.sin(ang))], axis=1)
    return jnp.asarray(mat, F32).astype(BF16)


HEADS_PER_TILE = 4
HEAD_TILE = HEADS_PER_TILE * NA_HEAD_DIM
HEAD_TILES = NA_WIDTH // HEAD_TILE


def _head_masks():
    row_head = lax.broadcasted_iota(jnp.int32, (HEAD_TILE, HEAD_TILE), 0) // GRID_W
    lane_head = lax.broadcasted_iota(jnp.int32, (HEAD_TILE, HEAD_TILE), 1) // NA_HEAD_DIM
    return row_head == lane_head, lax.broadcasted_iota(jnp.int32, (GRID_W, HEAD_TILE), 1) // NA_HEAD_DIM


def _stacked_queries(q4, own):
    return jnp.where(own, jnp.concatenate([q4] * HEADS_PER_TILE, axis=0), 0.0).astype(BF16)


def _own_head_blocks(o, out_head):
    o4 = o[0:GRID_W]
    for h in range(1, HEADS_PER_TILE):
        o4 = jnp.where(out_head == h, o[h * GRID_W:(h + 1) * GRID_W], o4)
    return o4


NA_ROWS_PER_STEP = 8


def _na_kernel(q_ref, k_ref, v_ref, kc_ref, vc_ref, gb_ref, bias_ref, o_ref, *, rows, kh, rps):
    nloc = kh * GRID_W
    own, out_head = _head_masks()
    for rr in range(rps):
        r = pl.program_id(1) * rps + rr
        r0 = jnp.clip(r - kh // 2, 0, rows - kh)
        start = pl.multiple_of(r0 * GRID_W, GRID_W)
        qs = slice(rr * GRID_W, (rr + 1) * GRID_W)
        for t in range(HEAD_TILES):
            cs = slice(t * HEAD_TILE, (t + 1) * HEAD_TILE)
            qm = _stacked_queries(q_ref[0, qs, cs], own)
            kl = k_ref[0, pl.ds(start, nloc), cs]
            vl = v_ref[0, pl.ds(start, nloc), cs]
            var = r0 - r + (NA_ROWS - 1)
            bias = jnp.concatenate([bias_ref[t, var + 2 * m_] for m_ in range(kh // 2)], axis=-1)
            s_loc = _dot_nt(qm, kl) + bias
            s_ctx = _dot_nt(qm, kc_ref[0, :, cs])
            m = jnp.maximum(jnp.max(s_loc, axis=-1, keepdims=True), jnp.max(s_ctx, axis=-1, keepdims=True))
            p_loc = jnp.exp2(s_loc - m)
            p_ctx = jnp.exp2(s_ctx - m)
            den = jnp.sum(p_loc, axis=-1, keepdims=True) + jnp.sum(p_ctx, axis=-1, keepdims=True)
            o = (_dot(p_loc.astype(BF16), vl) + _dot(p_ctx.astype(BF16), vc_ref[0, :, cs])) / den
            g = gb_ref[0, qs, cs].astype(F32)
            o_ref[0, qs, cs] = (_own_head_blocks(o, out_head) * _silu(g)).astype(BF16)


def _na_attention(q, k, v, kc, vc, gb, bias):
    b, l, w = q.shape
    lc = kc.shape[1]
    rows = l // GRID_W
    kh = min(NA_ROWS, rows)
    rps = NA_ROWS_PER_STEP

    row = pl.BlockSpec((1, rps * GRID_W, w), lambda bi, i: (bi, i, 0))
    full = lambda n: pl.BlockSpec((1, n, w), lambda bi, i: (bi, 0, 0))
    table = pl.BlockSpec(bias.shape, lambda bi, i: (0, 0, 0, 0), pipeline_mode=pl.Buffered(1))
    kern = functools.partial(_na_kernel, rows=rows, kh=kh, rps=rps)
    return pl.pallas_call(
        kern,
        grid=(b, rows // rps),
        in_specs=[row, full(l), full(l), full(lc), full(lc), row, table],
        out_specs=row,
        out_shape=jax.ShapeDtypeStruct((b, l, w), BF16),
        compiler_params=_cparams(("parallel", "arbitrary")),
        name="na_attention",
    )(q, k, v, kc, vc, gb, bias)


def _na_bias(rpb, kh):
    assert kh % 2 == 0
    cq = np.arange(GRID_W)
    ck = np.arange(GRID_W)
    c0 = np.clip(cq - NA_COLS // 2, 0, GRID_W - NA_COLS)
    col_in = (ck[None, :] >= c0[:, None]) & (ck[None, :] < c0[:, None] + NA_COLS)
    dc = np.clip(ck[None, :] - cq[:, None] + (NA_COLS - 1), 0, 2 * NA_COLS - 2)
    onehot = (dc.reshape(1, -1) == np.arange(2 * NA_COLS - 1)[:, None]).astype(np.float32)
    t = jnp.einsum('hrc,cx->hrx', rpb.astype(F32) * LOG2E, jnp.asarray(onehot), precision=lax.Precision.HIGHEST)
    n_dr = 2 * NA_ROWS - 1
    t = t.reshape(NA_HEADS, n_dr, GRID_W, GRID_W)
    t = jnp.where(col_in[None, None], t, -1e30)
    pairs = jnp.concatenate([t[:, :-1], t[:, 1:]], axis=-1)
    pairs = pairs.reshape(HEAD_TILES, HEADS_PER_TILE, n_dr - 1, GRID_W, 2 * GRID_W).transpose(0, 2, 1, 3, 4)
    return pairs.reshape(HEAD_TILES, n_dr - 1, HEAD_TILE, 2 * GRID_W)


def _ctx_attn_kernel(q_ref, k_ref, v_ref, gb_ref, o_ref):
    own, out_head = _head_masks()
    for t in range(HEAD_TILES):
        cs = slice(t * HEAD_TILE, (t + 1) * HEAD_TILE)
        for q0 in range(0, q_ref.shape[1], GRID_W):
            qs = slice(q0, q0 + GRID_W)
            s = _dot_nt(_stacked_queries(q_ref[0, qs, cs], own), k_ref[0, :, cs])
            p = jnp.exp2(s - jnp.max(s, axis=-1, keepdims=True))
            o = _dot(p.astype(BF16), v_ref[0, :, cs]) / jnp.sum(p, axis=-1, keepdims=True)
            g = gb_ref[0, qs, cs].astype(F32)
            o_ref[0, qs, cs] = (_own_head_blocks(o, out_head) * _silu(g)).astype(BF16)


def _ctx_attention(q, k, v, gb):
    b, lc, w = q.shape
    spec = pl.BlockSpec((1, lc, w), lambda bi: (bi, 0, 0))
    return pl.pallas_call(
        _ctx_attn_kernel,
        grid=(b,),
        in_specs=[spec, spec, spec, spec],
        out_specs=spec,
        out_shape=jax.ShapeDtypeStruct((b, lc, w), BF16),
        compiler_params=_cparams(("parallel",)),
        name="ctx_attention",
    )(q, k, v, gb)


SUPER_K = S5_CHUNK * SUPER_LANES
SG_ROW_TILE = 512
SG_APPLY_VMEM = 60 * 1024 * 1024
REC_LANES = 1024


def _gather_chunks(u_ref, u_scr):
    rows = u_scr.shape[0]
    for s in range(S5_CHUNK):
        u_scr[:, s * SUPER_LANES:(s + 1) * SUPER_LANES] = u_ref[0, pl.ds(s, rows, stride=S5_CHUNK), :].astype(BF16)


def _summary_kernel(u_ref, p_ref, o_ref, w_scr, u_scr):
    @pl.when(pl.program_id(1) == 0)
    def _():
        row_group = (lax.broadcasted_iota(jnp.int32, (SUPER_K, S5_TILE), 0) % SUPER_LANES) // S5_GROUP_DIM
        p = p_ref[0]
        for h in range(S5_SUPER):
            w_scr[:, h * S5_TILE:(h + 1) * S5_TILE] = jnp.where(row_group == h, p, 0.0).astype(BF16)

    _gather_chunks(u_ref, u_scr)
    o_ref[0] = _dot(u_scr[...], w_scr[...])


def _s5_summary(u, p_flat, name):
    k = SUPER_K
    r = u.shape[1] // S5_CHUNK
    tr = min(SG_ROW_TILE, r)
    return pl.pallas_call(
        _summary_kernel,
        grid=(N_SUPER, r // tr),
        in_specs=[pl.BlockSpec((1, tr * S5_CHUNK, SUPER_LANES), lambda j, ri: (j, ri, 0)),
                  pl.BlockSpec((1, k, S5_TILE), lambda j, ri: (j, 0, 0))],
        out_specs=pl.BlockSpec((1, tr, k), lambda j, ri: (j, ri, 0)),
        out_shape=jax.ShapeDtypeStruct((N_SUPER, r, k), F32),
        scratch_shapes=[pltpu.VMEM((k, k), BF16), pltpu.VMEM((tr, k), BF16)],
        compiler_params=_cparams(("parallel", "arbitrary")),
        name=name,
    )(u, p_flat)


def _apply_kernel(u_ref, h_ref, lag_ref, q_ref, sel_ref, o_ref, w_scr, q_scr, u_scr):
    @pl.when(pl.program_id(1) == 0)
    def _():
        for s in range(S5_CHUNK):
            lo = SUPER_LANES * (S5_CHUNK - 1 - s)
            w_scr[s * SUPER_LANES:(s + 1) * SUPER_LANES, :] = lag_ref[0, :, lo:lo + SUPER_K]
        col_group = (lax.broadcasted_iota(jnp.int32, (S5_TILE, SUPER_K), 1) % SUPER_LANES) // S5_GROUP_DIM
        for g in range(S5_SUPER):
            rs = slice(g * S5_TILE, (g + 1) * S5_TILE)
            q_scr[rs, :] = jnp.where(col_group == g, _dot(q_ref[0, rs, :], sel_ref[...]), 0.0).astype(BF16)

    _gather_chunks(u_ref, u_scr)
    acc = _dot(u_scr[...], w_scr[...]) + _dot(h_ref[0].astype(BF16), q_scr[...])
    rows = u_scr.shape[0]
    for s in range(S5_CHUNK):
        o_ref[0, pl.ds(s, rows, stride=S5_CHUNK), :] = acc[:, s * SUPER_LANES:(s + 1) * SUPER_LANES]


def _s5_apply(u, h, lag_blocks, q_flat, name):
    k = SUPER_K
    r = h.shape[1]
    tr = min(SG_ROW_TILE, r)
    once = pl.Buffered(1)
    rows = pl.BlockSpec((1, tr, k), lambda j, ri: (j, ri, 0))
    tok_rows = pl.BlockSpec((1, tr * S5_CHUNK, SUPER_LANES), lambda j, ri: (j, ri, 0))
    ri_, ci_ = np.arange(S5_TILE)[:, None], np.arange(k)[None, :]
    sel = (ri_ // S5_GROUP_DIM == ci_ // SUPER_LANES) & (ri_ % S5_GROUP_DIM == ci_ % S5_GROUP_DIM)
    sel = jnp.asarray(sel, F32).astype(BF16)
    return pl.pallas_call(
        _apply_kernel,
        grid=(N_SUPER, r // tr),
        in_specs=[tok_rows, rows,
                  pl.BlockSpec((1,) + lag_blocks.shape[1:], lambda j, ri: (j, 0, 0), pipeline_mode=once),
                  pl.BlockSpec((1, k, S5_TILE), lambda j, ri: (j, 0, 0), pipeline_mode=once),
                  pl.BlockSpec(sel.shape, lambda j, ri: (0, 0), pipeline_mode=once)],
        out_specs=tok_rows,
        out_shape=jax.ShapeDtypeStruct(u.shape, F32),
        scratch_shapes=[pltpu.VMEM((k, k), BF16), pltpu.VMEM((k, k), BF16), pltpu.VMEM((tr, k), BF16)],
        compiler_params=pltpu.CompilerParams(dimension_semantics=("parallel", "arbitrary"),
                                             vmem_limit_bytes=SG_APPLY_VMEM),
        name=name,
    )(u, h, lag_blocks, q_flat, sel)


def _rec_kernel(sc_ref, sl_ref, a_ref, hc_ref, hl_ref, *, nb, n_ctx, n_lat):
    half = S5_TILE // 2
    npieces = REC_LANES // S5_TILE
    fwd_lane = lax.broadcasted_iota(jnp.int32, (1, half), 1) < S5_STATE
    coef = [(a_ref[0, :, p * S5_TILE:p * S5_TILE + half], a_ref[0, :, p * S5_TILE + half:(p + 1) * S5_TILE])
            for p in range(npieces)]

    def step(s_ref, h_ref, c, carry, forward):
        rows = pl.ds(pl.multiple_of(c * nb, nb), nb)
        out = []
        for p in range(npieces):
            hre, him = carry[2 * p], carry[2 * p + 1]
            are, aim = coef[p]
            re_sl = slice(p * S5_TILE, p * S5_TILE + half)
            im_sl = slice(p * S5_TILE + half, (p + 1) * S5_TILE)
            if forward:
                h_ref[0, rows, re_sl] = jnp.where(fwd_lane, hre, 0.0)
                h_ref[0, rows, im_sl] = jnp.where(fwd_lane, him, 0.0)
            else:
                h_ref[0, rows, re_sl] = jnp.where(fwd_lane, h_ref[0, rows, re_sl], hre)
                h_ref[0, rows, im_sl] = jnp.where(fwd_lane, h_ref[0, rows, im_sl], him)
            sre, sim = s_ref[0, rows, re_sl], s_ref[0, rows, im_sl]
            out.append(are * hre - aim * him + sre)
            out.append(are * him + aim * hre + sim)
        return tuple(out)

    zero = tuple(jnp.zeros((nb, half), F32) for _ in range(2 * npieces))
    carry = lax.fori_loop(0, n_ctx, lambda c, cr: step(sc_ref, hc_ref, c, cr, True), zero)
    lax.fori_loop(0, n_lat, lambda c, cr: step(sl_ref, hl_ref, c, cr, True), carry)
    carry = lax.fori_loop(0, n_ctx, lambda i, cr: step(sc_ref, hc_ref, n_ctx - 1 - i, cr, False), zero)
    lax.fori_loop(0, n_lat, lambda i, cr: step(sl_ref, hl_ref, n_lat - 1 - i, cr, False), carry)


def _chunk_recurrence(s_c, s_l, a16, nb):
    _, rc, lanes = s_c.shape
    rl = s_l.shape[1]
    spec = lambda r: pl.BlockSpec((1, r, REC_LANES), lambda j, i: (j, 0, i))
    return pl.pallas_call(
        functools.partial(_rec_kernel, nb=nb, n_ctx=rc // nb, n_lat=rl // nb),
        grid=(N_SUPER, lanes // REC_LANES),
        in_specs=[spec(rc), spec(rl), spec(1)],
        out_specs=[spec(rc), spec(rl)],
        out_shape=[jax.ShapeDtypeStruct(s_c.shape, F32), jax.ShapeDtypeStruct(s_l.shape, F32)],
        compiler_params=_cparams(("parallel", "parallel")),
        name="s5_recurrence",
    )(s_c, s_l, a16)


def _cmul(ar, ai, br, bi):
    return ar * br - ai * bi, ar * bi + ai * br


def _s5_matrices(lam_re, lam_im, log_step, b_re, b_im, c_re, c_im):
    t = S5_CHUNK
    hp = lax.Precision.HIGHEST
    taus = jnp.arange(t + 1, dtype=F32)[:, None, None]
    er, ei, wr, wi, a_re, a_im, ks = [], [], [], [], [], [], []
    for d in range(2):
        lr = jnp.minimum(lam_re[d].astype(F32), -1e-4)
        li = lam_im[d].astype(F32)
        dt = jnp.exp(log_step[d].astype(F32))[:, None]
        mag = jnp.exp(lr * dt * taus)
        pr, pi = mag * jnp.cos(li * dt * taus), mag * jnp.sin(li * dt * taus)
        den = lr * lr + li * li
        qr = ((pr[1] - 1.0) * lr + pi[1] * li) / den
        qi = (pi[1] * lr - (pr[1] - 1.0) * li) / den
        bbr, bbi = _cmul(qr[..., None], qi[..., None], b_re[d].astype(F32), b_im[d].astype(F32))
        e_r, e_i = _cmul(pr[..., None], pi[..., None], bbr[None], bbi[None])
        cr, ci = c_re[d].astype(F32), c_im[d].astype(F32)
        w_r, w_i = _cmul(cr[None], ci[None], pr[:, :, None, :], pi[:, :, None, :])
        k = (jnp.einsum('gmp,tgpn->tgmn', cr, e_r[:t], precision=hp)
             - jnp.einsum('gmp,tgpn->tgmn', ci, e_i[:t], precision=hp))
        er.append(e_r); ei.append(e_i); wr.append(w_r); wi.append(w_i); ks.append(k)
        a_re.append(pr[t]); a_im.append(pi[t])
    rows = lambda e: e.transpose(1, 0, 3, 2)
    p_mat = jnp.concatenate([rows(er[0][:t][::-1]), rows(er[1][:t]),
                             rows(ei[0][:t][::-1]), rows(ei[1][:t])], axis=-1)
    cols = lambda w: w.transpose(1, 3, 0, 2)
    q_mat = jnp.concatenate([cols(wr[0][1:]), cols(wr[1][1:][::-1]),
                             -cols(wi[0][1:]), -cols(wi[1][1:][::-1])], axis=1)
    a16 = jnp.concatenate([a_re[0], a_re[1], a_im[0], a_im[1]], axis=-1)

    gd, sl = S5_GROUP_DIM, SUPER_LANES

    def spread(a, nblk):
        r = jnp.arange(nblk * gd)[:, None]
        c = jnp.arange(nblk * sl)[None, :]
        sel = ((r // gd == c // sl) & (r % gd == c % gd)).astype(BF16)
        return jnp.einsum('jrk,kc->jrc', a.astype(BF16), sel, preferred_element_type=F32).astype(BF16)

    def same_group(row_group, col_group):
        return (row_group[:, None] == col_group[None, :]).astype(BF16)[None]

    nlag = 2 * t - 1
    kf = ks[0].transpose(0, 1, 3, 2)
    kb = ks[1].transpose(0, 1, 3, 2)
    klag = jnp.concatenate([kb[1:][::-1], (kf[0] + kb[0])[None], kf[1:]], axis=0)
    k_flat = klag.transpose(1, 2, 0, 3).reshape(N_SUPER, sl, nlag * gd)
    lag_blocks = spread(k_flat, nlag) * same_group(jnp.arange(sl) // gd, (jnp.arange(nlag * sl) % sl) // gd)
    p_flat = p_mat.reshape(N_SUPER, S5_SUPER, t, gd, S5_TILE).transpose(0, 2, 1, 3, 4).reshape(N_SUPER, SUPER_K, S5_TILE)
    q_flat = q_mat.reshape(N_SUPER, S5_SUPER * S5_TILE, S5_TILE)
    return lag_blocks, p_flat.astype(BF16), q_flat.astype(BF16), a16.reshape(N_SUPER, 1, SUPER_K)


def _s5_scan(s_l, s_c, mats):
    lag_blocks, p_flat, q_flat, a16 = mats
    b = s_l.shape[2]
    u_l = s_l.reshape(N_SUPER, -1, SUPER_LANES)
    u_c = s_c.reshape(N_SUPER, -1, SUPER_LANES)
    sum_l = _s5_summary(u_l, p_flat, "s5_summary")
    sum_c = _s5_summary(u_c, p_flat, "s5_summary_ctx")
    h_c, h_l = _chunk_recurrence(sum_c, sum_l, a16, b)
    y_l = _s5_apply(u_l, h_l, lag_blocks, q_flat, "s5_apply")
    y_c = _s5_apply(u_c, h_c, lag_blocks, q_flat, "s5_apply_ctx")
    return y_l.reshape(s_l.shape), y_c.reshape(s_c.shape)


def kernel(x, c, ctx, c_ctx, w_ada, b_ada, pre_g, post_g, w_in_even, w_out_even, na_rpb,
           w_in_odd, w_out_odd, sgu_w, sgu_b, sgu_g, s5_lam_re, s5_lam_im, s5_log_step,
           s5_b_re, s5_b_im, s5_c_re, s5_c_im, s5_d, glu_w, glu_b):
    b, l, d = x.shape
    lc = ctx.shape[1]
    depth = w_ada.shape[0]
    rows = l // GRID_W
    kh = min(NA_ROWS, rows)

    n_rows = -(-(b + 1) // 8) * 8
    cond = jnp.zeros((n_rows, d), F32).at[:b].set(c).at[b].set(c_ctx)
    mod = _ada_mod(cond, w_ada, b_ada)

    bd = _group_dft_mat()
    cm_l, sm_l = _dft_mats(l)
    cm_c, sm_c = _dft_mats(lc)

    w_in_e, w_in_o = w_in_even.astype(BF16), w_in_odd.astype(BF16)
    w_out_e, w_out_o = w_out_even.astype(BF16), w_out_odd.astype(BF16)
    sgu_ws, glu_ws = sgu_w.astype(BF16), glu_w.astype(BF16)
    n_odd = w_in_odd.shape[0]
    gw = SGU_WIDTH // SGU_GROUPS
    sgu_bs = [jnp.repeat(sgu_b[j].astype(F32).T, gw, axis=1) for j in range(n_odd)]
    tm_l, tm_c = min(ROW_TILE, l), min(ROW_TILE, lc)

    def modulation(i):
        lat = tuple(mod[i, :b, k * d:(k + 1) * d][:, None, :] for k in range(3))
        ctx_mod = jnp.broadcast_to(mod[i, b][None, None, :], (b, 1, 3 * d))
        return lat, tuple(ctx_mod[..., k * d:(k + 1) * d] for k in range(3))

    def in_part(i, seq, shift, scale):
        tm = min(ROW_TILE, seq)
        if i % 2 == 0:
            return ("even",) + _even_in_io(b, seq, d, tm, pre_g[i], scale, shift, w_in_e, i // 2, bd)
        j = i // 2
        return ("odd",) + _odd_in_io(b, seq, d, tm, pre_g[i], scale, shift, w_in_o, sgu_ws, j, sgu_bs[j], sgu_g[j])

    (shift, scale, gate), (shift_c, scale_c, gate_c) = modulation(0)
    xl, xc = x, ctx
    res_l = _projection(xl, None, in_part(0, l, shift, scale), "proj_in")
    res_c = _projection(xc, None, in_part(0, lc, shift_c, scale_c), "proj_in_ctx")
    for i in range(depth):
        last = i == depth - 1
        j = i // 2
        if i % 2 == 0:
            zc, zs, ga, q, k_, v, gb = res_l
            zc_c, zs_c, ga_c, q_c, k_c, v_c, gb_c = res_c
            a_l = _dft_mix(cm_l, sm_l, zc, zs, ga)
            n_l = _na_attention(q, k_, v, k_c, v_c, gb, _na_bias(na_rpb[j], kh))
            out_l = ("even",) + _even_out_io(d, tm_l, a_l, n_l, w_out_e, j, post_g[i], gate)
            if not last:
                a_c = _dft_mix(cm_c, sm_c, zc_c, zs_c, ga_c)
                n_c = _ctx_attention(q_c, k_c, v_c, gb_c)
                out_c = ("even",) + _even_out_io(d, tm_c, a_c, n_c, w_out_e, j, post_g[i], gate_c)
        else:
            mats = _s5_matrices(s5_lam_re[j], s5_lam_im[j], s5_log_step[j], s5_b_re[j], s5_b_im[j],
                                s5_c_re[j], s5_c_im[j])
            sg_l, s_l, gd_l = res_l
            sg_c, s_c, gd_c = res_c
            y_l, y_c = _s5_scan(s_l, s_c, mats)
            out_l = ("odd",) + _odd_out_io(d, tm_l, sg_l, y_l, s_l, gd_l, s5_d[j], glu_ws, glu_b[j], w_out_o, j,
                                           post_g[i], gate)
            if not last:
                out_c = ("odd",) + _odd_out_io(d, tm_c, sg_c, y_c, s_c, gd_c, s5_d[j], glu_ws, glu_b[j], w_out_o, j,
                                               post_g[i], gate_c)
        if last:
            (xl,) = _projection(xl, out_l, None, "proj_out")
        else:
            (shift, scale, gate), (shift_c, scale_c, gate_c) = modulation(i + 1)
            xl, *res_l = _projection(xl, out_l, in_part(i + 1, l, shift, scale), "proj_out_in")
            xc, *res_c = _projection(xc, out_c, in_part(i + 1, lc, shift_c, scale_c), "proj_out_in_ctx")
    return xl
```

```python
import functools

import numpy as np
import jax
import jax.numpy as jnp
from jax import lax
from jax.experimental import pallas as pl
from jax.experimental.pallas import tpu as pltpu

F32 = jnp.float32
BF16 = jnp.bfloat16

EPS = 1e-6
GRID_W = 64
FNET_WIDTH = 256
FNET_GROUP_DIM = 64
NA_WIDTH = 768
NA_HEAD_DIM = 64
NA_HEADS = 12
NA_ROWS = 8
NA_COLS = 16
SGU_CHUNK = 128
SGU_WIDTH = 512
SGU_GROUPS = 4
S5_WIDTH = 512
S5_GROUP_DIM = 16
S5_GROUPS = 32
S5_STATE = 64
S5_CHUNK = 16
S5_TILE = S5_CHUNK * S5_GROUP_DIM
LOG2E = float(np.log2(np.e))
QK_SCALE = NA_HEAD_DIM ** -0.5 * LOG2E

V7X_VMEM_BYTES = 64 * 1024 * 1024
VMEM_LIMIT = 48 * 1024 * 1024
ROW_TILE = 512


def _cparams(sem):
    return pltpu.CompilerParams(dimension_semantics=sem, vmem_limit_bytes=VMEM_LIMIT)


def _silu(x):
    return x * jax.nn.sigmoid(x)


def _dot(a, b):
    return jnp.dot(a, b, preferred_element_type=F32)


def _dot_nt(a, b):
    return lax.dot_general(a, b, (((1,), (1,)), ((), ())), preferred_element_type=F32)


def _ada_kernel(c_ref, w_ref, b_ref, o_ref):
    c = c_ref[...]
    o_ref[0] = jnp.dot(_silu(c), w_ref[0], preferred_element_type=F32,
                       precision=lax.Precision.HIGHEST) + b_ref[0]


def _ada_mod(cond, w_ada, b_ada):
    depth, d, n = w_ada.shape
    rows = cond.shape[0]
    tn = 1024
    return pl.pallas_call(
        _ada_kernel,
        grid=(depth, n // tn),
        in_specs=[pl.BlockSpec((rows, d), lambda i, j: (0, 0)),
                  pl.BlockSpec((1, d, tn), lambda i, j: (i, 0, j)),
                  pl.BlockSpec((1, 1, tn), lambda i, j: (i, 0, j))],
        out_specs=pl.BlockSpec((1, rows, tn), lambda i, j: (i, 0, j)),
        out_shape=jax.ShapeDtypeStruct((depth, rows, n), F32),
        compiler_params=_cparams(("parallel", "parallel")),
        name="ada_mod",
    )(cond, w_ada, b_ada.reshape(depth, 1, n))


def _norm_mod(x, g, scale, shift):
    y = x * lax.rsqrt(jnp.mean(x * x, axis=-1, keepdims=True) + EPS)
    return (y * g) * (1.0 + scale) + shift


def _post(y, post_g, gate, x):
    yn = y * lax.rsqrt(jnp.mean(y * y, axis=-1, keepdims=True) + EPS)
    return x + gate * (yn * post_g)


def _vec_spec(d):
    return pl.BlockSpec((1, 1, d), lambda bi, i: (bi, 0, 0))


def _const_spec(a):
    return pl.BlockSpec(a.shape, lambda bi, i: (0,) * a.ndim)


def _layer_spec(a, j):
    return pl.BlockSpec((1,) + a.shape[1:], lambda bi, i: (j,) + (0,) * (a.ndim - 1))


def _tok_spec(tm, width):
    return pl.BlockSpec((1, tm, width), lambda bi, i: (bi, i, 0))


def _even_in_body(x, g_ref, sc, sh, w_ref, bd_ref, z_scr, outs):
    zc_ref, zs_ref, ga_ref, q_ref, k_ref, v_ref, gb_ref = outs
    h = _norm_mod(x, g_ref[...], sc, sh).astype(BF16)
    fa = _dot(h, w_ref[0, :, 0:FNET_WIDTH]).astype(BF16)
    z = _dot(fa, bd_ref[...])
    planes = z.shape[1] // 128
    for p in range(planes):
        z_scr[p] = z[:, p * 128:(p + 1) * 128]
    half = z.shape[0] // 2
    for par in range(2):
        for p in range(planes):
            piece = z_scr[p, pl.ds(par, half, stride=2), :].astype(BF16)
            ref, q0 = (zc_ref, p) if p < planes // 2 else (zs_ref, p - planes // 2)
            ref[par, :, q0 * 128:(q0 + 1) * 128] = piece
    o = FNET_WIDTH
    ga_ref[0] = _dot(h, w_ref[0, :, o:o + FNET_WIDTH]).astype(BF16)
    o += FNET_WIDTH
    q_ref[0] = (_dot(h, w_ref[0, :, o:o + NA_WIDTH]) * QK_SCALE).astype(BF16)
    o += NA_WIDTH
    k_ref[0] = _dot(h, w_ref[0, :, o:o + NA_WIDTH]).astype(BF16)
    o += NA_WIDTH
    v_ref[0] = _dot(h, w_ref[0, :, o:o + NA_WIDTH]).astype(BF16)
    o += NA_WIDTH
    gb_ref[0] = _dot(h, w_ref[0, :, o:o + NA_WIDTH]).astype(BF16)


def _even_in_io(b, l, d, tm, pre_g, scale, shift, w, j, bd):
    zspec = pl.BlockSpec((2, tm // 2, FNET_WIDTH), lambda bi, i: (0, i, bi))
    zsd = jax.ShapeDtypeStruct((2, l // 2, b * FNET_WIDTH), BF16)
    sd = lambda width: jax.ShapeDtypeStruct((b, l, width), BF16)
    g2 = pre_g.reshape(1, d)
    widths = (FNET_WIDTH, NA_WIDTH, NA_WIDTH, NA_WIDTH, NA_WIDTH)
    return ([g2, scale, shift, w, bd],
            [_const_spec(g2), _vec_spec(d), _vec_spec(d), _layer_spec(w, j), _const_spec(bd)],
            [zspec, zspec] + [_tok_spec(tm, wd) for wd in widths],
            [zsd, zsd] + [sd(wd) for wd in widths],
            [pltpu.VMEM((2 * FNET_WIDTH // 128, tm, 128), F32)])


def _even_out_body(a_ref, n_ref, w_ref, pg_ref, gate, x):
    y = _dot(a_ref[0], w_ref[0, :FNET_WIDTH, :]) + _dot(n_ref[0], w_ref[0, FNET_WIDTH:, :])
    return _post(y, pg_ref[...], gate, x)


def _even_out_io(d, tm, a, n, w, j, post_g, gate):
    pg2 = post_g.reshape(1, d)
    return ([a, n, w, pg2, gate],
            [_tok_spec(tm, FNET_WIDTH), _tok_spec(tm, NA_WIDTH), _layer_spec(w, j), _const_spec(pg2), _vec_spec(d)])


S5_SUPER = 8
N_SUPER = S5_GROUPS // S5_SUPER
SUPER_LANES = S5_SUPER * S5_GROUP_DIM


def _super_spec(tm):
    return pl.BlockSpec((N_SUPER, tm // S5_CHUNK, 1, S5_CHUNK, SUPER_LANES), lambda bi, i: (0, i, bi, 0, 0))


def _super_shape(b, l):
    return jax.ShapeDtypeStruct((N_SUPER, l // S5_CHUNK, b, S5_CHUNK, SUPER_LANES), F32)


def _odd_in_body(x, g_ref, sc, sh, w_ref, ws_ref, bs_ref, sg_g_ref, outs):
    sg_ref, s_ref, gd_ref = outs
    tm = x.shape[0]
    h = _norm_mod(x, g_ref[...], sc, sh).astype(BF16)
    wd = SGU_WIDTH
    u = _dot(h, w_ref[0, :, 0:wd])
    v = _dot(h, w_ref[0, :, wd:2 * wd])
    gc = _dot(h, w_ref[0, :, 2 * wd:3 * wd])
    s = _dot(h, w_ref[0, :, 3 * wd:3 * wd + S5_WIDTH])
    for j in range(N_SUPER):
        sj = s[:, j * SUPER_LANES:(j + 1) * SUPER_LANES]
        s_ref[j, :, 0] = sj.reshape(tm // S5_CHUNK, S5_CHUNK, SUPER_LANES)
    gd_ref[0] = _dot(h, w_ref[0, :, 3 * wd + S5_WIDTH:]).astype(BF16)
    vc = v - jnp.mean(v, axis=-1, keepdims=True)
    vn = (vc * lax.rsqrt(jnp.mean(vc * vc, axis=-1, keepdims=True) + EPS) * sg_g_ref[...]).astype(BF16)
    gate = u * _silu(gc)
    gw = SGU_WIDTH // SGU_GROUPS
    for j in range(tm // SGU_CHUNK):
        rs = slice(j * SGU_CHUNK, (j + 1) * SGU_CHUNK)
        for g in range(SGU_GROUPS):
            cs = slice(g * gw, (g + 1) * gw)
            mixed = _dot(ws_ref[0, g], vn[rs, cs]) + bs_ref[:, cs]
            sg_ref[0, rs, cs] = (gate[rs, cs] * mixed).astype(BF16)


def _odd_in_io(b, l, d, tm, pre_g, scale, shift, w, ws, j, bs_full, sgu_g):
    g2 = pre_g.reshape(1, d)
    sg2 = sgu_g.reshape(1, SGU_WIDTH)
    sd = jax.ShapeDtypeStruct((b, l, SGU_WIDTH), BF16)
    return ([g2, scale, shift, w, ws, bs_full, sg2],
            [_const_spec(g2), _vec_spec(d), _vec_spec(d), _layer_spec(w, j), _layer_spec(ws, j), _const_spec(bs_full),
             _const_spec(sg2)],
            [_tok_spec(tm, SGU_WIDTH), _super_spec(tm), _tok_spec(tm, S5_WIDTH)],
            [sd, _super_shape(b, l), sd],
            [])


def _gelu_tanh(x):
    return 0.5 * x * (1.0 + jnp.tanh(float(np.sqrt(2.0 / np.pi)) * (x + 0.044715 * (x * x * x))))


def _odd_out_body(sg_ref, y_ref, s_ref, gd_ref, dsk_ref, wg_ref, bg_ref, w_ref, pg_ref, gate, x):
    tm = x.shape[0]
    tokens = lambda ref: jnp.concatenate([ref[j, :, 0].reshape(tm, SUPER_LANES) for j in range(N_SUPER)], axis=-1)
    ys = tokens(y_ref)
    s = tokens(s_ref)
    z = _gelu_tanh(ys + dsk_ref[...] * s)
    zz = _dot(z.astype(BF16), wg_ref[0]) + bg_ref[...]
    gd = gd_ref[0].astype(F32)
    ss = zz[:, :S5_WIDTH] * jax.nn.sigmoid(zz[:, S5_WIDTH:]) * _silu(gd)
    y = _dot(sg_ref[0], w_ref[0, :SGU_WIDTH, :]) + _dot(ss.astype(BF16), w_ref[0, SGU_WIDTH:, :])
    return _post(y, pg_ref[...], gate, x)


def _odd_out_io(d, tm, sg, y, s, gd, d_skip, wg, bg, w, j, post_g, gate):
    dsk = d_skip.reshape(1, S5_WIDTH).astype(F32)
    bg2 = bg.reshape(1, 2 * S5_WIDTH).astype(F32)
    pg2 = post_g.reshape(1, d)
    return ([sg, y, s, gd, dsk, wg, bg2, w, pg2, gate],
            [_tok_spec(tm, SGU_WIDTH), _super_spec(tm), _super_spec(tm), _tok_spec(tm, S5_WIDTH),
             _const_spec(dsk), _layer_spec(wg, j), _const_spec(bg2), _layer_spec(w, j), _const_spec(pg2), _vec_spec(d)])


_OUT_BODY = {"even": (_even_out_body, 5), "odd": (_odd_out_body, 10)}
_IN_BODY = {"even": (_even_in_body, 5, 7), "odd": (_odd_in_body, 7, 3)}


def _proj_kernel(*refs, out_kind, in_kind):
    pos = 0
    if out_kind is not None:
        body, n_ops = _OUT_BODY[out_kind]
        ops = refs[pos:pos + n_ops]
        pos += n_ops
    x_ref = refs[pos]
    pos += 1
    if in_kind is not None:
        in_body, n_in, n_out = _IN_BODY[in_kind]
        in_ops = refs[pos:pos + n_in]
        pos += n_in
    x = x_ref[0]
    if out_kind is not None:
        x = body(*ops[:-1], ops[-1][0], x)
        refs[pos][0] = x
        pos += 1
    if in_kind is not None:
        outs = refs[pos:pos + n_out]
        scratch = refs[pos + n_out:]
        g_ref, sc_ref, sh_ref = in_ops[:3]
        in_body(x, g_ref, sc_ref[0], sh_ref[0], *in_ops[3:], *scratch, outs)


def _projection(x, out_part, in_part, name):
    b, l, d = x.shape
    tm = min(ROW_TILE, l)
    operands, in_specs, out_specs, out_shapes, scratch = [], [], [], [], []
    if out_part is not None:
        operands += out_part[1]
        in_specs += out_part[2]
    operands.append(x)
    in_specs.append(_tok_spec(tm, d))
    if in_part is not None:
        operands += in_part[1]
        in_specs += in_part[2]
    if out_part is not None:
        out_specs.append(_tok_spec(tm, d))
        out_shapes.append(jax.ShapeDtypeStruct((b, l, d), F32))
    if in_part is not None:
        out_specs += in_part[3]
        out_shapes += in_part[4]
        scratch = in_part[5]
    kern = functools.partial(_proj_kernel, out_kind=out_part and out_part[0], in_kind=in_part and in_part[0])
    return pl.pallas_call(
        kern,
        grid=(b, l // tm),
        in_specs=in_specs,
        out_specs=out_specs,
        out_shape=out_shapes,
        scratch_shapes=scratch,
        compiler_params=_cparams(("parallel", "parallel")),
        name=name,
    )(*operands)


def _dft_kernel(ce_ref, se_ref, co_ref, so_ref, zc_ref, zs_ref, ga_ref, o_ref, acc_e, acc_o, *, nb, norm):
    kk = pl.program_id(1)

    @pl.when(kk == 0)
    def _():
        acc_e[...] = jnp.zeros_like(acc_e)
        acc_o[...] = jnp.zeros_like(acc_o)

    acc_e[...] += _dot(ce_ref[...], zc_ref[0]) + _dot(se_ref[...], zs_ref[0])
    acc_o[...] += _dot(co_ref[...], zc_ref[1]) + _dot(so_ref[...], zs_ref[1])

    @pl.when(kk == pl.num_programs(1) - 1)
    def _():
        for bi in range(nb):
            cs = slice(bi * FNET_WIDTH, (bi + 1) * FNET_WIDTH)
            e, o = acc_e[:, cs], acc_o[:, cs]
            for half, y in enumerate((e + o, e - o)):
                g = ga_ref[bi, half].astype(F32)
                o_ref[bi, half] = (y * norm * _silu(g)).astype(BF16)


def _dft_mix(cm, sm, zc, zs, ga):
    b, l, _ = ga.shape
    lh = l // 2
    tm = min(512, lh)
    tk = min(256, lh)
    nk = lh // tk
    nc = b * FNET_WIDTH
    kern = functools.partial(_dft_kernel, nb=b, norm=float((l * FNET_GROUP_DIM) ** -0.5))
    mat_e = pl.BlockSpec((tm, tk), lambda i, k: (i, k))
    mat_o = pl.BlockSpec((tm, tk), lambda i, k: (i, nk + k))
    zspec = pl.BlockSpec((2, tk, nc), lambda i, k: (0, k, 0))
    halves = pl.BlockSpec((b, 2, tm, FNET_WIDTH), lambda i, k: (0, 0, i, 0))
    out = pl.pallas_call(
        kern,
        grid=(lh // tm, nk),
        in_specs=[mat_e, mat_e, mat_o, mat_o, zspec, zspec, halves],
        out_specs=halves,
        out_shape=jax.ShapeDtypeStruct((b, 2, lh, FNET_WIDTH), BF16),
        scratch_shapes=[pltpu.VMEM((tm, nc), F32), pltpu.VMEM((tm, nc), F32)],
        compiler_params=_cparams(("parallel", "arbitrary")),
        name="dft_mix",
    )(cm, sm, cm, sm, zc, zs, ga.reshape(b, 2, lh, FNET_WIDTH))
    return out.reshape(b, l, FNET_WIDTH)


def _dft_gen_kernel(cb_ref, sb_ref, ca_ref, sa_ref, c_ref, s_ref):
    ca, sa = ca_ref[0], sa_ref[0]
    cb, sb = cb_ref[...], sb_ref[...]
    c_ref[...] = (ca * cb - sa * sb).astype(BF16)
    s_ref[...] = (-(sa * cb + ca * sb)).astype(BF16)


def _dft_mats(l):
    lh = l // 2
    tr = min(256, lh)
    kh = jnp.arange(lh, dtype=jnp.int32)
    k = jnp.concatenate([2 * kh, 2 * kh + 1])
    w = 2.0 * np.pi / l
    ang_b = ((jnp.arange(tr, dtype=jnp.int32)[:, None] * k[None, :]) % l).astype(F32) * w
    ang_a = ((jnp.arange(lh // tr, dtype=jnp.int32)[:, None] * tr * k[None, :]) % l).astype(F32) * w
    ca, sa = jnp.cos(ang_a)[:, None, :], jnp.sin(ang_a)[:, None, :]
    base = pl.BlockSpec((tr, l), lambda i: (0, 0))
    rowv = pl.BlockSpec((1, 1, l), lambda i: (i, 0, 0))
    out = pl.BlockSpec((tr, l), lambda i: (i, 0))
    sd = jax.ShapeDtypeStruct((lh, l), BF16)
    return pl.pallas_call(
        _dft_gen_kernel,
        grid=(lh // tr,),
        in_specs=[base, base, rowv, rowv],
        out_specs=[out, out],
        out_shape=[sd, sd],
        compiler_params=_cparams(("parallel",)),
        name="dft_gen",
    )(jnp.cos(ang_b), jnp.sin(ang_b), ca, sa)


def _group_dft_mat():
    n = FNET_GROUP_DIM
    j = np.arange(n)
    ang = 2.0 * np.pi * ((j[:, None] * j[None, :]) % n) / n
    eye = np.eye(FNET_WIDTH // n)
    mat = np.concatenate([np.kron(eye, np.cos(ang)), np.kron(eye, np.sin(ang))], axis=1)
    return jnp.asarray(mat, F32).astype(BF16)


HEADS_PER_TILE = 4
HEAD_TILE = HEADS_PER_TILE * NA_HEAD_DIM
HEAD_TILES = NA_WIDTH // HEAD_TILE


def _head_masks():
    row_head = lax.broadcasted_iota(jnp.int32, (HEAD_TILE, HEAD_TILE), 0) // GRID_W
    lane_head = lax.broadcasted_iota(jnp.int32, (HEAD_TILE, HEAD_TILE), 1) // NA_HEAD_DIM
    return row_head == lane_head, lax.broadcasted_iota(jnp.int32, (GRID_W, HEAD_TILE), 1) // NA_HEAD_DIM


def _stacked_queries(q4, own):
    return jnp.where(own, jnp.concatenate([q4] * HEADS_PER_TILE, axis=0), 0.0).astype(BF16)


def _own_head_blocks(o, out_head):
    o4 = o[0:GRID_W]
    for h in range(1, HEADS_PER_TILE):
        o4 = jnp.where(out_head == h, o[h * GRID_W:(h + 1) * GRID_W], o4)
    return o4


NA_ROWS_PER_STEP = 8


def _na_kernel(q_ref, k_ref, v_ref, kc_ref, vc_ref, gb_ref, bias_ref, o_ref, *, rows, kh, rps):
    nloc = kh * GRID_W
    own, out_head = _head_masks()
    for rr in range(rps):
        r = pl.program_id(1) * rps + rr
        r0 = jnp.clip(r - kh // 2, 0, rows - kh)
        start = pl.multiple_of(r0 * GRID_W, GRID_W)
        qs = slice(rr * GRID_W, (rr + 1) * GRID_W)
        for t in range(HEAD_TILES):
            cs = slice(t * HEAD_TILE, (t + 1) * HEAD_TILE)
            qm = _stacked_queries(q_ref[0, qs, cs], own)
            kl = k_ref[0, pl.ds(start, nloc), cs]
            vl = v_ref[0, pl.ds(start, nloc), cs]
            var = r0 - r + (NA_ROWS - 1)
            bias = jnp.concatenate([bias_ref[0, t, var + 2 * m_] for m_ in range(kh // 2)], axis=-1)
            s_loc = _dot_nt(qm, kl) + bias
            s_ctx = _dot_nt(qm, kc_ref[0, :, cs])
            m = jnp.maximum(jnp.max(s_loc, axis=-1, keepdims=True), jnp.max(s_ctx, axis=-1, keepdims=True))
            p_loc = jnp.exp2(s_loc - m)
            p_ctx = jnp.exp2(s_ctx - m)
            den = jnp.sum(p_loc, axis=-1, keepdims=True) + jnp.sum(p_ctx, axis=-1, keepdims=True)
            o = (_dot(p_loc.astype(BF16), vl) + _dot(p_ctx.astype(BF16), vc_ref[0, :, cs])) / den
            g = gb_ref[0, qs, cs].astype(F32)
            o_ref[0, qs, cs] = (_own_head_blocks(o, out_head) * _silu(g)).astype(BF16)


def _na_attention(q, k, v, kc, vc, gb, bias, layer):
    b, l, w = q.shape
    lc = kc.shape[1]
    rows = l // GRID_W
    kh = min(NA_ROWS, rows)
    rps = NA_ROWS_PER_STEP

    row = pl.BlockSpec((1, rps * GRID_W, w), lambda bi, i: (bi, i, 0))
    full = lambda n: pl.BlockSpec((1, n, w), lambda bi, i: (bi, 0, 0))
    table = pl.BlockSpec((1,) + bias.shape[1:], lambda bi, i: (layer, 0, 0, 0, 0), pipeline_mode=pl.Buffered(1))
    kern = functools.partial(_na_kernel, rows=rows, kh=kh, rps=rps)
    return pl.pallas_call(
        kern,
        grid=(b, rows // rps),
        in_specs=[row, full(l), full(l), full(lc), full(lc), row, table],
        out_specs=row,
        out_shape=jax.ShapeDtypeStruct((b, l, w), BF16),
        compiler_params=_cparams(("parallel", "arbitrary")),
        name="na_attention",
    )(q, k, v, kc, vc, gb, bias)


def _na_bias(rpb, kh):
    assert kh % 2 == 0
    cq = np.arange(GRID_W)
    ck = np.arange(GRID_W)
    c0 = np.clip(cq - NA_COLS // 2, 0, GRID_W - NA_COLS)
    col_in = (ck[None, :] >= c0[:, None]) & (ck[None, :] < c0[:, None] + NA_COLS)
    dc = np.clip(ck[None, :] - cq[:, None] + (NA_COLS - 1), 0, 2 * NA_COLS - 2)
    onehot = (dc.reshape(1, -1) == np.arange(2 * NA_COLS - 1)[:, None]).astype(np.float32)
    t = jnp.einsum('hrc,cx->hrx', rpb.astype(F32) * LOG2E, jnp.asarray(onehot), precision=lax.Precision.HIGHEST)
    n_dr = 2 * NA_ROWS - 1
    t = t.reshape(NA_HEADS, n_dr, GRID_W, GRID_W)
    t = jnp.where(col_in[None, None], t, -1e30)
    pairs = jnp.concatenate([t[:, :-1], t[:, 1:]], axis=-1)
    pairs = pairs.reshape(HEAD_TILES, HEADS_PER_TILE, n_dr - 1, GRID_W, 2 * GRID_W).transpose(0, 2, 1, 3, 4)
    return pairs.reshape(HEAD_TILES, n_dr - 1, HEAD_TILE, 2 * GRID_W)


def _ctx_attn_kernel(q_ref, k_ref, v_ref, gb_ref, o_ref):
    own, out_head = _head_masks()
    for t in range(HEAD_TILES):
        cs = slice(t * HEAD_TILE, (t + 1) * HEAD_TILE)
        for q0 in range(0, q_ref.shape[1], GRID_W):
            qs = slice(q0, q0 + GRID_W)
            s = _dot_nt(_stacked_queries(q_ref[0, qs, cs], own), k_ref[0, :, cs])
            p = jnp.exp2(s - jnp.max(s, axis=-1, keepdims=True))
            o = _dot(p.astype(BF16), v_ref[0, :, cs]) / jnp.sum(p, axis=-1, keepdims=True)
            g = gb_ref[0, qs, cs].astype(F32)
            o_ref[0, qs, cs] = (_own_head_blocks(o, out_head) * _silu(g)).astype(BF16)


def _ctx_attention(q, k, v, gb):
    b, lc, w = q.shape
    spec = pl.BlockSpec((1, lc, w), lambda bi: (bi, 0, 0))
    return pl.pallas_call(
        _ctx_attn_kernel,
        grid=(b,),
        in_specs=[spec, spec, spec, spec],
        out_specs=spec,
        out_shape=jax.ShapeDtypeStruct((b, lc, w), BF16),
        compiler_params=_cparams(("parallel",)),
        name="ctx_attention",
    )(q, k, v, gb)


SUPER_K = S5_CHUNK * SUPER_LANES
SG_ROW_TILE = 512
SG_APPLY_ROWS = 256
REC_LANES = 1024


def _gather_chunks(u_ref, u_scr):
    rows = u_scr.shape[0]
    for s in range(S5_CHUNK):
        u_scr[:, s * SUPER_LANES:(s + 1) * SUPER_LANES] = u_ref[0, pl.ds(s, rows, stride=S5_CHUNK), :].astype(BF16)


def _summary_kernel(u_ref, p_ref, o_ref, w_scr, u_scr):
    @pl.when(pl.program_id(1) == 0)
    def _():
        row_group = (lax.broadcasted_iota(jnp.int32, (SUPER_K, S5_TILE), 0) % SUPER_LANES) // S5_GROUP_DIM
        p = p_ref[0, 0]
        for h in range(S5_SUPER):
            w_scr[:, h * S5_TILE:(h + 1) * S5_TILE] = jnp.where(row_group == h, p, 0.0).astype(BF16)

    _gather_chunks(u_ref, u_scr)
    o_ref[0] = _dot(u_scr[...], w_scr[...])


def _s5_summary(u, p_flat, layer, name):
    k = SUPER_K
    r = u.shape[1] // S5_CHUNK
    tr = min(SG_ROW_TILE, r)
    return pl.pallas_call(
        _summary_kernel,
        grid=(N_SUPER, r // tr),
        in_specs=[pl.BlockSpec((1, tr * S5_CHUNK, SUPER_LANES), lambda j, ri: (j, ri, 0)),
                  pl.BlockSpec((1, 1, k, S5_TILE), lambda j, ri: (layer, j, 0, 0))],
        out_specs=pl.BlockSpec((1, tr, k), lambda j, ri: (j, ri, 0)),
        out_shape=jax.ShapeDtypeStruct((N_SUPER, r, k), F32),
        scratch_shapes=[pltpu.VMEM((k, k), BF16), pltpu.VMEM((tr, k), BF16)],
        compiler_params=_cparams(("parallel", "arbitrary")),
        name=name,
    )(u, p_flat)


def _apply_kernel(u_ref, h_ref, lag_ref, q_ref, sel_ref, o_ref, w_scr, q_scr, u_scr):
    @pl.when(pl.program_id(1) == 0)
    def _():
        for s in range(S5_CHUNK):
            lo = SUPER_LANES * (S5_CHUNK - 1 - s)
            w_scr[s * SUPER_LANES:(s + 1) * SUPER_LANES, :] = lag_ref[0, 0, :, lo:lo + SUPER_K]
        col_group = (lax.broadcasted_iota(jnp.int32, (S5_TILE, SUPER_K), 1) % SUPER_LANES) // S5_GROUP_DIM
        for g in range(S5_SUPER):
            rs = slice(g * S5_TILE, (g + 1) * S5_TILE)
            q_scr[rs, :] = jnp.where(col_group == g, _dot(q_ref[0, 0, rs, :], sel_ref[...]), 0.0).astype(BF16)

    _gather_chunks(u_ref, u_scr)
    acc = _dot(u_scr[...], w_scr[...]) + _dot(h_ref[0].astype(BF16), q_scr[...])
    rows = u_scr.shape[0]
    for s in range(S5_CHUNK):
        o_ref[0, pl.ds(s, rows, stride=S5_CHUNK), :] = acc[:, s * SUPER_LANES:(s + 1) * SUPER_LANES]


def _s5_apply(u, h, lag_blocks, q_flat, layer, name):
    k = SUPER_K
    r = h.shape[1]
    tr = min(SG_APPLY_ROWS, r)
    rows = pl.BlockSpec((1, tr, k), lambda j, ri: (j, ri, 0))
    tok_rows = pl.BlockSpec((1, tr * S5_CHUNK, SUPER_LANES), lambda j, ri: (j, ri, 0))
    ri_, ci_ = np.arange(S5_TILE)[:, None], np.arange(k)[None, :]
    sel = (ri_ // S5_GROUP_DIM == ci_ // SUPER_LANES) & (ri_ % S5_GROUP_DIM == ci_ % S5_GROUP_DIM)
    sel = jnp.asarray(sel, F32).astype(BF16)
    return pl.pallas_call(
        _apply_kernel,
        grid=(N_SUPER, r // tr),
        in_specs=[tok_rows, rows,
                  pl.BlockSpec((1, 1) + lag_blocks.shape[2:], lambda j, ri: (layer, j, 0, 0)),
                  pl.BlockSpec((1, 1, k, S5_TILE), lambda j, ri: (layer, j, 0, 0)),
                  pl.BlockSpec(sel.shape, lambda j, ri: (0, 0))],
        out_specs=tok_rows,
        out_shape=jax.ShapeDtypeStruct(u.shape, F32),
        scratch_shapes=[pltpu.VMEM((k, k), BF16), pltpu.VMEM((k, k), BF16), pltpu.VMEM((tr, k), BF16)],
        compiler_params=_cparams(("parallel", "arbitrary")),
        name=name,
    )(u, h, lag_blocks, q_flat, sel)


def _rec_kernel(sc_ref, sl_ref, a_ref, hc_ref, hl_ref, *, nb, n_ctx, n_lat):
    half = S5_TILE // 2
    npieces = REC_LANES // S5_TILE
    fwd_lane = lax.broadcasted_iota(jnp.int32, (1, half), 1) < S5_STATE
    coef = [(a_ref[0, 0, :, p * S5_TILE:p * S5_TILE + half], a_ref[0, 0, :, p * S5_TILE + half:(p + 1) * S5_TILE])
            for p in range(npieces)]

    def step(s_ref, h_ref, c, carry, forward):
        rows = pl.ds(pl.multiple_of(c * nb, nb), nb)
        out = []
        for p in range(npieces):
            hre, him = carry[2 * p], carry[2 * p + 1]
            are, aim = coef[p]
            re_sl = slice(p * S5_TILE, p * S5_TILE + half)
            im_sl = slice(p * S5_TILE + half, (p + 1) * S5_TILE)
            if forward:
                h_ref[0, rows, re_sl] = jnp.where(fwd_lane, hre, 0.0)
                h_ref[0, rows, im_sl] = jnp.where(fwd_lane, him, 0.0)
            else:
                h_ref[0, rows, re_sl] = jnp.where(fwd_lane, h_ref[0, rows, re_sl], hre)
                h_ref[0, rows, im_sl] = jnp.where(fwd_lane, h_ref[0, rows, im_sl], him)
            sre, sim = s_ref[0, rows, re_sl], s_ref[0, rows, im_sl]
            out.append(are * hre - aim * him + sre)
            out.append(are * him + aim * hre + sim)
        return tuple(out)

    zero = tuple(jnp.zeros((nb, half), F32) for _ in range(2 * npieces))
    carry = lax.fori_loop(0, n_ctx, lambda c, cr: step(sc_ref, hc_ref, c, cr, True), zero)
    lax.fori_loop(0, n_lat, lambda c, cr: step(sl_ref, hl_ref, c, cr, True), carry)
    carry = lax.fori_loop(0, n_ctx, lambda i, cr: step(sc_ref, hc_ref, n_ctx - 1 - i, cr, False), zero)
    lax.fori_loop(0, n_lat, lambda i, cr: step(sl_ref, hl_ref, n_lat - 1 - i, cr, False), carry)


def _chunk_recurrence(s_c, s_l, a16, layer, nb):
    _, rc, lanes = s_c.shape
    rl = s_l.shape[1]
    spec = lambda r: pl.BlockSpec((1, r, REC_LANES), lambda j, i: (j, 0, i))
    return pl.pallas_call(
        functools.partial(_rec_kernel, nb=nb, n_ctx=rc // nb, n_lat=rl // nb),
        grid=(N_SUPER, lanes // REC_LANES),
        in_specs=[spec(rc), spec(rl), pl.BlockSpec((1, 1, 1, REC_LANES), lambda j, i: (layer, j, 0, i))],
        out_specs=[spec(rc), spec(rl)],
        out_shape=[jax.ShapeDtypeStruct(s_c.shape, F32), jax.ShapeDtypeStruct(s_l.shape, F32)],
        compiler_params=_cparams(("parallel", "parallel")),
        name="s5_recurrence",
    )(s_c, s_l, a16)


def _cmul(ar, ai, br, bi):
    return ar * br - ai * bi, ar * bi + ai * br


def _s5_matrices(lam_re, lam_im, log_step, b_re, b_im, c_re, c_im):
    t = S5_CHUNK
    hp = lax.Precision.HIGHEST
    taus = jnp.arange(t + 1, dtype=F32)[:, None, None]
    er, ei, wr, wi, a_re, a_im, ks = [], [], [], [], [], [], []
    for d in range(2):
        lr = jnp.minimum(lam_re[d].astype(F32), -1e-4)
        li = lam_im[d].astype(F32)
        dt = jnp.exp(log_step[d].astype(F32))[:, None]
        mag = jnp.exp(lr * dt * taus)
        pr, pi = mag * jnp.cos(li * dt * taus), mag * jnp.sin(li * dt * taus)
        den = lr * lr + li * li
        qr = ((pr[1] - 1.0) * lr + pi[1] * li) / den
        qi = (pi[1] * lr - (pr[1] - 1.0) * li) / den
        bbr, bbi = _cmul(qr[..., None], qi[..., None], b_re[d].astype(F32), b_im[d].astype(F32))
        e_r, e_i = _cmul(pr[..., None], pi[..., None], bbr[None], bbi[None])
        cr, ci = c_re[d].astype(F32), c_im[d].astype(F32)
        w_r, w_i = _cmul(cr[None], ci[None], pr[:, :, None, :], pi[:, :, None, :])
        k = (jnp.einsum('gmp,tgpn->tgmn', cr, e_r[:t], precision=hp)
             - jnp.einsum('gmp,tgpn->tgmn', ci, e_i[:t], precision=hp))
        er.append(e_r); ei.append(e_i); wr.append(w_r); wi.append(w_i); ks.append(k)
        a_re.append(pr[t]); a_im.append(pi[t])
    rows = lambda e: e.transpose(1, 0, 3, 2)
    p_mat = jnp.concatenate([rows(er[0][:t][::-1]), rows(er[1][:t]),
                             rows(ei[0][:t][::-1]), rows(ei[1][:t])], axis=-1)
    cols = lambda w: w.transpose(1, 3, 0, 2)
    q_mat = jnp.concatenate([cols(wr[0][1:]), cols(wr[1][1:][::-1]),
                             -cols(wi[0][1:]), -cols(wi[1][1:][::-1])], axis=1)
    a16 = jnp.concatenate([a_re[0], a_re[1], a_im[0], a_im[1]], axis=-1)

    gd, sl = S5_GROUP_DIM, SUPER_LANES

    def spread(a, nblk):
        r = jnp.arange(nblk * gd)[:, None]
        c = jnp.arange(nblk * sl)[None, :]
        sel = ((r // gd == c // sl) & (r % gd == c % gd)).astype(BF16)
        return jnp.einsum('jrk,kc->jrc', a.astype(BF16), sel, preferred_element_type=F32).astype(BF16)

    def same_group(row_group, col_group):
        return (row_group[:, None] == col_group[None, :]).astype(BF16)[None]

    nlag = 2 * t - 1
    kf = ks[0].transpose(0, 1, 3, 2)
    kb = ks[1].transpose(0, 1, 3, 2)
    klag = jnp.concatenate([kb[1:][::-1], (kf[0] + kb[0])[None], kf[1:]], axis=0)
    k_flat = klag.transpose(1, 2, 0, 3).reshape(N_SUPER, sl, nlag * gd)
    lag_blocks = spread(k_flat, nlag) * same_group(jnp.arange(sl) // gd, (jnp.arange(nlag * sl) % sl) // gd)
    p_flat = p_mat.reshape(N_SUPER, S5_SUPER, t, gd, S5_TILE).transpose(0, 2, 1, 3, 4).reshape(N_SUPER, SUPER_K, S5_TILE)
    q_flat = q_mat.reshape(N_SUPER, S5_SUPER * S5_TILE, S5_TILE)
    return lag_blocks, p_flat.astype(BF16), q_flat.astype(BF16), a16.reshape(N_SUPER, 1, SUPER_K)


def _s5_scan(s_l, s_c, mats, layer):
    lag_blocks, p_flat, q_flat, a16 = mats
    b = s_l.shape[2]
    u_l = s_l.reshape(N_SUPER, -1, SUPER_LANES)
    u_c = s_c.reshape(N_SUPER, -1, SUPER_LANES)
    sum_l = _s5_summary(u_l, p_flat, layer, "s5_summary")
    sum_c = _s5_summary(u_c, p_flat, layer, "s5_summary_ctx")
    h_c, h_l = _chunk_recurrence(sum_c, sum_l, a16, layer, b)
    y_l = _s5_apply(u_l, h_l, lag_blocks, q_flat, layer, "s5_apply")
    y_c = _s5_apply(u_c, h_c, lag_blocks, q_flat, layer, "s5_apply_ctx")
    return y_l.reshape(s_l.shape), y_c.reshape(s_c.shape)


def kernel(x, c, ctx, c_ctx, w_ada, b_ada, pre_g, post_g, w_in_even, w_out_even, na_rpb,
           w_in_odd, w_out_odd, sgu_w, sgu_b, sgu_g, s5_lam_re, s5_lam_im, s5_log_step,
           s5_b_re, s5_b_im, s5_c_re, s5_c_im, s5_d, glu_w, glu_b):
    b, l, d = x.shape
    lc = ctx.shape[1]
    depth = w_ada.shape[0]
    rows = l // GRID_W
    kh = min(NA_ROWS, rows)

    n_rows = -(-(b + 1) // 8) * 8
    cond = jnp.zeros((n_rows, d), F32).at[:b].set(c).at[b].set(c_ctx)
    mod = _ada_mod(cond, w_ada, b_ada)

    bd = _group_dft_mat()
    cm_l, sm_l = _dft_mats(l)
    cm_c, sm_c = _dft_mats(lc)

    w_in_e, w_in_o = w_in_even.astype(BF16), w_in_odd.astype(BF16)
    w_out_e, w_out_o = w_out_even.astype(BF16), w_out_odd.astype(BF16)
    sgu_ws, glu_ws = sgu_w.astype(BF16), glu_w.astype(BF16)
    n_odd = w_in_odd.shape[0]
    gw = SGU_WIDTH // SGU_GROUPS
    sgu_bs = [jnp.repeat(sgu_b[j].astype(F32).T, gw, axis=1) for j in range(n_odd)]
    tm_l, tm_c = min(ROW_TILE, l), min(ROW_TILE, lc)

    def modulation(i):
        lat = tuple(mod[i, :b, k * d:(k + 1) * d][:, None, :] for k in range(3))
        ctx_mod = jnp.broadcast_to(mod[i, b][None, None, :], (b, 1, 3 * d))
        return lat, tuple(ctx_mod[..., k * d:(k + 1) * d] for k in range(3))

    def in_part(i, seq, shift, scale):
        tm = min(ROW_TILE, seq)
        if i % 2 == 0:
            return ("even",) + _even_in_io(b, seq, d, tm, pre_g[i], scale, shift, w_in_e, i // 2, bd)
        j = i // 2
        return ("odd",) + _odd_in_io(b, seq, d, tm, pre_g[i], scale, shift, w_in_o, sgu_ws, j, sgu_bs[j], sgu_g[j])

    na_tables = jax.vmap(lambda r: _na_bias(r, kh))(na_rpb)
    s5_mats = jax.vmap(_s5_matrices)(s5_lam_re, s5_lam_im, s5_log_step, s5_b_re, s5_b_im, s5_c_re, s5_c_im)

    (shift, scale, gate), (shift_c, scale_c, gate_c) = modulation(0)
    xl, xc = x, ctx
    res_l = _projection(xl, None, in_part(0, l, shift, scale), "proj_in")
    res_c = _projection(xc, None, in_part(0, lc, shift_c, scale_c), "proj_in_ctx")
    for i in range(depth):
        last = i == depth - 1
        j = i // 2
        if i % 2 == 0:
            zc, zs, ga, q, k_, v, gb = res_l
            zc_c, zs_c, ga_c, q_c, k_c, v_c, gb_c = res_c
            a_l = _dft_mix(cm_l, sm_l, zc, zs, ga)
            n_l = _na_attention(q, k_, v, k_c, v_c, gb, na_tables, j)
            out_l = ("even",) + _even_out_io(d, tm_l, a_l, n_l, w_out_e, j, post_g[i], gate)
            if not last:
                a_c = _dft_mix(cm_c, sm_c, zc_c, zs_c, ga_c)
                n_c = _ctx_attention(q_c, k_c, v_c, gb_c)
                out_c = ("even",) + _even_out_io(d, tm_c, a_c, n_c, w_out_e, j, post_g[i], gate_c)
        else:
            sg_l, s_l, gd_l = res_l
            sg_c, s_c, gd_c = res_c
            y_l, y_c = _s5_scan(s_l, s_c, s5_mats, j)
            out_l = ("odd",) + _odd_out_io(d, tm_l, sg_l, y_l, s_l, gd_l, s5_d[j], glu_ws, glu_b[j], w_out_o, j,
                                           post_g[i], gate)
            if not last:
                out_c = ("odd",) + _odd_out_io(d, tm_c, sg_c, y_c, s_c, gd_c, s5_d[j], glu_ws, glu_b[j], w_out_o, j,
                                               post_g[i], gate_c)
        if last:
            (xl,) = _projection(xl, out_l, None, "proj_out")
        else:
            (shift, scale, gate), (shift_c, scale_c, gate_c) = modulation(i + 1)
            xl, *res_l = _projection(xl, out_l, in_part(i + 1, l, shift, scale), "proj_out_in")
            xc, *res_c = _projection(xc, out_c, in_part(i + 1, lc, shift_c, scale_c), "proj_out_in_ctx")
    return xl
```

```python
import functools

import numpy as np
import jax
import jax.numpy as jnp
from jax import lax
from jax.experimental import pallas as pl
from jax.experimental.pallas import tpu as pltpu

F32 = jnp.float32
BF16 = jnp.bfloat16

EPS = 1e-6
GRID_W = 64
FNET_WIDTH = 256
FNET_GROUP_DIM = 64
NA_WIDTH = 768
NA_HEAD_DIM = 64
NA_HEADS = 12
NA_ROWS = 8
NA_COLS = 16
SGU_CHUNK = 128
SGU_WIDTH = 512
SGU_GROUPS = 4
S5_WIDTH = 512
S5_GROUP_DIM = 16
S5_GROUPS = 32
S5_STATE = 64
S5_CHUNK = 16
S5_TILE = S5_CHUNK * S5_GROUP_DIM
LOG2E = float(np.log2(np.e))
QK_SCALE = NA_HEAD_DIM ** -0.5 * LOG2E

V7X_VMEM_BYTES = 64 * 1024 * 1024
VMEM_LIMIT = 48 * 1024 * 1024
ROW_TILE = 512


def _cparams(sem):
    return pltpu.CompilerParams(dimension_semantics=sem, vmem_limit_bytes=VMEM_LIMIT)


def _silu(x):
    return x * jax.nn.sigmoid(x)


def _dot(a, b):
    return jnp.dot(a, b, preferred_element_type=F32)


def _dot_nt(a, b):
    return lax.dot_general(a, b, (((1,), (1,)), ((), ())), preferred_element_type=F32)


def _ada_kernel(c_ref, w_ref, b_ref, o_ref):
    c = c_ref[...]
    o_ref[0] = jnp.dot(_silu(c), w_ref[0], preferred_element_type=F32,
                       precision=lax.Precision.HIGHEST) + b_ref[0]


def _ada_mod(cond, w_ada, b_ada):
    depth, d, n = w_ada.shape
    rows = cond.shape[0]
    tn = 1024
    return pl.pallas_call(
        _ada_kernel,
        grid=(depth, n // tn),
        in_specs=[pl.BlockSpec((rows, d), lambda i, j: (0, 0)),
                  pl.BlockSpec((1, d, tn), lambda i, j: (i, 0, j)),
                  pl.BlockSpec((1, 1, tn), lambda i, j: (i, 0, j))],
        out_specs=pl.BlockSpec((1, rows, tn), lambda i, j: (i, 0, j)),
        out_shape=jax.ShapeDtypeStruct((depth, rows, n), F32),
        compiler_params=_cparams(("parallel", "parallel")),
        name="ada_mod",
    )(cond, w_ada, b_ada.reshape(depth, 1, n))


def _norm_mod(x, g, scale, shift):
    y = x * lax.rsqrt(jnp.mean(x * x, axis=-1, keepdims=True) + EPS)
    return (y * g) * (1.0 + scale) + shift


def _post(y, post_g, gate, x):
    yn = y * lax.rsqrt(jnp.mean(y * y, axis=-1, keepdims=True) + EPS)
    return x + gate * (yn * post_g)


def _vec_spec(d):
    return pl.BlockSpec((1, 1, d), lambda bi, i: (bi, 0, 0))


def _const_spec(a):
    return pl.BlockSpec(a.shape, lambda bi, i: (0,) * a.ndim)


def _layer_spec(a, j):
    return pl.BlockSpec((1,) + a.shape[1:], lambda bi, i: (j,) + (0,) * (a.ndim - 1))


def _tok_spec(tm, width):
    return pl.BlockSpec((1, tm, width), lambda bi, i: (bi, i, 0))


def _even_in_body(x, g_ref, sc, sh, w_ref, bd_ref, z_scr, outs):
    zc_ref, zs_ref, ga_ref, q_ref, k_ref, v_ref, gb_ref = outs
    h = _norm_mod(x, g_ref[...], sc, sh).astype(BF16)
    fa = _dot(h, w_ref[0, :, 0:FNET_WIDTH]).astype(BF16)
    z = _dot(fa, bd_ref[...])
    planes = z.shape[1] // 128
    for p in range(planes):
        z_scr[p] = z[:, p * 128:(p + 1) * 128]
    half = z.shape[0] // 2
    for par in range(2):
        for p in range(planes):
            piece = z_scr[p, pl.ds(par, half, stride=2), :].astype(BF16)
            ref, q0 = (zc_ref, p) if p < planes // 2 else (zs_ref, p - planes // 2)
            ref[par, :, q0 * 128:(q0 + 1) * 128] = piece
    o = FNET_WIDTH
    ga_ref[0] = _dot(h, w_ref[0, :, o:o + FNET_WIDTH]).astype(BF16)
    o += FNET_WIDTH
    q_ref[0] = (_dot(h, w_ref[0, :, o:o + NA_WIDTH]) * QK_SCALE).astype(BF16)
    o += NA_WIDTH
    k_ref[0] = _dot(h, w_ref[0, :, o:o + NA_WIDTH]).astype(BF16)
    o += NA_WIDTH
    v_ref[0] = _dot(h, w_ref[0, :, o:o + NA_WIDTH]).astype(BF16)
    o += NA_WIDTH
    gb_ref[0] = _dot(h, w_ref[0, :, o:o + NA_WIDTH]).astype(BF16)


def _even_in_io(b, l, d, tm, pre_g, scale, shift, w, j, bd):
    zspec = pl.BlockSpec((2, tm // 2, FNET_WIDTH), lambda bi, i: (0, i, bi))
    zsd = jax.ShapeDtypeStruct((2, l // 2, b * FNET_WIDTH), BF16)
    sd = lambda width: jax.ShapeDtypeStruct((b, l, width), BF16)
    g2 = pre_g.reshape(1, d)
    widths = (FNET_WIDTH, NA_WIDTH, NA_WIDTH, NA_WIDTH, NA_WIDTH)
    return ([g2, scale, shift, w, bd],
            [_const_spec(g2), _vec_spec(d), _vec_spec(d), _layer_spec(w, j), _const_spec(bd)],
            [zspec, zspec] + [_tok_spec(tm, wd) for wd in widths],
            [zsd, zsd] + [sd(wd) for wd in widths],
            [pltpu.VMEM((2 * FNET_WIDTH // 128, tm, 128), F32)])


def _even_out_body(a_ref, n_ref, w_ref, pg_ref, gate, x):
    y = _dot(a_ref[0], w_ref[0, :FNET_WIDTH, :]) + _dot(n_ref[0], w_ref[0, FNET_WIDTH:, :])
    return _post(y, pg_ref[...], gate, x)


def _even_out_io(d, tm, a, n, w, j, post_g, gate):
    pg2 = post_g.reshape(1, d)
    return ([a, n, w, pg2, gate],
            [_tok_spec(tm, FNET_WIDTH), _tok_spec(tm, NA_WIDTH), _layer_spec(w, j), _const_spec(pg2), _vec_spec(d)])


S5_SUPER = 8
N_SUPER = S5_GROUPS // S5_SUPER
SUPER_LANES = S5_SUPER * S5_GROUP_DIM


def _super_spec(tm):
    return pl.BlockSpec((N_SUPER, tm // S5_CHUNK, 1, S5_CHUNK, SUPER_LANES), lambda bi, i: (0, i, bi, 0, 0))


def _super_shape(b, l):
    return jax.ShapeDtypeStruct((N_SUPER, l // S5_CHUNK, b, S5_CHUNK, SUPER_LANES), F32)


def _odd_in_body(x, g_ref, sc, sh, w_ref, ws_ref, bs_ref, sg_g_ref, outs):
    sg_ref, s_ref, gd_ref = outs
    tm = x.shape[0]
    h = _norm_mod(x, g_ref[...], sc, sh).astype(BF16)
    wd = SGU_WIDTH
    u = _dot(h, w_ref[0, :, 0:wd])
    v = _dot(h, w_ref[0, :, wd:2 * wd])
    gc = _dot(h, w_ref[0, :, 2 * wd:3 * wd])
    s = _dot(h, w_ref[0, :, 3 * wd:3 * wd + S5_WIDTH])
    for j in range(N_SUPER):
        sj = s[:, j * SUPER_LANES:(j + 1) * SUPER_LANES]
        s_ref[j, :, 0] = sj.reshape(tm // S5_CHUNK, S5_CHUNK, SUPER_LANES)
    gd_ref[0] = _dot(h, w_ref[0, :, 3 * wd + S5_WIDTH:]).astype(BF16)
    vc = v - jnp.mean(v, axis=-1, keepdims=True)
    vn = (vc * lax.rsqrt(jnp.mean(vc * vc, axis=-1, keepdims=True) + EPS) * sg_g_ref[...]).astype(BF16)
    gate = u * _silu(gc)
    gw = SGU_WIDTH // SGU_GROUPS
    for j in range(tm // SGU_CHUNK):
        rs = slice(j * SGU_CHUNK, (j + 1) * SGU_CHUNK)
        for g in range(SGU_GROUPS):
            cs = slice(g * gw, (g + 1) * gw)
            mixed = _dot(ws_ref[0, g], vn[rs, cs]) + bs_ref[:, cs]
            sg_ref[0, rs, cs] = (gate[rs, cs] * mixed).astype(BF16)


def _odd_in_io(b, l, d, tm, pre_g, scale, shift, w, ws, j, bs_full, sgu_g):
    g2 = pre_g.reshape(1, d)
    sg2 = sgu_g.reshape(1, SGU_WIDTH)
    sd = jax.ShapeDtypeStruct((b, l, SGU_WIDTH), BF16)
    return ([g2, scale, shift, w, ws, bs_full, sg2],
            [_const_spec(g2), _vec_spec(d), _vec_spec(d), _layer_spec(w, j), _layer_spec(ws, j), _const_spec(bs_full),
             _const_spec(sg2)],
            [_tok_spec(tm, SGU_WIDTH), _super_spec(tm), _tok_spec(tm, S5_WIDTH)],
            [sd, _super_shape(b, l), sd],
            [])


def _gelu_tanh(x):
    return 0.5 * x * (1.0 + jnp.tanh(float(np.sqrt(2.0 / np.pi)) * (x + 0.044715 * (x * x * x))))


def _odd_out_body(sg_ref, y_ref, s_ref, gd_ref, dsk_ref, wg_ref, bg_ref, w_ref, pg_ref, gate, x):
    tm = x.shape[0]
    tokens = lambda ref: jnp.concatenate([ref[j, :, 0].reshape(tm, SUPER_LANES) for j in range(N_SUPER)], axis=-1)
    ys = tokens(y_ref)
    s = tokens(s_ref)
    z = _gelu_tanh(ys + dsk_ref[...] * s)
    zz = _dot(z.astype(BF16), wg_ref[0]) + bg_ref[...]
    gd = gd_ref[0].astype(F32)
    ss = zz[:, :S5_WIDTH] * jax.nn.sigmoid(zz[:, S5_WIDTH:]) * _silu(gd)
    y = _dot(sg_ref[0], w_ref[0, :SGU_WIDTH, :]) + _dot(ss.astype(BF16), w_ref[0, SGU_WIDTH:, :])
    return _post(y, pg_ref[...], gate, x)


def _odd_out_io(d, tm, sg, y, s, gd, d_skip, wg, bg, w, j, post_g, gate):
    dsk = d_skip.reshape(1, S5_WIDTH).astype(F32)
    bg2 = bg.reshape(1, 2 * S5_WIDTH).astype(F32)
    pg2 = post_g.reshape(1, d)
    return ([sg, y, s, gd, dsk, wg, bg2, w, pg2, gate],
            [_tok_spec(tm, SGU_WIDTH), _super_spec(tm), _super_spec(tm), _tok_spec(tm, S5_WIDTH),
             _const_spec(dsk), _layer_spec(wg, j), _const_spec(bg2), _layer_spec(w, j), _const_spec(pg2), _vec_spec(d)])


_OUT_BODY = {"even": (_even_out_body, 5), "odd": (_odd_out_body, 10)}
_IN_BODY = {"even": (_even_in_body, 5, 7), "odd": (_odd_in_body, 7, 3)}


def _proj_kernel(*refs, out_kind, in_kind):
    pos = 0
    if out_kind is not None:
        body, n_ops = _OUT_BODY[out_kind]
        ops = refs[pos:pos + n_ops]
        pos += n_ops
    x_ref = refs[pos]
    pos += 1
    if in_kind is not None:
        in_body, n_in, n_out = _IN_BODY[in_kind]
        in_ops = refs[pos:pos + n_in]
        pos += n_in
    x = x_ref[0]
    if out_kind is not None:
        x = body(*ops[:-1], ops[-1][0], x)
        refs[pos][0] = x
        pos += 1
    if in_kind is not None:
        outs = refs[pos:pos + n_out]
        scratch = refs[pos + n_out:]
        g_ref, sc_ref, sh_ref = in_ops[:3]
        in_body(x, g_ref, sc_ref[0], sh_ref[0], *in_ops[3:], *scratch, outs)


def _projection(x, out_part, in_part, name):
    b, l, d = x.shape
    tm = min(ROW_TILE, l)
    operands, in_specs, out_specs, out_shapes, scratch = [], [], [], [], []
    if out_part is not None:
        operands += out_part[1]
        in_specs += out_part[2]
    operands.append(x)
    in_specs.append(_tok_spec(tm, d))
    if in_part is not None:
        operands += in_part[1]
        in_specs += in_part[2]
    if out_part is not None:
        out_specs.append(_tok_spec(tm, d))
        out_shapes.append(jax.ShapeDtypeStruct((b, l, d), F32))
    if in_part is not None:
        out_specs += in_part[3]
        out_shapes += in_part[4]
        scratch = in_part[5]
    kern = functools.partial(_proj_kernel, out_kind=out_part and out_part[0], in_kind=in_part and in_part[0])
    return pl.pallas_call(
        kern,
        grid=(b, l // tm),
        in_specs=in_specs,
        out_specs=out_specs,
        out_shape=out_shapes,
        scratch_shapes=scratch,
        compiler_params=_cparams(("parallel", "parallel")),
        name=name,
    )(*operands)


def _dft_kernel(ce_ref, se_ref, co_ref, so_ref, zc_ref, zs_ref, ga_ref, o_ref, acc_e, acc_o, *, nb, norm):
    kk = pl.program_id(1)

    @pl.when(kk == 0)
    def _():
        acc_e[...] = jnp.zeros_like(acc_e)
        acc_o[...] = jnp.zeros_like(acc_o)

    acc_e[...] += _dot(ce_ref[...], zc_ref[0]) + _dot(se_ref[...], zs_ref[0])
    acc_o[...] += _dot(co_ref[...], zc_ref[1]) + _dot(so_ref[...], zs_ref[1])

    @pl.when(kk == pl.num_programs(1) - 1)
    def _():
        for bi in range(nb):
            cs = slice(bi * FNET_WIDTH, (bi + 1) * FNET_WIDTH)
            e, o = acc_e[:, cs], acc_o[:, cs]
            for half, y in enumerate((e + o, e - o)):
                g = ga_ref[bi, half].astype(F32)
                o_ref[bi, half] = (y * norm * _silu(g)).astype(BF16)


def _dft_mix(cm, sm, zc, zs, ga):
    b, l, _ = ga.shape
    lh = l // 2
    tm = min(512, lh)
    tk = min(256, lh)
    nk = lh // tk
    nc = b * FNET_WIDTH
    kern = functools.partial(_dft_kernel, nb=b, norm=float((l * FNET_GROUP_DIM) ** -0.5))
    mat_e = pl.BlockSpec((tm, tk), lambda i, k: (i, k))
    mat_o = pl.BlockSpec((tm, tk), lambda i, k: (i, nk + k))
    zspec = pl.BlockSpec((2, tk, nc), lambda i, k: (0, k, 0))
    halves = pl.BlockSpec((b, 2, tm, FNET_WIDTH), lambda i, k: (0, 0, i, 0))
    out = pl.pallas_call(
        kern,
        grid=(lh // tm, nk),
        in_specs=[mat_e, mat_e, mat_o, mat_o, zspec, zspec, halves],
        out_specs=halves,
        out_shape=jax.ShapeDtypeStruct((b, 2, lh, FNET_WIDTH), BF16),
        scratch_shapes=[pltpu.VMEM((tm, nc), F32), pltpu.VMEM((tm, nc), F32)],
        compiler_params=_cparams(("parallel", "arbitrary")),
        name="dft_mix",
    )(cm, sm, cm, sm, zc, zs, ga.reshape(b, 2, lh, FNET_WIDTH))
    return out.reshape(b, l, FNET_WIDTH)


def _dft_gen_kernel(cb_ref, sb_ref, ca_ref, sa_ref, c_ref, s_ref):
    ca, sa = ca_ref[0], sa_ref[0]
    cb, sb = cb_ref[...], sb_ref[...]
    c_ref[...] = (ca * cb - sa * sb).astype(BF16)
    s_ref[...] = (-(sa * cb + ca * sb)).astype(BF16)


def _dft_mats(l):
    lh = l // 2
    tr = min(256, lh)
    kh = jnp.arange(lh, dtype=jnp.int32)
    k = jnp.concatenate([2 * kh, 2 * kh + 1])
    w = 2.0 * np.pi / l
    ang_b = ((jnp.arange(tr, dtype=jnp.int32)[:, None] * k[None, :]) % l).astype(F32) * w
    ang_a = ((jnp.arange(lh // tr, dtype=jnp.int32)[:, None] * tr * k[None, :]) % l).astype(F32) * w
    ca, sa = jnp.cos(ang_a)[:, None, :], jnp.sin(ang_a)[:, None, :]
    base = pl.BlockSpec((tr, l), lambda i: (0, 0))
    rowv = pl.BlockSpec((1, 1, l), lambda i: (i, 0, 0))
    out = pl.BlockSpec((tr, l), lambda i: (i, 0))
    sd = jax.ShapeDtypeStruct((lh, l), BF16)
    return pl.pallas_call(
        _dft_gen_kernel,
        grid=(lh // tr,),
        in_specs=[base, base, rowv, rowv],
        out_specs=[out, out],
        out_shape=[sd, sd],
        compiler_params=_cparams(("parallel",)),
        name="dft_gen",
    )(jnp.cos(ang_b), jnp.sin(ang_b), ca, sa)


def _group_dft_mat():
    n = FNET_GROUP_DIM
    j = np.arange(n)
    ang = 2.0 * np.pi * ((j[:, None] * j[None, :]) % n) / n
    eye = np.eye(FNET_WIDTH // n)
    mat = np.concatenate([np.kron(eye, np.cos(ang)), np.kron(eye, np.sin(ang))], axis=1)
    return jnp.asarray(mat, F32).astype(BF16)


HEADS_PER_TILE = 4
HEAD_TILE = HEADS_PER_TILE * NA_HEAD_DIM
HEAD_TILES = NA_WIDTH // HEAD_TILE


def _head_masks():
    row_head = lax.broadcasted_iota(jnp.int32, (HEAD_TILE, HEAD_TILE), 0) // GRID_W
    lane_head = lax.broadcasted_iota(jnp.int32, (HEAD_TILE, HEAD_TILE), 1) // NA_HEAD_DIM
    return row_head == lane_head, lax.broadcasted_iota(jnp.int32, (GRID_W, HEAD_TILE), 1) // NA_HEAD_DIM


def _stacked_queries(q4, own):
    return jnp.where(own, jnp.concatenate([q4] * HEADS_PER_TILE, axis=0), 0.0).astype(BF16)


def _own_head_blocks(o, out_head):
    o4 = o[0:GRID_W]
    for h in range(1, HEADS_PER_TILE):
        o4 = jnp.where(out_head == h, o[h * GRID_W:(h + 1) * GRID_W], o4)
    return o4


NA_ROWS_PER_STEP = 8


def _na_kernel(q_ref, k_ref, v_ref, kc_ref, vc_ref, gb_ref, bias_ref, o_ref, *, rows, kh, rps):
    nloc = kh * GRID_W
    own, out_head = _head_masks()
    for rr in range(rps):
        r = pl.program_id(1) * rps + rr
        r0 = jnp.clip(r - kh // 2, 0, rows - kh)
        start = pl.multiple_of(r0 * GRID_W, GRID_W)
        qs = slice(rr * GRID_W, (rr + 1) * GRID_W)
        for t in range(HEAD_TILES):
            cs = slice(t * HEAD_TILE, (t + 1) * HEAD_TILE)
            qm = _stacked_queries(q_ref[0, qs, cs], own)
            kl = k_ref[0, pl.ds(start, nloc), cs]
            vl = v_ref[0, pl.ds(start, nloc), cs]
            var = r0 - r + (NA_ROWS - 1)
            bias = jnp.concatenate([bias_ref[t, var + 2 * m_] for m_ in range(kh // 2)], axis=-1)
            s_loc = _dot_nt(qm, kl) + bias
            s_ctx = _dot_nt(qm, kc_ref[0, :, cs])
            m = jnp.maximum(jnp.max(s_loc, axis=-1, keepdims=True), jnp.max(s_ctx, axis=-1, keepdims=True))
            p_loc = jnp.exp2(s_loc - m)
            p_ctx = jnp.exp2(s_ctx - m)
            den = jnp.sum(p_loc, axis=-1, keepdims=True) + jnp.sum(p_ctx, axis=-1, keepdims=True)
            o = (_dot(p_loc.astype(BF16), vl) + _dot(p_ctx.astype(BF16), vc_ref[0, :, cs])) / den
            g = gb_ref[0, qs, cs].astype(F32)
            o_ref[0, qs, cs] = (_own_head_blocks(o, out_head) * _silu(g)).astype(BF16)


def _na_attention(q, k, v, kc, vc, gb, bias):
    b, l, w = q.shape
    lc = kc.shape[1]
    rows = l // GRID_W
    kh = min(NA_ROWS, rows)
    rps = NA_ROWS_PER_STEP

    row = pl.BlockSpec((1, rps * GRID_W, w), lambda bi, i: (bi, i, 0))
    full = lambda n: pl.BlockSpec((1, n, w), lambda bi, i: (bi, 0, 0))
    table = pl.BlockSpec(bias.shape, lambda bi, i: (0, 0, 0, 0), pipeline_mode=pl.Buffered(1))
    kern = functools.partial(_na_kernel, rows=rows, kh=kh, rps=rps)
    return pl.pallas_call(
        kern,
        grid=(b, rows // rps),
        in_specs=[row, full(l), full(l), full(lc), full(lc), row, table],
        out_specs=row,
        out_shape=jax.ShapeDtypeStruct((b, l, w), BF16),
        compiler_params=_cparams(("parallel", "arbitrary")),
        name="na_attention",
    )(q, k, v, kc, vc, gb, bias)


def _na_bias(rpb, kh):
    assert kh % 2 == 0
    cq = np.arange(GRID_W)
    ck = np.arange(GRID_W)
    c0 = np.clip(cq - NA_COLS // 2, 0, GRID_W - NA_COLS)
    col_in = (ck[None, :] >= c0[:, None]) & (ck[None, :] < c0[:, None] + NA_COLS)
    dc = np.clip(ck[None, :] - cq[:, None] + (NA_COLS - 1), 0, 2 * NA_COLS - 2)
    onehot = (dc.reshape(1, -1) == np.arange(2 * NA_COLS - 1)[:, None]).astype(np.float32)
    t = jnp.einsum('hrc,cx->hrx', rpb.astype(F32) * LOG2E, jnp.asarray(onehot), precision=lax.Precision.HIGHEST)
    n_dr = 2 * NA_ROWS - 1
    t = t.reshape(NA_HEADS, n_dr, GRID_W, GRID_W)
    t = jnp.where(col_in[None, None], t, -1e30)
    pairs = jnp.concatenate([t[:, :-1], t[:, 1:]], axis=-1)
    pairs = pairs.reshape(HEAD_TILES, HEADS_PER_TILE, n_dr - 1, GRID_W, 2 * GRID_W).transpose(0, 2, 1, 3, 4)
    return pairs.reshape(HEAD_TILES, n_dr - 1, HEAD_TILE, 2 * GRID_W)


def _ctx_attn_kernel(q_ref, k_ref, v_ref, gb_ref, o_ref):
    own, out_head = _head_masks()
    for t in range(HEAD_TILES):
        cs = slice(t * HEAD_TILE, (t + 1) * HEAD_TILE)
        for q0 in range(0, q_ref.shape[1], GRID_W):
            qs = slice(q0, q0 + GRID_W)
            s = _dot_nt(_stacked_queries(q_ref[0, qs, cs], own), k_ref[0, :, cs])
            p = jnp.exp2(s - jnp.max(s, axis=-1, keepdims=True))
            o = _dot(p.astype(BF16), v_ref[0, :, cs]) / jnp.sum(p, axis=-1, keepdims=True)
            g = gb_ref[0, qs, cs].astype(F32)
            o_ref[0, qs, cs] = (_own_head_blocks(o, out_head) * _silu(g)).astype(BF16)


def _ctx_attention(q, k, v, gb):
    b, lc, w = q.shape
    spec = pl.BlockSpec((1, lc, w), lambda bi: (bi, 0, 0))
    return pl.pallas_call(
        _ctx_attn_kernel,
        grid=(b,),
        in_specs=[spec, spec, spec, spec],
        out_specs=spec,
        out_shape=jax.ShapeDtypeStruct((b, lc, w), BF16),
        compiler_params=_cparams(("parallel",)),
        name="ctx_attention",
    )(q, k, v, gb)


SUPER_K = S5_CHUNK * SUPER_LANES
SG_ROW_TILE = 512
SG_APPLY_ROWS = 256
REC_LANES = 1024


def _gather_chunks(u_ref, u_scr):
    rows = u_scr.shape[0]
    for s in range(S5_CHUNK):
        u_scr[:, s * SUPER_LANES:(s + 1) * SUPER_LANES] = u_ref[0, pl.ds(s, rows, stride=S5_CHUNK), :].astype(BF16)


def _summary_kernel(u_ref, p_ref, o_ref, w_scr, u_scr):
    @pl.when(pl.program_id(1) == 0)
    def _():
        row_group = (lax.broadcasted_iota(jnp.int32, (SUPER_K, S5_TILE), 0) % SUPER_LANES) // S5_GROUP_DIM
        p = p_ref[0]
        for h in range(S5_SUPER):
            w_scr[:, h * S5_TILE:(h + 1) * S5_TILE] = jnp.where(row_group == h, p, 0.0).astype(BF16)

    _gather_chunks(u_ref, u_scr)
    o_ref[0] = _dot(u_scr[...], w_scr[...])


def _s5_summary(u, p_flat, name):
    k = SUPER_K
    r = u.shape[1] // S5_CHUNK
    tr = min(SG_ROW_TILE, r)
    return pl.pallas_call(
        _summary_kernel,
        grid=(N_SUPER, r // tr),
        in_specs=[pl.BlockSpec((1, tr * S5_CHUNK, SUPER_LANES), lambda j, ri: (j, ri, 0)),
                  pl.BlockSpec((1, k, S5_TILE), lambda j, ri: (j, 0, 0))],
        out_specs=pl.BlockSpec((1, tr, k), lambda j, ri: (j, ri, 0)),
        out_shape=jax.ShapeDtypeStruct((N_SUPER, r, k), F32),
        scratch_shapes=[pltpu.VMEM((k, k), BF16), pltpu.VMEM((tr, k), BF16)],
        compiler_params=_cparams(("parallel", "arbitrary")),
        name=name,
    )(u, p_flat)


def _apply_kernel(u_ref, h_ref, lag_ref, q_ref, sel_ref, o_ref, w_scr, q_scr, u_scr):
    @pl.when(pl.program_id(1) == 0)
    def _():
        for s in range(S5_CHUNK):
            lo = SUPER_LANES * (S5_CHUNK - 1 - s)
            w_scr[s * SUPER_LANES:(s + 1) * SUPER_LANES, :] = lag_ref[0, :, lo:lo + SUPER_K]
        col_group = (lax.broadcasted_iota(jnp.int32, (S5_TILE, SUPER_K), 1) % SUPER_LANES) // S5_GROUP_DIM
        for g in range(S5_SUPER):
            rs = slice(g * S5_TILE, (g + 1) * S5_TILE)
            q_scr[rs, :] = jnp.where(col_group == g, _dot(q_ref[0, rs, :], sel_ref[...]), 0.0).astype(BF16)

    _gather_chunks(u_ref, u_scr)
    acc = _dot(u_scr[...], w_scr[...]) + _dot(h_ref[0].astype(BF16), q_scr[...])
    rows = u_scr.shape[0]
    for s in range(S5_CHUNK):
        o_ref[0, pl.ds(s, rows, stride=S5_CHUNK), :] = acc[:, s * SUPER_LANES:(s + 1) * SUPER_LANES]


def _s5_apply(u, h, lag_blocks, q_flat, name):
    k = SUPER_K
    r = h.shape[1]
    tr = min(SG_APPLY_ROWS, r)
    rows = pl.BlockSpec((1, tr, k), lambda j, ri: (j, ri, 0))
    tok_rows = pl.BlockSpec((1, tr * S5_CHUNK, SUPER_LANES), lambda j, ri: (j, ri, 0))
    ri_, ci_ = np.arange(S5_TILE)[:, None], np.arange(k)[None, :]
    sel = (ri_ // S5_GROUP_DIM == ci_ // SUPER_LANES) & (ri_ % S5_GROUP_DIM == ci_ % S5_GROUP_DIM)
    sel = jnp.asarray(sel, F32).astype(BF16)
    return pl.pallas_call(
        _apply_kernel,
        grid=(N_SUPER, r // tr),
        in_specs=[tok_rows, rows,
                  pl.BlockSpec((1,) + lag_blocks.shape[1:], lambda j, ri: (j, 0, 0)),
                  pl.BlockSpec((1, k, S5_TILE), lambda j, ri: (j, 0, 0)),
                  pl.BlockSpec(sel.shape, lambda j, ri: (0, 0))],
        out_specs=tok_rows,
        out_shape=jax.ShapeDtypeStruct(u.shape, F32),
        scratch_shapes=[pltpu.VMEM((k, k), BF16), pltpu.VMEM((k, k), BF16), pltpu.VMEM((tr, k), BF16)],
        compiler_params=_cparams(("parallel", "arbitrary")),
        name=name,
    )(u, h, lag_blocks, q_flat, sel)


def _rec_kernel(sc_ref, sl_ref, a_ref, hc_ref, hl_ref, *, nb, n_ctx, n_lat):
    half = S5_TILE // 2
    npieces = REC_LANES // S5_TILE
    fwd_lane = lax.broadcasted_iota(jnp.int32, (1, half), 1) < S5_STATE
    coef = [(a_ref[0, :, p * S5_TILE:p * S5_TILE + half], a_ref[0, :, p * S5_TILE + half:(p + 1) * S5_TILE])
            for p in range(npieces)]

    def step(s_ref, h_ref, c, carry, forward):
        rows = pl.ds(pl.multiple_of(c * nb, nb), nb)
        out = []
        for p in range(npieces):
            hre, him = carry[2 * p], carry[2 * p + 1]
            are, aim = coef[p]
            re_sl = slice(p * S5_TILE, p * S5_TILE + half)
            im_sl = slice(p * S5_TILE + half, (p + 1) * S5_TILE)
            if forward:
                h_ref[0, rows, re_sl] = jnp.where(fwd_lane, hre, 0.0)
                h_ref[0, rows, im_sl] = jnp.where(fwd_lane, him, 0.0)
            else:
                h_ref[0, rows, re_sl] = jnp.where(fwd_lane, h_ref[0, rows, re_sl], hre)
                h_ref[0, rows, im_sl] = jnp.where(fwd_lane, h_ref[0, rows, im_sl], him)
            sre, sim = s_ref[0, rows, re_sl], s_ref[0, rows, im_sl]
            out.append(are * hre - aim * him + sre)
            out.append(are * him + aim * hre + sim)
        return tuple(out)

    zero = tuple(jnp.zeros((nb, half), F32) for _ in range(2 * npieces))
    carry = lax.fori_loop(0, n_ctx, lambda c, cr: step(sc_ref, hc_ref, c, cr, True), zero)
    lax.fori_loop(0, n_lat, lambda c, cr: step(sl_ref, hl_ref, c, cr, True), carry)
    carry = lax.fori_loop(0, n_ctx, lambda i, cr: step(sc_ref, hc_ref, n_ctx - 1 - i, cr, False), zero)
    lax.fori_loop(0, n_lat, lambda i, cr: step(sl_ref, hl_ref, n_lat - 1 - i, cr, False), carry)


def _chunk_recurrence(s_c, s_l, a16, nb):
    _, rc, lanes = s_c.shape
    rl = s_l.shape[1]
    spec = lambda r: pl.BlockSpec((1, r, REC_LANES), lambda j, i: (j, 0, i))
    return pl.pallas_call(
        functools.partial(_rec_kernel, nb=nb, n_ctx=rc // nb, n_lat=rl // nb),
        grid=(N_SUPER, lanes // REC_LANES),
        in_specs=[spec(rc), spec(rl), spec(1)],
        out_specs=[spec(rc), spec(rl)],
        out_shape=[jax.ShapeDtypeStruct(s_c.shape, F32), jax.ShapeDtypeStruct(s_l.shape, F32)],
        compiler_params=_cparams(("parallel", "parallel")),
        name="s5_recurrence",
    )(s_c, s_l, a16)


def _cmul(ar, ai, br, bi):
    return ar * br - ai * bi, ar * bi + ai * br


def _s5_matrices(lam_re, lam_im, log_step, b_re, b_im, c_re, c_im):
    t = S5_CHUNK
    hp = lax.Precision.HIGHEST
    taus = jnp.arange(t + 1, dtype=F32)[:, None, None]
    er, ei, wr, wi, a_re, a_im, ks = [], [], [], [], [], [], []
    for d in range(2):
        lr = jnp.minimum(lam_re[d].astype(F32), -1e-4)
        li = lam_im[d].astype(F32)
        dt = jnp.exp(log_step[d].astype(F32))[:, None]
        mag = jnp.exp(lr * dt * taus)
        pr, pi = mag * jnp.cos(li * dt * taus), mag * jnp.sin(li * dt * taus)
        den = lr * lr + li * li
        qr = ((pr[1] - 1.0) * lr + pi[1] * li) / den
        qi = (pi[1] * lr - (pr[1] - 1.0) * li) / den
        bbr, bbi = _cmul(qr[..., None], qi[..., None], b_re[d].astype(F32), b_im[d].astype(F32))
        e_r, e_i = _cmul(pr[..., None], pi[..., None], bbr[None], bbi[None])
        cr, ci = c_re[d].astype(F32), c_im[d].astype(F32)
        w_r, w_i = _cmul(cr[None], ci[None], pr[:, :, None, :], pi[:, :, None, :])
        k = (jnp.einsum('gmp,tgpn->tgmn', cr, e_r[:t], precision=hp)
             - jnp.einsum('gmp,tgpn->tgmn', ci, e_i[:t], precision=hp))
        er.append(e_r); ei.append(e_i); wr.append(w_r); wi.append(w_i); ks.append(k)
        a_re.append(pr[t]); a_im.append(pi[t])
    rows = lambda e: e.transpose(1, 0, 3, 2)
    p_mat = jnp.concatenate([rows(er[0][:t][::-1]), rows(er[1][:t]),
                             rows(ei[0][:t][::-1]), rows(ei[1][:t])], axis=-1)
    cols = lambda w: w.transpose(1, 3, 0, 2)
    q_mat = jnp.concatenate([cols(wr[0][1:]), cols(wr[1][1:][::-1]),
                             -cols(wi[0][1:]), -cols(wi[1][1:][::-1])], axis=1)
    a16 = jnp.concatenate([a_re[0], a_re[1], a_im[0], a_im[1]], axis=-1)

    gd, sl = S5_GROUP_DIM, SUPER_LANES

    def spread(a, nblk):
        r = jnp.arange(nblk * gd)[:, None]
        c = jnp.arange(nblk * sl)[None, :]
        sel = ((r // gd == c // sl) & (r % gd == c % gd)).astype(BF16)
        return jnp.einsum('jrk,kc->jrc', a.astype(BF16), sel, preferred_element_type=F32).astype(BF16)

    def same_group(row_group, col_group):
        return (row_group[:, None] == col_group[None, :]).astype(BF16)[None]

    nlag = 2 * t - 1
    kf = ks[0].transpose(0, 1, 3, 2)
    kb = ks[1].transpose(0, 1, 3, 2)
    klag = jnp.concatenate([kb[1:][::-1], (kf[0] + kb[0])[None], kf[1:]], axis=0)
    k_flat = klag.transpose(1, 2, 0, 3).reshape(N_SUPER, sl, nlag * gd)
    lag_blocks = spread(k_flat, nlag) * same_group(jnp.arange(sl) // gd, (jnp.arange(nlag * sl) % sl) // gd)
    p_flat = p_mat.reshape(N_SUPER, S5_SUPER, t, gd, S5_TILE).transpose(0, 2, 1, 3, 4).reshape(N_SUPER, SUPER_K, S5_TILE)
    q_flat = q_mat.reshape(N_SUPER, S5_SUPER * S5_TILE, S5_TILE)
    return lag_blocks, p_flat.astype(BF16), q_flat.astype(BF16), a16.reshape(N_SUPER, 1, SUPER_K)


def _s5_scan(s_l, s_c, mats):
    lag_blocks, p_flat, q_flat, a16 = mats
    b = s_l.shape[2]
    u_l = s_l.reshape(N_SUPER, -1, SUPER_LANES)
    u_c = s_c.reshape(N_SUPER, -1, SUPER_LANES)
    sum_l = _s5_summary(u_l, p_flat, "s5_summary")
    sum_c = _s5_summary(u_c, p_flat, "s5_summary_ctx")
    h_c, h_l = _chunk_recurrence(sum_c, sum_l, a16, b)
    y_l = _s5_apply(u_l, h_l, lag_blocks, q_flat, "s5_apply")
    y_c = _s5_apply(u_c, h_c, lag_blocks, q_flat, "s5_apply_ctx")
    return y_l.reshape(s_l.shape), y_c.reshape(s_c.shape)


def kernel(x, c, ctx, c_ctx, w_ada, b_ada, pre_g, post_g, w_in_even, w_out_even, na_rpb,
           w_in_odd, w_out_odd, sgu_w, sgu_b, sgu_g, s5_lam_re, s5_lam_im, s5_log_step,
           s5_b_re, s5_b_im, s5_c_re, s5_c_im, s5_d, glu_w, glu_b):
    b, l, d = x.shape
    lc = ctx.shape[1]
    depth = w_ada.shape[0]
    rows = l // GRID_W
    kh = min(NA_ROWS, rows)

    n_rows = -(-(b + 1) // 8) * 8
    cond = jnp.zeros((n_rows, d), F32).at[:b].set(c).at[b].set(c_ctx)
    mod = _ada_mod(cond, w_ada, b_ada)

    bd = _group_dft_mat()
    cm_l, sm_l = _dft_mats(l)
    cm_c, sm_c = _dft_mats(lc)

    w_in_e, w_in_o = w_in_even.astype(BF16), w_in_odd.astype(BF16)
    w_out_e, w_out_o = w_out_even.astype(BF16), w_out_odd.astype(BF16)
    sgu_ws, glu_ws = sgu_w.astype(BF16), glu_w.astype(BF16)
    n_odd = w_in_odd.shape[0]
    gw = SGU_WIDTH // SGU_GROUPS
    sgu_bs = [jnp.repeat(sgu_b[j].astype(F32).T, gw, axis=1) for j in range(n_odd)]
    tm_l, tm_c = min(ROW_TILE, l), min(ROW_TILE, lc)

    def modulation(i):
        lat = tuple(mod[i, :b, k * d:(k + 1) * d][:, None, :] for k in range(3))
        ctx_mod = jnp.broadcast_to(mod[i, b][None, None, :], (b, 1, 3 * d))
        return lat, tuple(ctx_mod[..., k * d:(k + 1) * d] for k in range(3))

    def in_part(i, seq, shift, scale):
        tm = min(ROW_TILE, seq)
        if i % 2 == 0:
            return ("even",) + _even_in_io(b, seq, d, tm, pre_g[i], scale, shift, w_in_e, i // 2, bd)
        j = i // 2
        return ("odd",) + _odd_in_io(b, seq, d, tm, pre_g[i], scale, shift, w_in_o, sgu_ws, j, sgu_bs[j], sgu_g[j])

    (shift, scale, gate), (shift_c, scale_c, gate_c) = modulation(0)
    xl, xc = x, ctx
    res_l = _projection(xl, None, in_part(0, l, shift, scale), "proj_in")
    res_c = _projection(xc, None, in_part(0, lc, shift_c, scale_c), "proj_in_ctx")
    for i in range(depth):
        last = i == depth - 1
        j = i // 2
        if i % 2 == 0:
            zc, zs, ga, q, k_, v, gb = res_l
            zc_c, zs_c, ga_c, q_c, k_c, v_c, gb_c = res_c
            a_l = _dft_mix(cm_l, sm_l, zc, zs, ga)
            n_l = _na_attention(q, k_, v, k_c, v_c, gb, _na_bias(na_rpb[j], kh))
            out_l = ("even",) + _even_out_io(d, tm_l, a_l, n_l, w_out_e, j, post_g[i], gate)
            if not last:
                a_c = _dft_mix(cm_c, sm_c, zc_c, zs_c, ga_c)
                n_c = _ctx_attention(q_c, k_c, v_c, gb_c)
                out_c = ("even",) + _even_out_io(d, tm_c, a_c, n_c, w_out_e, j, post_g[i], gate_c)
        else:
            mats = _s5_matrices(s5_lam_re[j], s5_lam_im[j], s5_log_step[j], s5_b_re[j], s5_b_im[j],
                                s5_c_re[j], s5_c_im[j])
            sg_l, s_l, gd_l = res_l
            sg_c, s_c, gd_c = res_c
            y_l, y_c = _s5_scan(s_l, s_c, mats)
            out_l = ("odd",) + _odd_out_io(d, tm_l, sg_l, y_l, s_l, gd_l, s5_d[j], glu_ws, glu_b[j], w_out_o, j,
                                           post_g[i], gate)
            if not last:
                out_c = ("odd",) + _odd_out_io(d, tm_c, sg_c, y_c, s_c, gd_c, s5_d[j], glu_ws, glu_b[j], w_out_o, j,
                                               post_g[i], gate_c)
        if last:
            (xl,) = _projection(xl, out_l, None, "proj_out")
        else:
            (shift, scale, gate), (shift_c, scale_c, gate_c) = modulation(i + 1)
            xl, *res_l = _projection(xl, out_l, in_part(i + 1, l, shift, scale), "proj_out_in")
            xc, *res_c = _projection(xc, out_c, in_part(i + 1, lc, shift_c, scale_c), "proj_out_in_ctx")
    return xl
```

```python
import functools

import numpy as np
import jax
import jax.numpy as jnp
from jax import lax
from jax.experimental import pallas as pl
from jax.experimental.pallas import tpu as pltpu

F32 = jnp.float32
BF16 = jnp.bfloat16

EPS = 1e-6
GRID_W = 64
FNET_WIDTH = 256
FNET_GROUP_DIM = 64
NA_WIDTH = 768
NA_HEAD_DIM = 64
NA_HEADS = 12
NA_ROWS = 8
NA_COLS = 16
SGU_CHUNK = 128
SGU_WIDTH = 512
SGU_GROUPS = 4
S5_WIDTH = 512
S5_GROUP_DIM = 16
S5_GROUPS = 32
S5_STATE = 64
S5_CHUNK = 16
S5_TILE = S5_CHUNK * S5_GROUP_DIM
LOG2E = float(np.log2(np.e))
QK_SCALE = NA_HEAD_DIM ** -0.5 * LOG2E

V7X_VMEM_BYTES = 64 * 1024 * 1024
VMEM_LIMIT = 48 * 1024 * 1024
ROW_TILE = 512


def _cparams(sem):
    return pltpu.CompilerParams(dimension_semantics=sem, vmem_limit_bytes=VMEM_LIMIT)


def _silu(x):
    return x * jax.nn.sigmoid(x)


def _dot(a, b):
    return jnp.dot(a, b, preferred_element_type=F32)


def _dot_nt(a, b):
    return lax.dot_general(a, b, (((1,), (1,)), ((), ())), preferred_element_type=F32)


def _ada_kernel(c_ref, w_ref, b_ref, o_ref):
    c = c_ref[...]
    o_ref[0] = jnp.dot(_silu(c), w_ref[0], preferred_element_type=F32,
                       precision=lax.Precision.HIGHEST) + b_ref[0]


def _ada_mod(cond, w_ada, b_ada):
    depth, d, n = w_ada.shape
    rows = cond.shape[0]
    tn = 1024
    return pl.pallas_call(
        _ada_kernel,
        grid=(depth, n // tn),
        in_specs=[pl.BlockSpec((rows, d), lambda i, j: (0, 0)),
                  pl.BlockSpec((1, d, tn), lambda i, j: (i, 0, j)),
                  pl.BlockSpec((1, 1, tn), lambda i, j: (i, 0, j))],
        out_specs=pl.BlockSpec((1, rows, tn), lambda i, j: (i, 0, j)),
        out_shape=jax.ShapeDtypeStruct((depth, rows, n), F32),
        compiler_params=_cparams(("parallel", "parallel")),
        name="ada_mod",
    )(cond, w_ada, b_ada.reshape(depth, 1, n))


def _norm_mod(x, g, scale, shift):
    y = x * lax.rsqrt(jnp.mean(x * x, axis=-1, keepdims=True) + EPS)
    return (y * g) * (1.0 + scale) + shift


def _post(y, post_g, gate, x):
    yn = y * lax.rsqrt(jnp.mean(y * y, axis=-1, keepdims=True) + EPS)
    return x + gate * (yn * post_g)


def _vec_spec(d):
    return pl.BlockSpec((1, 1, d), lambda bi, i: (bi, 0, 0))


def _const_spec(a):
    return pl.BlockSpec(a.shape, lambda bi, i: (0,) * a.ndim)


def _layer_spec(a, j):
    return pl.BlockSpec((1,) + a.shape[1:], lambda bi, i: (j,) + (0,) * (a.ndim - 1))


def _tok_spec(tm, width):
    return pl.BlockSpec((1, tm, width), lambda bi, i: (bi, i, 0))


def _even_in_body(x, g_ref, sc, sh, w_ref, bd_ref, z_scr, outs):
    zc_ref, zs_ref, ga_ref, q_ref, k_ref, v_ref, gb_ref = outs
    h = _norm_mod(x, g_ref[...], sc, sh).astype(BF16)
    fa = _dot(h, w_ref[0, :, 0:FNET_WIDTH]).astype(BF16)
    z = _dot(fa, bd_ref[...])
    planes = z.shape[1] // 128
    for p in range(planes):
        z_scr[p] = z[:, p * 128:(p + 1) * 128]
    half = z.shape[0] // 2
    for par in range(2):
        for p in range(planes):
            piece = z_scr[p, pl.ds(par, half, stride=2), :].astype(BF16)
            ref, q0 = (zc_ref, p) if p < planes // 2 else (zs_ref, p - planes // 2)
            ref[par, :, q0 * 128:(q0 + 1) * 128] = piece
    o = FNET_WIDTH
    ga_ref[0] = _dot(h, w_ref[0, :, o:o + FNET_WIDTH]).astype(BF16)
    o += FNET_WIDTH
    q_ref[0] = (_dot(h, w_ref[0, :, o:o + NA_WIDTH]) * QK_SCALE).astype(BF16)
    o += NA_WIDTH
    k_ref[0] = _dot(h, w_ref[0, :, o:o + NA_WIDTH]).astype(BF16)
    o += NA_WIDTH
    v_ref[0] = _dot(h, w_ref[0, :, o:o + NA_WIDTH]).astype(BF16)
    o += NA_WIDTH
    gb_ref[0] = _dot(h, w_ref[0, :, o:o + NA_WIDTH]).astype(BF16)


def _even_in_io(b, l, d, tm, pre_g, scale, shift, w, j, bd):
    zspec = pl.BlockSpec((2, tm // 2, FNET_WIDTH), lambda bi, i: (0, i, bi))
    zsd = jax.ShapeDtypeStruct((2, l // 2, b * FNET_WIDTH), BF16)
    sd = lambda width: jax.ShapeDtypeStruct((b, l, width), BF16)
    g2 = pre_g.reshape(1, d)
    widths = (FNET_WIDTH, NA_WIDTH, NA_WIDTH, NA_WIDTH, NA_WIDTH)
    return ([g2, scale, shift, w, bd],
            [_const_spec(g2), _vec_spec(d), _vec_spec(d), _layer_spec(w, j), _const_spec(bd)],
            [zspec, zspec] + [_tok_spec(tm, wd) for wd in widths],
            [zsd, zsd] + [sd(wd) for wd in widths],
            [pltpu.VMEM((2 * FNET_WIDTH // 128, tm, 128), F32)])


def _even_out_body(a_ref, n_ref, w_ref, pg_ref, gate, x):
    y = _dot(a_ref[0], w_ref[0, :FNET_WIDTH, :]) + _dot(n_ref[0], w_ref[0, FNET_WIDTH:, :])
    return _post(y, pg_ref[...], gate, x)


def _even_out_io(d, tm, a, n, w, j, post_g, gate):
    pg2 = post_g.reshape(1, d)
    return ([a, n, w, pg2, gate],
            [_tok_spec(tm, FNET_WIDTH), _tok_spec(tm, NA_WIDTH), _layer_spec(w, j), _const_spec(pg2), _vec_spec(d)])


S5_SUPER = 8
N_SUPER = S5_GROUPS // S5_SUPER
SUPER_LANES = S5_SUPER * S5_GROUP_DIM


def _super_spec(tm):
    return pl.BlockSpec((N_SUPER, tm // S5_CHUNK, 1, S5_CHUNK, SUPER_LANES), lambda bi, i: (0, i, bi, 0, 0))


def _super_shape(b, l):
    return jax.ShapeDtypeStruct((N_SUPER, l // S5_CHUNK, b, S5_CHUNK, SUPER_LANES), F32)


def _odd_in_body(x, g_ref, sc, sh, w_ref, ws_ref, bs_ref, sg_g_ref, outs):
    sg_ref, s_ref, gd_ref = outs
    tm = x.shape[0]
    h = _norm_mod(x, g_ref[...], sc, sh).astype(BF16)
    wd = SGU_WIDTH
    u = _dot(h, w_ref[0, :, 0:wd])
    v = _dot(h, w_ref[0, :, wd:2 * wd])
    gc = _dot(h, w_ref[0, :, 2 * wd:3 * wd])
    s = _dot(h, w_ref[0, :, 3 * wd:3 * wd + S5_WIDTH])
    for j in range(N_SUPER):
        sj = s[:, j * SUPER_LANES:(j + 1) * SUPER_LANES]
        s_ref[j, :, 0] = sj.reshape(tm // S5_CHUNK, S5_CHUNK, SUPER_LANES)
    gd_ref[0] = _dot(h, w_ref[0, :, 3 * wd + S5_WIDTH:]).astype(BF16)
    vc = v - jnp.mean(v, axis=-1, keepdims=True)
    vn = (vc * lax.rsqrt(jnp.mean(vc * vc, axis=-1, keepdims=True) + EPS) * sg_g_ref[...]).astype(BF16)
    gate = u * _silu(gc)
    gw = SGU_WIDTH // SGU_GROUPS
    for j in range(tm // SGU_CHUNK):
        rs = slice(j * SGU_CHUNK, (j + 1) * SGU_CHUNK)
        for g in range(SGU_GROUPS):
            cs = slice(g * gw, (g + 1) * gw)
            mixed = _dot(ws_ref[0, g], vn[rs, cs]) + bs_ref[:, cs]
            sg_ref[0, rs, cs] = (gate[rs, cs] * mixed).astype(BF16)


def _odd_in_io(b, l, d, tm, pre_g, scale, shift, w, ws, j, bs_full, sgu_g):
    g2 = pre_g.reshape(1, d)
    sg2 = sgu_g.reshape(1, SGU_WIDTH)
    sd = jax.ShapeDtypeStruct((b, l, SGU_WIDTH), BF16)
    return ([g2, scale, shift, w, ws, bs_full, sg2],
            [_const_spec(g2), _vec_spec(d), _vec_spec(d), _layer_spec(w, j), _layer_spec(ws, j), _const_spec(bs_full),
             _const_spec(sg2)],
            [_tok_spec(tm, SGU_WIDTH), _super_spec(tm), _tok_spec(tm, S5_WIDTH)],
            [sd, _super_shape(b, l), sd],
            [])


def _gelu_tanh(x):
    return 0.5 * x * (1.0 + jnp.tanh(float(np.sqrt(2.0 / np.pi)) * (x + 0.044715 * (x * x * x))))


def _odd_out_body(sg_ref, y_ref, s_ref, gd_ref, dsk_ref, wg_ref, bg_ref, w_ref, pg_ref, gate, x):
    tm = x.shape[0]
    tokens = lambda ref: jnp.concatenate([ref[j, :, 0].reshape(tm, SUPER_LANES) for j in range(N_SUPER)], axis=-1)
    ys = tokens(y_ref)
    s = tokens(s_ref)
    z = _gelu_tanh(ys + dsk_ref[...] * s)
    zz = _dot(z.astype(BF16), wg_ref[0]) + bg_ref[...]
    gd = gd_ref[0].astype(F32)
    ss = zz[:, :S5_WIDTH] * jax.nn.sigmoid(zz[:, S5_WIDTH:]) * _silu(gd)
    y = _dot(sg_ref[0], w_ref[0, :SGU_WIDTH, :]) + _dot(ss.astype(BF16), w_ref[0, SGU_WIDTH:, :])
    return _post(y, pg_ref[...], gate, x)


def _odd_out_io(d, tm, sg, y, s, gd, d_skip, wg, bg, w, j, post_g, gate):
    dsk = d_skip.reshape(1, S5_WIDTH).astype(F32)
    bg2 = bg.reshape(1, 2 * S5_WIDTH).astype(F32)
    pg2 = post_g.reshape(1, d)
    return ([sg, y, s, gd, dsk, wg, bg2, w, pg2, gate],
            [_tok_spec(tm, SGU_WIDTH), _super_spec(tm), _super_spec(tm), _tok_spec(tm, S5_WIDTH),
             _const_spec(dsk), _layer_spec(wg, j), _const_spec(bg2), _layer_spec(w, j), _const_spec(pg2), _vec_spec(d)])


_OUT_BODY = {"even": (_even_out_body, 5), "odd": (_odd_out_body, 10)}
_IN_BODY = {"even": (_even_in_body, 5, 7), "odd": (_odd_in_body, 7, 3)}


def _proj_kernel(*refs, out_kind, in_kind):
    pos = 0
    if out_kind is not None:
        body, n_ops = _OUT_BODY[out_kind]
        ops = refs[pos:pos + n_ops]
        pos += n_ops
    x_ref = refs[pos]
    pos += 1
    if in_kind is not None:
        in_body, n_in, n_out = _IN_BODY[in_kind]
        in_ops = refs[pos:pos + n_in]
        pos += n_in
    x = x_ref[0]
    if out_kind is not None:
        x = body(*ops[:-1], ops[-1][0], x)
        refs[pos][0] = x
        pos += 1
    if in_kind is not None:
        outs = refs[pos:pos + n_out]
        scratch = refs[pos + n_out:]
        g_ref, sc_ref, sh_ref = in_ops[:3]
        in_body(x, g_ref, sc_ref[0], sh_ref[0], *in_ops[3:], *scratch, outs)


def _projection(x, out_part, in_part, name):
    b, l, d = x.shape
    tm = min(ROW_TILE, l)
    operands, in_specs, out_specs, out_shapes, scratch = [], [], [], [], []
    if out_part is not None:
        operands += out_part[1]
        in_specs += out_part[2]
    operands.append(x)
    in_specs.append(_tok_spec(tm, d))
    if in_part is not None:
        operands += in_part[1]
        in_specs += in_part[2]
    if out_part is not None:
        out_specs.append(_tok_spec(tm, d))
        out_shapes.append(jax.ShapeDtypeStruct((b, l, d), F32))
    if in_part is not None:
        out_specs += in_part[3]
        out_shapes += in_part[4]
        scratch = in_part[5]
    kern = functools.partial(_proj_kernel, out_kind=out_part and out_part[0], in_kind=in_part and in_part[0])
    return pl.pallas_call(
        kern,
        grid=(b, l // tm),
        in_specs=in_specs,
        out_specs=out_specs,
        out_shape=out_shapes,
        scratch_shapes=scratch,
        compiler_params=_cparams(("parallel", "parallel")),
        name=name,
    )(*operands)


def _dft_kernel(ce_ref, se_ref, co_ref, so_ref, zc_ref, zs_ref, ga_ref, o_ref, acc_e, acc_o, *, nb, norm):
    kk = pl.program_id(1)

    @pl.when(kk == 0)
    def _():
        acc_e[...] = jnp.zeros_like(acc_e)
        acc_o[...] = jnp.zeros_like(acc_o)

    acc_e[...] += _dot(ce_ref[...], zc_ref[0]) + _dot(se_ref[...], zs_ref[0])
    acc_o[...] += _dot(co_ref[...], zc_ref[1]) + _dot(so_ref[...], zs_ref[1])

    @pl.when(kk == pl.num_programs(1) - 1)
    def _():
        for bi in range(nb):
            cs = slice(bi * FNET_WIDTH, (bi + 1) * FNET_WIDTH)
            e, o = acc_e[:, cs], acc_o[:, cs]
            for half, y in enumerate((e + o, e - o)):
                g = ga_ref[bi, half].astype(F32)
                o_ref[bi, half] = (y * norm * _silu(g)).astype(BF16)


def _dft_mix(cm, sm, zc, zs, ga):
    b, l, _ = ga.shape
    lh = l // 2
    tm = min(512, lh)
    tk = min(256, lh)
    nk = lh // tk
    nc = b * FNET_WIDTH
    kern = functools.partial(_dft_kernel, nb=b, norm=float((l * FNET_GROUP_DIM) ** -0.5))
    mat_e = pl.BlockSpec((tm, tk), lambda i, k: (i, k))
    mat_o = pl.BlockSpec((tm, tk), lambda i, k: (i, nk + k))
    zspec = pl.BlockSpec((2, tk, nc), lambda i, k: (0, k, 0))
    halves = pl.BlockSpec((b, 2, tm, FNET_WIDTH), lambda i, k: (0, 0, i, 0))
    out = pl.pallas_call(
        kern,
        grid=(lh // tm, nk),
        in_specs=[mat_e, mat_e, mat_o, mat_o, zspec, zspec, halves],
        out_specs=halves,
        out_shape=jax.ShapeDtypeStruct((b, 2, lh, FNET_WIDTH), BF16),
        scratch_shapes=[pltpu.VMEM((tm, nc), F32), pltpu.VMEM((tm, nc), F32)],
        compiler_params=_cparams(("parallel", "arbitrary")),
        name="dft_mix",
    )(cm, sm, cm, sm, zc, zs, ga.reshape(b, 2, lh, FNET_WIDTH))
    return out.reshape(b, l, FNET_WIDTH)


def _dft_gen_kernel(cb_ref, sb_ref, ca_ref, sa_ref, c_ref, s_ref):
    ca, sa = ca_ref[0], sa_ref[0]
    cb, sb = cb_ref[...], sb_ref[...]
    c_ref[...] = (ca * cb - sa * sb).astype(BF16)
    s_ref[...] = (-(sa * cb + ca * sb)).astype(BF16)


def _dft_mats(l):
    lh = l // 2
    tr = min(256, lh)
    kh = jnp.arange(lh, dtype=jnp.int32)
    k = jnp.concatenate([2 * kh, 2 * kh + 1])
    w = 2.0 * np.pi / l
    ang_b = ((jnp.arange(tr, dtype=jnp.int32)[:, None] * k[None, :]) % l).astype(F32) * w
    ang_a = ((jnp.arange(lh // tr, dtype=jnp.int32)[:, None] * tr * k[None, :]) % l).astype(F32) * w
    ca, sa = jnp.cos(ang_a)[:, None, :], jnp.sin(ang_a)[:, None, :]
    base = pl.BlockSpec((tr, l), lambda i: (0, 0))
    rowv = pl.BlockSpec((1, 1, l), lambda i: (i, 0, 0))
    out = pl.BlockSpec((tr, l), lambda i: (i, 0))
    sd = jax.ShapeDtypeStruct((lh, l), BF16)
    return pl.pallas_call(
        _dft_gen_kernel,
        grid=(lh // tr,),
        in_specs=[base, base, rowv, rowv],
        out_specs=[out, out],
        out_shape=[sd, sd],
        compiler_params=_cparams(("parallel",)),
        name="dft_gen",
    )(jnp.cos(ang_b), jnp.sin(ang_b), ca, sa)


def _group_dft_mat():
    n = FNET_GROUP_DIM
    j = np.arange(n)
    ang = 2.0 * np.pi * ((j[:, None] * j[None, :]) % n) / n
    eye = np.eye(FNET_WIDTH // n)
    mat = np.concatenate([np.kron(eye, np.cos(ang)), np.kron(eye, np.sin(ang))], axis=1)
    return jnp.asarray(mat, F32).astype(BF16)


HEADS_PER_TILE = 4
HEAD_TILE = HEADS_PER_TILE * NA_HEAD_DIM
HEAD_TILES = NA_WIDTH // HEAD_TILE


def _head_masks():
    row_head = lax.broadcasted_iota(jnp.int32, (HEAD_TILE, HEAD_TILE), 0) // GRID_W
    lane_head = lax.broadcasted_iota(jnp.int32, (HEAD_TILE, HEAD_TILE), 1) // NA_HEAD_DIM
    return row_head == lane_head, lax.broadcasted_iota(jnp.int32, (GRID_W, HEAD_TILE), 1) // NA_HEAD_DIM


def _stacked_queries(q4, own):
    return jnp.where(own, jnp.concatenate([q4] * HEADS_PER_TILE, axis=0), 0.0).astype(BF16)


def _own_head_blocks(o, out_head):
    o4 = o[0:GRID_W]
    for h in range(1, HEADS_PER_TILE):
        o4 = jnp.where(out_head == h, o[h * GRID_W:(h + 1) * GRID_W], o4)
    return o4


NA_ROWS_PER_STEP = 8


def _na_kernel(q_ref, k_ref, v_ref, kc_ref, vc_ref, gb_ref, bias_ref, o_ref, *, rows, kh, rps):
    nloc = kh * GRID_W
    own, out_head = _head_masks()
    for rr in range(rps):
        r = pl.program_id(1) * rps + rr
        r0 = jnp.clip(r - kh // 2, 0, rows - kh)
        start = pl.multiple_of(r0 * GRID_W, GRID_W)
        qs = slice(rr * GRID_W, (rr + 1) * GRID_W)
        for t in range(HEAD_TILES):
            cs = slice(t * HEAD_TILE, (t + 1) * HEAD_TILE)
            qm = _stacked_queries(q_ref[0, qs, cs], own)
            kl = k_ref[0, pl.ds(start, nloc), cs]
            vl = v_ref[0, pl.ds(start, nloc), cs]
            var = r0 - r + (NA_ROWS - 1)
            bias = jnp.concatenate([bias_ref[t, var + 2 * m_] for m_ in range(kh // 2)], axis=-1)
            s_loc = _dot_nt(qm, kl) + bias
            s_ctx = _dot_nt(qm, kc_ref[0, :, cs])
            m = jnp.maximum(jnp.max(s_loc, axis=-1, keepdims=True), jnp.max(s_ctx, axis=-1, keepdims=True))
            p_loc = jnp.exp2(s_loc - m)
            p_ctx = jnp.exp2(s_ctx - m)
            den = jnp.sum(p_loc, axis=-1, keepdims=True) + jnp.sum(p_ctx, axis=-1, keepdims=True)
            o = (_dot(p_loc.astype(BF16), vl) + _dot(p_ctx.astype(BF16), vc_ref[0, :, cs])) / den
            g = gb_ref[0, qs, cs].astype(F32)
            o_ref[0, qs, cs] = (_own_head_blocks(o, out_head) * _silu(g)).astype(BF16)


def _na_attention(q, k, v, kc, vc, gb, bias):
    b, l, w = q.shape
    lc = kc.shape[1]
    rows = l // GRID_W
    kh = min(NA_ROWS, rows)
    rps = NA_ROWS_PER_STEP

    row = pl.BlockSpec((1, rps * GRID_W, w), lambda bi, i: (bi, i, 0))
    full = lambda n: pl.BlockSpec((1, n, w), lambda bi, i: (bi, 0, 0))
    table = pl.BlockSpec(bias.shape, lambda bi, i: (0, 0, 0, 0), pipeline_mode=pl.Buffered(1))
    kern = functools.partial(_na_kernel, rows=rows, kh=kh, rps=rps)
    return pl.pallas_call(
        kern,
        grid=(b, rows // rps),
        in_specs=[row, full(l), full(l), full(lc), full(lc), row, table],
        out_specs=row,
        out_shape=jax.ShapeDtypeStruct((b, l, w), BF16),
        compiler_params=_cparams(("parallel", "arbitrary")),
        name="na_attention",
    )(q, k, v, kc, vc, gb, bias)


def _na_bias(rpb, kh):
    assert kh % 2 == 0
    cq = np.arange(GRID_W)
    ck = np.arange(GRID_W)
    c0 = np.clip(cq - NA_COLS // 2, 0, GRID_W - NA_COLS)
    col_in = (ck[None, :] >= c0[:, None]) & (ck[None, :] < c0[:, None] + NA_COLS)
    dc = np.clip(ck[None, :] - cq[:, None] + (NA_COLS - 1), 0, 2 * NA_COLS - 2)
    onehot = (dc.reshape(1, -1) == np.arange(2 * NA_COLS - 1)[:, None]).astype(np.float32)
    t = jnp.einsum('hrc,cx->hrx', rpb.astype(F32) * LOG2E, jnp.asarray(onehot), precision=lax.Precision.HIGHEST)
    n_dr = 2 * NA_ROWS - 1
    t = t.reshape(NA_HEADS, n_dr, GRID_W, GRID_W)
    t = jnp.where(col_in[None, None], t, -1e30)
    pairs = jnp.concatenate([t[:, :-1], t[:, 1:]], axis=-1)
    pairs = pairs.reshape(HEAD_TILES, HEADS_PER_TILE, n_dr - 1, GRID_W, 2 * GRID_W).transpose(0, 2, 1, 3, 4)
    return pairs.reshape(HEAD_TILES, n_dr - 1, HEAD_TILE, 2 * GRID_W)


def _ctx_attn_kernel(q_ref, k_ref, v_ref, gb_ref, o_ref):
    own, out_head = _head_masks()
    for t in range(HEAD_TILES):
        cs = slice(t * HEAD_TILE, (t + 1) * HEAD_TILE)
        for q0 in range(0, q_ref.shape[1], GRID_W):
            qs = slice(q0, q0 + GRID_W)
            s = _dot_nt(_stacked_queries(q_ref[0, qs, cs], own), k_ref[0, :, cs])
            p = jnp.exp2(s - jnp.max(s, axis=-1, keepdims=True))
            o = _dot(p.astype(BF16), v_ref[0, :, cs]) / jnp.sum(p, axis=-1, keepdims=True)
            g = gb_ref[0, qs, cs].astype(F32)
            o_ref[0, qs, cs] = (_own_head_blocks(o, out_head) * _silu(g)).astype(BF16)


def _ctx_attention(q, k, v, gb):
    b, lc, w = q.shape
    spec = pl.BlockSpec((1, lc, w), lambda bi: (bi, 0, 0))
    return pl.pallas_call(
        _ctx_attn_kernel,
        grid=(b,),
        in_specs=[spec, spec, spec, spec],
        out_specs=spec,
        out_shape=jax.ShapeDtypeStruct((b, lc, w), BF16),
        compiler_params=_cparams(("parallel",)),
        name="ctx_attention",
    )(q, k, v, gb)


SUPER_K = S5_CHUNK * SUPER_LANES
SG_ROW_TILE = 512
SG_APPLY_ROWS = 256
REC_LANES = 1024


def _gather_chunks(u_ref, u_scr):
    rows = u_scr.shape[0]
    for s in range(S5_CHUNK):
        u_scr[:, s * SUPER_LANES:(s + 1) * SUPER_LANES] = u_ref[0, pl.ds(s, rows, stride=S5_CHUNK), :].astype(BF16)


def _summary_kernel(u_ref, p_ref, o_ref, w_scr, u_scr):
    @pl.when(pl.program_id(1) == 0)
    def _():
        row_group = (lax.broadcasted_iota(jnp.int32, (SUPER_K, S5_TILE), 0) % SUPER_LANES) // S5_GROUP_DIM
        p = p_ref[0]
        for h in range(S5_SUPER):
            w_scr[:, h * S5_TILE:(h + 1) * S5_TILE] = jnp.where(row_group == h, p, 0.0).astype(BF16)

    _gather_chunks(u_ref, u_scr)
    o_ref[0] = _dot(u_scr[...], w_scr[...])


def _s5_summary(u, p_flat, name):
    k = SUPER_K
    r = u.shape[1] // S5_CHUNK
    tr = min(SG_ROW_TILE, r)
    return pl.pallas_call(
        _summary_kernel,
        grid=(N_SUPER, r // tr),
        in_specs=[pl.BlockSpec((1, tr * S5_CHUNK, SUPER_LANES), lambda j, ri: (j, ri, 0)),
                  pl.BlockSpec((1, k, S5_TILE), lambda j, ri: (j, 0, 0))],
        out_specs=pl.BlockSpec((1, tr, k), lambda j, ri: (j, ri, 0)),
        out_shape=jax.ShapeDtypeStruct((N_SUPER, r, k), F32),
        scratch_shapes=[pltpu.VMEM((k, k), BF16), pltpu.VMEM((tr, k), BF16)],
        compiler_params=_cparams(("parallel", "arbitrary")),
        name=name,
    )(u, p_flat)


def _apply_kernel(u_ref, h_ref, lag_ref, q_ref, sel_ref, o_ref, w_scr, q_scr, u_scr):
    @pl.when(pl.program_id(1) == 0)
    def _():
        for s in range(S5_CHUNK):
            lo = SUPER_LANES * (S5_CHUNK - 1 - s)
            w_scr[s * SUPER_LANES:(s + 1) * SUPER_LANES, :] = lag_ref[0, :, lo:lo + SUPER_K]
        col_group = (lax.broadcasted_iota(jnp.int32, (S5_TILE, SUPER_K), 1) % SUPER_LANES) // S5_GROUP_DIM
        for g in range(S5_SUPER):
            rs = slice(g * S5_TILE, (g + 1) * S5_TILE)
            q_scr[rs, :] = jnp.where(col_group == g, _dot(q_ref[0, rs, :], sel_ref[...]), 0.0).astype(BF16)

    _gather_chunks(u_ref, u_scr)
    acc = _dot(u_scr[...], w_scr[...]) + _dot(h_ref[0].astype(BF16), q_scr[...])
    rows = u_scr.shape[0]
    for s in range(S5_CHUNK):
        o_ref[0, pl.ds(s, rows, stride=S5_CHUNK), :] = acc[:, s * SUPER_LANES:(s + 1) * SUPER_LANES]


def _s5_apply(u, h, lag_blocks, q_flat, name):
    k = SUPER_K
    r = h.shape[1]
    tr = min(SG_APPLY_ROWS, r)
    rows = pl.BlockSpec((1, tr, k), lambda j, ri: (j, ri, 0))
    tok_rows = pl.BlockSpec((1, tr * S5_CHUNK, SUPER_LANES), lambda j, ri: (j, ri, 0))
    ri_, ci_ = np.arange(S5_TILE)[:, None], np.arange(k)[None, :]
    sel = (ri_ // S5_GROUP_DIM == ci_ // SUPER_LANES) & (ri_ % S5_GROUP_DIM == ci_ % S5_GROUP_DIM)
    sel = jnp.asarray(sel, F32).astype(BF16)
    return pl.pallas_call(
        _apply_kernel,
        grid=(N_SUPER, r // tr),
        in_specs=[tok_rows, rows,
                  pl.BlockSpec((1,) + lag_blocks.shape[1:], lambda j, ri: (j, 0, 0)),
                  pl.BlockSpec((1, k, S5_TILE), lambda j, ri: (j, 0, 0)),
                  pl.BlockSpec(sel.shape, lambda j, ri: (0, 0))],
        out_specs=tok_rows,
        out_shape=jax.ShapeDtypeStruct(u.shape, F32),
        scratch_shapes=[pltpu.VMEM((k, k), BF16), pltpu.VMEM((k, k), BF16), pltpu.VMEM((tr, k), BF16)],
        compiler_params=_cparams(("parallel", "arbitrary")),
        name=name,
    )(u, h, lag_blocks, q_flat, sel)


def _rec_kernel(sc_ref, sl_ref, a_ref, hc_ref, hl_ref, *, nb, n_ctx, n_lat):
    half = S5_TILE // 2
    npieces = REC_LANES // S5_TILE
    fwd_lane = lax.broadcasted_iota(jnp.int32, (1, half), 1) < S5_STATE
    coef = [(a_ref[0, :, p * S5_TILE:p * S5_TILE + half], a_ref[0, :, p * S5_TILE + half:(p + 1) * S5_TILE])
            for p in range(npieces)]

    def step(s_ref, h_ref, n, i, carry, first_write):
        rows_f = pl.ds(pl.multiple_of(i * nb, nb), nb)
        rows_b = pl.ds(pl.multiple_of((n - 1 - i) * nb, nb), nb)
        out = []
        for p in range(npieces):
            are, aim = coef[p]
            for part, sl in enumerate((slice(p * S5_TILE, p * S5_TILE + half),
                                       slice(p * S5_TILE + half, (p + 1) * S5_TILE))):
                h = carry[2 * p + part]
                if first_write:
                    h_ref[0, rows_f, sl] = jnp.where(fwd_lane, h, 0.0)
                    h_ref[0, rows_b, sl] = jnp.where(fwd_lane, 0.0, h)
                else:
                    h_ref[0, rows_f, sl] = jnp.where(fwd_lane, h, h_ref[0, rows_f, sl])
                    h_ref[0, rows_b, sl] = jnp.where(fwd_lane, h_ref[0, rows_b, sl], h)
            hre, him = carry[2 * p], carry[2 * p + 1]
            re_sl = slice(p * S5_TILE, p * S5_TILE + half)
            im_sl = slice(p * S5_TILE + half, (p + 1) * S5_TILE)
            sre = jnp.where(fwd_lane, s_ref[0, rows_f, re_sl], s_ref[0, rows_b, re_sl])
            sim = jnp.where(fwd_lane, s_ref[0, rows_f, im_sl], s_ref[0, rows_b, im_sl])
            out.append(are * hre - aim * him + sre)
            out.append(are * him + aim * hre + sim)
        return tuple(out)

    carry = tuple(jnp.zeros((nb, half), F32) for _ in range(2 * npieces))
    for s_ref, h_ref, n in ((sc_ref, hc_ref, n_ctx), (sl_ref, hl_ref, n_lat)):
        mid = n // 2
        carry = lax.fori_loop(0, mid, lambda i, cr: step(s_ref, h_ref, n, i, cr, True), carry)
        carry = lax.fori_loop(mid, n, lambda i, cr: step(s_ref, h_ref, n, i, cr, False), carry)


def _chunk_recurrence(s_c, s_l, a16, nb):
    _, rc, lanes = s_c.shape
    rl = s_l.shape[1]
    assert (rc // nb) % 2 == 0 and (rl // nb) % 2 == 0
    spec = lambda r: pl.BlockSpec((1, r, REC_LANES), lambda j, i: (j, 0, i))
    return pl.pallas_call(
        functools.partial(_rec_kernel, nb=nb, n_ctx=rc // nb, n_lat=rl // nb),
        grid=(N_SUPER, lanes // REC_LANES),
        in_specs=[spec(rc), spec(rl), spec(1)],
        out_specs=[spec(rc), spec(rl)],
        out_shape=[jax.ShapeDtypeStruct(s_c.shape, F32), jax.ShapeDtypeStruct(s_l.shape, F32)],
        compiler_params=_cparams(("parallel", "parallel")),
        name="s5_recurrence",
    )(s_c, s_l, a16)


def _cmul(ar, ai, br, bi):
    return ar * br - ai * bi, ar * bi + ai * br


def _s5_matrices(lam_re, lam_im, log_step, b_re, b_im, c_re, c_im):
    t = S5_CHUNK
    hp = lax.Precision.HIGHEST
    taus = jnp.arange(t + 1, dtype=F32)[:, None, None]
    er, ei, wr, wi, a_re, a_im, ks = [], [], [], [], [], [], []
    for d in range(2):
        lr = jnp.minimum(lam_re[d].astype(F32), -1e-4)
        li = lam_im[d].astype(F32)
        dt = jnp.exp(log_step[d].astype(F32))[:, None]
        mag = jnp.exp(lr * dt * taus)
        pr, pi = mag * jnp.cos(li * dt * taus), mag * jnp.sin(li * dt * taus)
        den = lr * lr + li * li
        qr = ((pr[1] - 1.0) * lr + pi[1] * li) / den
        qi = (pi[1] * lr - (pr[1] - 1.0) * li) / den
        bbr, bbi = _cmul(qr[..., None], qi[..., None], b_re[d].astype(F32), b_im[d].astype(F32))
        e_r, e_i = _cmul(pr[..., None], pi[..., None], bbr[None], bbi[None])
        cr, ci = c_re[d].astype(F32), c_im[d].astype(F32)
        w_r, w_i = _cmul(cr[None], ci[None], pr[:, :, None, :], pi[:, :, None, :])
        k = (jnp.einsum('gmp,tgpn->tgmn', cr, e_r[:t], precision=hp)
             - jnp.einsum('gmp,tgpn->tgmn', ci, e_i[:t], precision=hp))
        er.append(e_r); ei.append(e_i); wr.append(w_r); wi.append(w_i); ks.append(k)
        a_re.append(pr[t]); a_im.append(pi[t])
    rows = lambda e: e.transpose(1, 0, 3, 2)
    p_mat = jnp.concatenate([rows(er[0][:t][::-1]), rows(er[1][:t]),
                             rows(ei[0][:t][::-1]), rows(ei[1][:t])], axis=-1)
    cols = lambda w: w.transpose(1, 3, 0, 2)
    q_mat = jnp.concatenate([cols(wr[0][1:]), cols(wr[1][1:][::-1]),
                             -cols(wi[0][1:]), -cols(wi[1][1:][::-1])], axis=1)
    a16 = jnp.concatenate([a_re[0], a_re[1], a_im[0], a_im[1]], axis=-1)

    gd, sl = S5_GROUP_DIM, SUPER_LANES

    def spread(a, nblk):
        r = jnp.arange(nblk * gd)[:, None]
        c = jnp.arange(nblk * sl)[None, :]
        sel = ((r // gd == c // sl) & (r % gd == c % gd)).astype(BF16)
        return jnp.einsum('jrk,kc->jrc', a.astype(BF16), sel, preferred_element_type=F32).astype(BF16)

    def same_group(row_group, col_group):
        return (row_group[:, None] == col_group[None, :]).astype(BF16)[None]

    nlag = 2 * t - 1
    kf = ks[0].transpose(0, 1, 3, 2)
    kb = ks[1].transpose(0, 1, 3, 2)
    klag = jnp.concatenate([kb[1:][::-1], (kf[0] + kb[0])[None], kf[1:]], axis=0)
    k_flat = klag.transpose(1, 2, 0, 3).reshape(N_SUPER, sl, nlag * gd)
    lag_blocks = spread(k_flat, nlag) * same_group(jnp.arange(sl) // gd, (jnp.arange(nlag * sl) % sl) // gd)
    p_flat = p_mat.reshape(N_SUPER, S5_SUPER, t, gd, S5_TILE).transpose(0, 2, 1, 3, 4).reshape(N_SUPER, SUPER_K, S5_TILE)
    q_flat = q_mat.reshape(N_SUPER, S5_SUPER * S5_TILE, S5_TILE)
    return lag_blocks, p_flat.astype(BF16), q_flat.astype(BF16), a16.reshape(N_SUPER, 1, SUPER_K)


def _s5_scan(s_l, s_c, mats):
    lag_blocks, p_flat, q_flat, a16 = mats
    b = s_l.shape[2]
    u_l = s_l.reshape(N_SUPER, -1, SUPER_LANES)
    u_c = s_c.reshape(N_SUPER, -1, SUPER_LANES)
    sum_l = _s5_summary(u_l, p_flat, "s5_summary")
    sum_c = _s5_summary(u_c, p_flat, "s5_summary_ctx")
    h_c, h_l = _chunk_recurrence(sum_c, sum_l, a16, b)
    y_l = _s5_apply(u_l, h_l, lag_blocks, q_flat, "s5_apply")
    y_c = _s5_apply(u_c, h_c, lag_blocks, q_flat, "s5_apply_ctx")
    return y_l.reshape(s_l.shape), y_c.reshape(s_c.shape)


def kernel(x, c, ctx, c_ctx, w_ada, b_ada, pre_g, post_g, w_in_even, w_out_even, na_rpb,
           w_in_odd, w_out_odd, sgu_w, sgu_b, sgu_g, s5_lam_re, s5_lam_im, s5_log_step,
           s5_b_re, s5_b_im, s5_c_re, s5_c_im, s5_d, glu_w, glu_b):
    b, l, d = x.shape
    lc = ctx.shape[1]
    depth = w_ada.shape[0]
    rows = l // GRID_W
    kh = min(NA_ROWS, rows)

    n_rows = -(-(b + 1) // 8) * 8
    cond = jnp.zeros((n_rows, d), F32).at[:b].set(c).at[b].set(c_ctx)
    mod = _ada_mod(cond, w_ada, b_ada)

    bd = _group_dft_mat()
    cm_l, sm_l = _dft_mats(l)
    cm_c, sm_c = _dft_mats(lc)

    w_in_e, w_in_o = w_in_even.astype(BF16), w_in_odd.astype(BF16)
    w_out_e, w_out_o = w_out_even.astype(BF16), w_out_odd.astype(BF16)
    sgu_ws, glu_ws = sgu_w.astype(BF16), glu_w.astype(BF16)
    n_odd = w_in_odd.shape[0]
    gw = SGU_WIDTH // SGU_GROUPS
    sgu_bs = [jnp.repeat(sgu_b[j].astype(F32).T, gw, axis=1) for j in range(n_odd)]
    tm_l, tm_c = min(ROW_TILE, l), min(ROW_TILE, lc)

    def modulation(i):
        lat = tuple(mod[i, :b, k * d:(k + 1) * d][:, None, :] for k in range(3))
        ctx_mod = jnp.broadcast_to(mod[i, b][None, None, :], (b, 1, 3 * d))
        return lat, tuple(ctx_mod[..., k * d:(k + 1) * d] for k in range(3))

    def in_part(i, seq, shift, scale):
        tm = min(ROW_TILE, seq)
        if i % 2 == 0:
            return ("even",) + _even_in_io(b, seq, d, tm, pre_g[i], scale, shift, w_in_e, i // 2, bd)
        j = i // 2
        return ("odd",) + _odd_in_io(b, seq, d, tm, pre_g[i], scale, shift, w_in_o, sgu_ws, j, sgu_bs[j], sgu_g[j])

    (shift, scale, gate), (shift_c, scale_c, gate_c) = modulation(0)
    xl, xc = x, ctx
    res_l = _projection(xl, None, in_part(0, l, shift, scale), "proj_in")
    res_c = _projection(xc, None, in_part(0, lc, shift_c, scale_c), "proj_in_ctx")
    for i in range(depth):
        last = i == depth - 1
        j = i // 2
        if i % 2 == 0:
            zc, zs, ga, q, k_, v, gb = res_l
            zc_c, zs_c, ga_c, q_c, k_c, v_c, gb_c = res_c
            a_l = _dft_mix(cm_l, sm_l, zc, zs, ga)
            n_l = _na_attention(q, k_, v, k_c, v_c, gb, _na_bias(na_rpb[j], kh))
            out_l = ("even",) + _even_out_io(d, tm_l, a_l, n_l, w_out_e, j, post_g[i], gate)
            if not last:
                a_c = _dft_mix(cm_c, sm_c, zc_c, zs_c, ga_c)
                n_c = _ctx_attention(q_c, k_c, v_c, gb_c)
                out_c = ("even",) + _even_out_io(d, tm_c, a_c, n_c, w_out_e, j, post_g[i], gate_c)
        else:
            mats = _s5_matrices(s5_lam_re[j], s5_lam_im[j], s5_log_step[j], s5_b_re[j], s5_b_im[j],
                                s5_c_re[j], s5_c_im[j])
            sg_l, s_l, gd_l = res_l
            sg_c, s_c, gd_c = res_c
            y_l, y_c = _s5_scan(s_l, s_c, mats)
            out_l = ("odd",) + _odd_out_io(d, tm_l, sg_l, y_l, s_l, gd_l, s5_d[j], glu_ws, glu_b[j], w_out_o, j,
                                           post_g[i], gate)
            if not last:
                out_c = ("odd",) + _odd_out_io(d, tm_c, sg_c, y_c, s_c, gd_c, s5_d[j], glu_ws, glu_b[j], w_out_o, j,
                                               post_g[i], gate_c)
        if last:
            (xl,) = _projection(xl, out_l, None, "proj_out")
        else:
            (shift, scale, gate), (shift_c, scale_c, gate_c) = modulation(i + 1)
            xl, *res_l = _projection(xl, out_l, in_part(i + 1, l, shift, scale), "proj_out_in")
            xc, *res_c = _projection(xc, out_c, in_part(i + 1, lc, shift_c, scale_c), "proj_out_in_ctx")
    return xl
```
